```python
import math
import jax
import jax.numpy as jnp
from jax import lax
import numpy as np

D_MODEL = 1024
BATCH = 32
SEQ = 256
DEPTH = 2
DEC_BATCH = 8
DEC_SEQ = 1024
PAST_LEN = 512

F32 = jnp.float32
GRID_W = 64
NORM_EPS = 1e-6
CHUNK = 64
CONV_W = 5

SSD_HEAD_DIM = 64
SSD_HEADS = D_MODEL // SSD_HEAD_DIM
SSD_INNER = SSD_HEADS * SSD_HEAD_DIM
SSD_GROUPS = 2
SSD_STATE = 128
SSD_CONV_DIM = SSD_INNER + 2 * SSD_GROUPS * SSD_STATE

GDN_HEAD_DIM = 128
GDN_HEADS = D_MODEL // GDN_HEAD_DIM
GDN_INNER = GDN_HEADS * GDN_HEAD_DIM

ATT_HEAD_DIM = 128
ATT_HEADS = D_MODEL // ATT_HEAD_DIM
ATT_KV_HEADS = 2
ATT_GROUP = ATT_HEADS // ATT_KV_HEADS
ATT_INNER = ATT_HEADS * ATT_HEAD_DIM
ATT_KV_INNER = ATT_KV_HEADS * ATT_HEAD_DIM
WINDOW = 128
QBLOCK = 128
ROPE_BASE = 10000.0

RWKV_HEAD_DIM = 64
RWKV_HEADS = D_MODEL // RWKV_HEAD_DIM
RWKV_INNER = RWKV_HEADS * RWKV_HEAD_DIM
RWKV_W_RANK = 64
RWKV_A_RANK = 64
RWKV_G_RANK = 128
RWKV_LN_EPS = 64e-5

FFN_HIDDEN = ((8 * D_MODEL // 3 + 255) // 256) * 256

EV_SPLITS = (SSD_INNER, SSD_CONV_DIM, 2 * SSD_HEADS, 3 * GDN_INNER, GDN_INNER, 2 * GDN_HEADS, 2 * GDN_HEADS)
EV_COLS = SSD_INNER + SSD_CONV_DIM + 2 * SSD_HEADS + 4 * GDN_INNER + 4 * GDN_HEADS
EV_MIX = SSD_INNER + GDN_INNER
RWKV_SPLITS = (RWKV_INNER, RWKV_INNER, RWKV_INNER, 2 * RWKV_W_RANK, 2 * RWKV_A_RANK, RWKV_G_RANK)
RWKV_COLS = 3 * RWKV_INNER + 2 * RWKV_W_RANK + 2 * RWKV_A_RANK + RWKV_G_RANK
OD_SPLITS = (ATT_INNER, ATT_KV_INNER, ATT_KV_INNER, RWKV_COLS)
OD_COLS = ATT_INNER + 2 * ATT_KV_INNER + RWKV_COLS
OD_MIX = ATT_INNER + RWKV_INNER

kernel_name = 'hybrid_diffusion_trunk_step'


def _split(x, sizes):
    return jnp.split(x, np.cumsum(sizes)[:-1].tolist(), axis=-1)


def _flip(t):
    return jnp.flip(t, axis=1)


def rms_norm(x, g, eps=NORM_EPS):
    xf = x.astype(F32)
    y = xf * lax.rsqrt(jnp.mean(xf * xf, axis=-1, keepdims=True) + eps)
    return (y * g.astype(F32)).astype(x.dtype)


def l2_normalize(x):
    xf = x.astype(F32)
    return (xf * lax.rsqrt(jnp.sum(xf * xf, axis=-1, keepdims=True) + NORM_EPS)).astype(x.dtype)


def adaln(cvec, w, b):
    return _split(jax.nn.silu(cvec) @ w + b, (D_MODEL,) * 6)


def modulate(h, shift, scale):
    return h * (1 + scale) + shift


def swiglu(h, w1, w2):
    gate, up = _split(h @ w1, (FFN_HIDDEN, FFN_HIDDEN))
    return (jax.nn.silu(gate) * up) @ w2


def centred_dwconv(x, w, b=None):
    c = x.shape[-1]
    pad = CONV_W // 2
    y = lax.conv_general_dilated(x, w[:, None, :].astype(x.dtype), window_strides=(1,),
                                 padding=[(pad, pad)], dimension_numbers=('NWC', 'WIO', 'NWC'),
                                 feature_group_count=c)
    if b is not None:
        y = y + b
    return y


def centred_shift(x):
    xp = jnp.pad(x, ((0, 0), (1, 1), (0, 0)))
    return 0.5 * (xp[:, :-2] + xp[:, 2:])


def axial_rope(x):
    t, dh = x.shape[1], x.shape[-1]
    rows = t // GRID_W
    row = jnp.repeat(jnp.arange(rows), GRID_W).astype(F32)
    col = jnp.tile(jnp.arange(GRID_W), rows).astype(F32)
    n_freq = dh // 4
    inv = ROPE_BASE ** (-jnp.arange(n_freq, dtype=F32) / n_freq)
    ang = jnp.concatenate([row[:, None] * inv, col[:, None] * inv], axis=-1)
    cos = jnp.cos(ang)[None, :, None, :]
    sin = jnp.sin(ang)[None, :, None, :]
    xf = x.astype(F32)
    x1, x2 = xf[..., :dh // 2], xf[..., dh // 2:]
    return jnp.concatenate([x1 * cos - x2 * sin, x2 * cos + x1 * sin], axis=-1).astype(x.dtype)


def softmax_with_sink(s, sink):
    sk = jnp.broadcast_to(sink.astype(F32)[None, :, :, None, None], s.shape[:-1] + (1,))
    return jax.nn.softmax(jnp.concatenate([s, sk], axis=-1), axis=-1)[..., :-1]


def ssd_scan(x, dt, a_neg, bm, cm, s0):
    bsz, t, h, p = x.shape
    g, n = bm.shape[2], bm.shape[3]
    r = h // g
    nc = t // CHUNK
    xc = x.astype(F32).reshape(bsz, nc, CHUNK, g, r, p)
    dtc = dt.astype(F32).reshape(bsz, nc, CHUNK, g, r)
    bc = bm.astype(F32).reshape(bsz, nc, CHUNK, g, n)
    cc = cm.astype(F32).reshape(bsz, nc, CHUNK, g, n)
    acum = jnp.cumsum(dtc * a_neg.astype(F32).reshape(g, r), axis=2)
    causal = jnp.tril(jnp.ones((CHUNK, CHUNK), bool))
    seg = acum[:, :, :, None] - acum[:, :, None, :]
    decay = jnp.exp(jnp.where(causal[:, :, None, None], seg, -jnp.inf))
    cb = jnp.einsum('bclgn,bcsgn->bclsg', cc, bc)
    xdt = dtc[..., None] * xc
    y_intra = jnp.einsum('bclsgr,bcsgrp->bclgrp', cb[..., None] * decay, xdt)
    a_last = acum[:, :, -1]
    to_end = jnp.exp(a_last[:, :, None] - acum)
    chunk_states = jnp.einsum('bclgn,bclgrp->bcgrpn', bc, to_end[..., None] * xdt)

    def step(s, inp):
        dl, cs = inp
        return jnp.exp(dl)[..., None, None] * s + cs, s

    s_fin, s_in = lax.scan(step, s0.astype(F32).reshape(bsz, g, r, p, n),
                           (jnp.moveaxis(a_last, 1, 0), jnp.moveaxis(chunk_states, 1, 0)))
    s_in = jnp.moveaxis(s_in, 0, 1)
    y_inter = jnp.einsum('bclgn,bcgrpn->bclgrp', cc, s_in) * jnp.exp(acum)[..., None]
    y = (y_intra + y_inter).reshape(bsz, t, h, p)
    return y.astype(x.dtype), s_fin.reshape(bsz, h, p, n)


def gdn_scan(q, k, v, g, beta, s0):
    bsz, t, h, _ = q.shape
    vd = v.shape[-1]
    nc = t // CHUNK

    def to_chunks(u):
        u = u.astype(F32).reshape((bsz, nc, CHUNK) + u.shape[2:])
        return jnp.moveaxis(u, 3, 2)

    qc, kc, vc, gc, bc = to_chunks(q), to_chunks(k), to_chunks(v), to_chunks(g), to_chunks(beta)
    gam = jnp.cumsum(gc, axis=-1)
    causal = jnp.tril(jnp.ones((CHUNK, CHUNK), bool))
    strict = jnp.tril(jnp.ones((CHUNK, CHUNK), bool), k=-1)
    decay = jnp.exp(jnp.where(causal, gam[..., :, None] - gam[..., None, :], -jnp.inf))
    a_mat = jnp.where(strict, bc[..., :, None] * jnp.einsum('bchld,bchsd->bchls', kc, kc) * decay, 0.0)
    t_mat = a_mat + jnp.eye(CHUNK, dtype=F32)
    u = lax.linalg.triangular_solve(t_mat, bc[..., None] * vc, left_side=True, lower=True, unit_diagonal=True)
    w = lax.linalg.triangular_solve(t_mat, bc[..., None] * kc * jnp.exp(gam)[..., None],
                                    left_side=True, lower=True, unit_diagonal=True)
    qk = jnp.einsum('bchld,bchsd->bchls', qc, kc) * decay
    q_dec = qc * jnp.exp(gam)[..., None]
    g_last = gam[..., -1]
    k_end = kc * jnp.exp(g_last[..., None] - gam)[..., None]

    def step(s, inp):
        u_c, w_c, qk_c, qd_c, ke_c, gl_c = inp
        v_new = u_c - jnp.einsum('bhlk,bhkv->bhlv', w_c, s)
        o = jnp.einsum('bhlk,bhkv->bhlv', qd_c, s) + jnp.einsum('bhls,bhsv->bhlv', qk_c, v_new)
        s = jnp.exp(gl_c)[..., None, None] * s + jnp.einsum('bhlk,bhlv->bhkv', ke_c, v_new)
        return s, o

    xs = tuple(jnp.moveaxis(a, 1, 0) for a in (u, w, qk, q_dec, k_end, g_last))
    s_fin, o = lax.scan(step, s0.astype(F32), xs)
    o = jnp.moveaxis(jnp.moveaxis(o, 0, 1), 2, 3).reshape(bsz, t, h, vd)
    return o.astype(q.dtype), s_fin


def rwkv7_scan(r, w_log, k, v, kk, a, s0):
    def step(s, inp):
        r_t, wl_t, k_t, v_t, kk_t, a_t = inp
        sa = jnp.einsum('bhvk,bhk->bhv', s, -kk_t)
        s = (s * jnp.exp(wl_t)[:, :, None, :] + sa[..., None] * (kk_t * a_t)[:, :, None, :]
             + v_t[..., None] * k_t[:, :, None, :])
        return s, jnp.einsum('bhvk,bhk->bhv', s, r_t)

    xs = tuple(jnp.moveaxis(u.astype(F32), 1, 0) for u in (r, w_log, k, v, kk, a))
    s_fin, y = lax.scan(step, s0.astype(F32), xs)
    return jnp.moveaxis(y, 0, 1), s_fin


def context_attention(q, k, v, sink):
    bsz, t, h, dh = q.shape
    nb = t // QBLOCK
    scale = ATT_HEAD_DIM ** -0.5
    sink_g = sink.reshape(ATT_KV_HEADS, ATT_GROUP)
    qb = jnp.moveaxis(q.reshape(bsz, nb, QBLOCK, ATT_KV_HEADS, ATT_GROUP, dh), 1, 0)

    def block(qi):
        s = jnp.einsum('bqkgd,bskd->bkgqs', qi, k).astype(F32) * scale
        p = softmax_with_sink(s, sink_g)
        return jnp.einsum('bkgqs,bskd->bqkgd', p.astype(v.dtype), v).reshape(bsz, QBLOCK, h * dh)

    out = lax.map(block, qb)
    return jnp.moveaxis(out, 0, 1).reshape(bsz, t, h * dh)


def latent_window_attention(q, k, v, kc, vc, sink):
    bsz, t, h, dh = q.shape
    nb = t // QBLOCK
    band = QBLOCK + 2 * WINDOW
    scale = ATT_HEAD_DIM ** -0.5
    sink_g = sink.reshape(ATT_KV_HEADS, ATT_GROUP)
    pad = ((0, 0), (WINDOW, WINDOW), (0, 0), (0, 0))
    kp = jnp.pad(k, pad)
    vp = jnp.pad(v, pad)

    def block(i):
        start = i * QBLOCK
        qi = lax.dynamic_slice_in_dim(q, start, QBLOCK, axis=1).reshape(bsz, QBLOCK, ATT_KV_HEADS, ATT_GROUP, dh)
        ki = lax.dynamic_slice_in_dim(kp, start, band, axis=1)
        vi = lax.dynamic_slice_in_dim(vp, start, band, axis=1)
        qpos = start + jnp.arange(QBLOCK)
        kpos = start - WINDOW + jnp.arange(band)
        valid = (jnp.abs(qpos[:, None] - kpos[None, :]) <= WINDOW) & (kpos >= 0)[None, :] & (kpos < t)[None, :]
        s_band = jnp.where(valid, jnp.einsum('bqkgd,bskd->bkgqs', qi, ki).astype(F32) * scale, -jnp.inf)
        s_ctx = jnp.einsum('bqkgd,bskd->bkgqs', qi, kc).astype(F32) * scale
        p = softmax_with_sink(jnp.concatenate([s_band, s_ctx], axis=-1), sink_g).astype(v.dtype)
        o = (jnp.einsum('bkgqs,bskd->bqkgd', p[..., :band], vi)
             + jnp.einsum('bkgqs,bskd->bqkgd', p[..., band:], vc))
        return o.reshape(bsz, QBLOCK, h * dh)

    out = lax.map(block, jnp.arange(nb))
    return jnp.moveaxis(out, 0, 1).reshape(bsz, t, h * dh)


def rwkv7_mix(rw, s_init, mu, w0, w_up, a0, a_up, g_up, k_k, k_a, r_k, ln_g, ln_b):
    bsz, t, _ = rw.shape
    rw = rw + mu * (centred_shift(rw) - rw)
    r, k, v, wd, ad, gd = _split(rw, RWKV_SPLITS)
    wd = wd.reshape(bsz, t, 2, RWKV_W_RANK)
    ad = ad.reshape(bsz, t, 2, RWKV_A_RANK)
    w = w0 + jnp.einsum('btdr,drc->btdc', jnp.tanh(wd), w_up)
    w_log = -jnp.exp(-jax.nn.softplus(-w.astype(F32)) - 0.5)
    a = jax.nn.sigmoid(a0 + jnp.einsum('btdr,drc->btdc', ad, a_up))
    gate = jax.nn.sigmoid(gd) @ g_up

    def heads(u):
        return u.reshape(u.shape[:-1] + (RWKV_HEADS, RWKV_HEAD_DIM))

    kk = l2_normalize(heads(k * k_k))
    k_dir = heads(k[:, :, None, :] * (1 + (a - 1) * k_a))
    a_h, wl_h = heads(a), heads(w_log)
    rh, vh = heads(r), heads(v)
    y_f, s_f = rwkv7_scan(rh, wl_h[:, :, 0], k_dir[:, :, 0], vh, kk, a_h[:, :, 0], s_init[:, 0])
    y_b, s_b = rwkv7_scan(_flip(rh), _flip(wl_h[:, :, 1]), _flip(k_dir[:, :, 1]), _flip(vh), _flip(kk),
                          _flip(a_h[:, :, 1]), s_init[:, 1])
    y = y_f + _flip(y_b)
    mean = jnp.mean(y, axis=-1, keepdims=True)
    var = jnp.mean(jnp.square(y - mean), axis=-1, keepdims=True)
    yn = ((y - mean) * lax.rsqrt(var + RWKV_LN_EPS)).reshape(bsz, t, RWKV_INNER) * ln_g + ln_b
    k_bonus = 0.5 * (k_dir[:, :, 0] + k_dir[:, :, 1])
    bonus = jnp.sum(rh * k_bonus * r_k, axis=-1, keepdims=True) * vh
    out = (yn.astype(rw.dtype) + bonus.reshape(bsz, t, RWKV_INNER)) * gate
    return out, jnp.stack([s_f, s_b], axis=1)


def even_mixer(h, s_ssd, s_gdn, w_in, w_out, conv_w, conv_b, a_log, dt_bias, d_skip, norm_g,
               gdn_conv_w, gdn_a_log, gdn_dt_bias, gdn_norm_g):
    bsz, t, _ = h.shape
    z_a, xbc, dt_raw, qkv, z_b, alpha_raw, beta_raw = _split(h @ w_in, EV_SPLITS)
    xbc = jax.nn.silu(centred_dwconv(xbc, conv_w, conv_b))
    xs, bm, cm = _split(xbc, (SSD_INNER, SSD_GROUPS * SSD_STATE, SSD_GROUPS * SSD_STATE))
    xs = xs.reshape(bsz, t, SSD_HEADS, SSD_HEAD_DIM)
    bm = bm.reshape(bsz, t, SSD_GROUPS, SSD_STATE)
    cm = cm.reshape(bsz, t, SSD_GROUPS, SSD_STATE)
    dt = jax.nn.softplus((dt_raw.reshape(bsz, t, 2, SSD_HEADS) + dt_bias).astype(F32))
    a_neg = -jnp.exp(a_log.astype(F32))
    y_f, st_f = ssd_scan(xs, dt[:, :, 0], a_neg[0], bm, cm, s_ssd[:, 0])
    y_b, st_b = ssd_scan(_flip(xs), _flip(dt[:, :, 1]), a_neg[1], _flip(bm), _flip(cm), s_ssd[:, 1])
    y = y_f + _flip(y_b) + d_skip[:, None] * xs
    grp = SSD_INNER // SSD_GROUPS
    y = (y.reshape(bsz, t, SSD_INNER) * jax.nn.silu(z_a)).reshape(bsz, t, SSD_GROUPS, grp)
    y = rms_norm(y, norm_g.reshape(SSD_GROUPS, grp)).reshape(bsz, t, SSD_INNER)
    qkv = jax.nn.silu(centred_dwconv(qkv, gdn_conv_w))
    q, k, v = [u.reshape(bsz, t, GDN_HEADS, GDN_HEAD_DIM) for u in _split(qkv, (GDN_INNER,) * 3)]
    q = l2_normalize(q) * (GDN_HEAD_DIM ** -0.5)
    k = l2_normalize(k)
    g = -jnp.exp(gdn_a_log.astype(F32)) * jax.nn.softplus(
        (alpha_raw.reshape(bsz, t, 2, GDN_HEADS) + gdn_dt_bias).astype(F32))
    beta = jax.nn.sigmoid(beta_raw.reshape(bsz, t, 2, GDN_HEADS).astype(F32))
    o_f, gs_f = gdn_scan(q, k, v, g[:, :, 0], beta[:, :, 0], s_gdn[:, 0])
    o_b, gs_b = gdn_scan(_flip(q), _flip(k), _flip(v), _flip(g[:, :, 1]), _flip(beta[:, :, 1]), s_gdn[:, 1])
    o = rms_norm(o_f + _flip(o_b), gdn_norm_g) * jax.nn.silu(z_b.reshape(bsz, t, GDN_HEADS, GDN_HEAD_DIM))
    out = jnp.concatenate([y, o.reshape(bsz, t, GDN_INNER)], axis=-1) @ w_out
    return out, jnp.stack([st_f, st_b], axis=1), jnp.stack([gs_f, gs_b], axis=1)


def odd_mixer(h, latent, k_ctx, v_ctx, s_rwkv, w_in, w_out, sink, mu, w0, w_up, a0, a_up, g_up,
              k_k, k_a, r_k, ln_g, ln_b):
    bsz, t, _ = h.shape
    q, k, v, rw = _split(h @ w_in, OD_SPLITS)
    q = q.reshape(bsz, t, ATT_HEADS, ATT_HEAD_DIM)
    k = k.reshape(bsz, t, ATT_KV_HEADS, ATT_HEAD_DIM)
    v = v.reshape(bsz, t, ATT_KV_HEADS, ATT_HEAD_DIM)
    if latent:
        att = latent_window_attention(axial_rope(q), axial_rope(k), v, k_ctx, v_ctx, sink)
    else:
        att = context_attention(q, k, v, sink)
        k_ctx, v_ctx = k, v
    rw_out, s_new = rwkv7_mix(rw, s_rwkv, mu, w0, w_up, a0, a_up, g_up, k_k, k_a, r_k, ln_g, ln_b)
    out = jnp.concatenate([att, rw_out], axis=-1) @ w_out
    return out, k_ctx, v_ctx, s_new


def setup_inputs(seed: int = 0) -> dict:
    key = jax.random.key(seed)
    ks = iter(list(jax.random.split(key, 64)))
    ne, no = (DEPTH + 1) // 2, DEPTH // 2
    d = D_MODEL

    def nrm(shape, s):
        return jax.random.normal(next(ks), shape, F32) * s

    def uni(shape, lo, hi):
        return jax.random.uniform(next(ks), shape, F32, lo, hi)

    def dt_bias(shape):
        dt = jnp.exp(uni(shape, math.log(1e-3), math.log(1e-1)))
        return dt + jnp.log(-jnp.expm1(-dt))

    return {
        'x_prompt': nrm((BATCH, SEQ, d), 1.0),
        'x_sample': nrm((DEC_BATCH, DEC_SEQ, d), 1.0),
        'state_ssd': nrm((DEC_BATCH, ne, 2, SSD_HEADS, SSD_HEAD_DIM, SSD_STATE), 0.1),
        'state_gdn': nrm((DEC_BATCH, ne, 2, GDN_HEADS, GDN_HEAD_DIM, GDN_HEAD_DIM), 0.1),
        'cache_k': nrm((DEC_BATCH, no, PAST_LEN, ATT_KV_HEADS, ATT_HEAD_DIM), 1.0),
        'cache_v': nrm((DEC_BATCH, no, PAST_LEN, ATT_KV_HEADS, ATT_HEAD_DIM), 1.0),
        'state_rwkv': nrm((DEC_BATCH, no, 2, RWKV_HEADS, RWKV_HEAD_DIM, RWKV_HEAD_DIM), 0.1),
        'c': nrm((DEC_BATCH, d), 1.0),
        'c_ctx': nrm((d,), 1.0),
        'ada_w': nrm((DEPTH, d, 6 * d), 0.5 * d ** -0.5),
        'ada_b': nrm((DEPTH, 6 * d), 0.02),
        'norm1_g': 1.0 + nrm((DEPTH, d), 0.02),
        'norm2_g': 1.0 + nrm((DEPTH, d), 0.02),
        'ffn_w1': nrm((DEPTH, d, 2 * FFN_HIDDEN), d ** -0.5),
        'ffn_w2': nrm((DEPTH, FFN_HIDDEN, d), FFN_HIDDEN ** -0.5),
        'final_g': 1.0 + nrm((d,), 0.02),
        'ev_w_in': nrm((ne, d, EV_COLS), d ** -0.5),
        'ev_w_out': nrm((ne, EV_MIX, d), EV_MIX ** -0.5),
        'ssd_conv_w': nrm((ne, CONV_W, SSD_CONV_DIM), CONV_W ** -0.5),
        'ssd_conv_b': nrm((ne, SSD_CONV_DIM), 0.02),
        'ssd_a_log': jnp.log(uni((ne, 2, SSD_HEADS), 1.0, 16.0)),
        'ssd_dt_bias': dt_bias((ne, 2, SSD_HEADS)),
        'ssd_d': 1.0 + nrm((ne, SSD_HEADS), 0.1),
        'ssd_norm_g': 1.0 + nrm((ne, SSD_INNER), 0.02),
        'gdn_conv_w': nrm((ne, CONV_W, 3 * GDN_INNER), CONV_W ** -0.5),
        'gdn_a_log': jnp.log(uni((ne, 2, GDN_HEADS), 1.0, 16.0)),
        'gdn_dt_bias': dt_bias((ne, 2, GDN_HEADS)),
        'gdn_norm_g': 1.0 + nrm((ne, GDN_HEAD_DIM), 0.02),
        'od_w_in': nrm((no, d, OD_COLS), d ** -0.5),
        'od_w_out': nrm((no, OD_MIX, d), OD_MIX ** -0.5),
        'attn_sink': nrm((no, ATT_HEADS), 0.5),
        'rwkv_mu': uni((no, RWKV_COLS), 0.0, 1.0),
        'rwkv_w0': uni((no, 2, RWKV_INNER), -5.0, 1.0),
        'rwkv_w_up': nrm((no, 2, RWKV_W_RANK, RWKV_INNER), 0.1),
        'rwkv_a0': nrm((no, 2, RWKV_INNER), 0.1),
        'rwkv_a_up': nrm((no, 2, RWKV_A_RANK, RWKV_INNER), 0.1),
        'rwkv_g_up': nrm((no, RWKV_G_RANK, RWKV_INNER), RWKV_G_RANK ** -0.5),
        'rwkv_k_k': 0.85 + nrm((no, RWKV_INNER), 0.02),
        'rwkv_k_a': 1.0 + nrm((no, RWKV_INNER), 0.02),
        'rwkv_r_k': nrm((no, RWKV_HEADS, RWKV_HEAD_DIM), 0.1),
        'rwkv_ln_g': 1.0 + nrm((no, RWKV_INNER), 0.02),
        'rwkv_ln_b': nrm((no, RWKV_INNER), 0.02),
    }


def reference(x_prompt, x_sample, state_ssd, state_gdn, cache_k, cache_v, state_rwkv, c, c_ctx,
              ada_w, ada_b, norm1_g, norm2_g, ffn_w1, ffn_w2, final_g,
              ev_w_in, ev_w_out, ssd_conv_w, ssd_conv_b, ssd_a_log, ssd_dt_bias, ssd_d, ssd_norm_g,
              gdn_conv_w, gdn_a_log, gdn_dt_bias, gdn_norm_g,
              od_w_in, od_w_out, attn_sink, rwkv_mu, rwkv_w0, rwkv_w_up, rwkv_a0, rwkv_a_up, rwkv_g_up,
              rwkv_k_k, rwkv_k_a, rwkv_r_k, rwkv_ln_g, rwkv_ln_b):
    xp, xs = x_prompt, x_sample
    bp = x_prompt.shape[0]
    new_ssd, new_gdn, new_k, new_v, new_rwkv = [], [], [], [], []
    for l in range(DEPTH):
        i = l // 2
        mp_ = adaln(c_ctx, ada_w[l], ada_b[l])
        ms_ = [m[:, None, :] for m in adaln(c, ada_w[l], ada_b[l])]
        hp = modulate(rms_norm(xp, norm1_g[l]), mp_[0], mp_[1])
        hs = modulate(rms_norm(xs, norm1_g[l]), ms_[0], ms_[1])
        if l % 2 == 0:
            ev = (ev_w_in[i], ev_w_out[i], ssd_conv_w[i], ssd_conv_b[i], ssd_a_log[i], ssd_dt_bias[i],
                  ssd_d[i], ssd_norm_g[i], gdn_conv_w[i], gdn_a_log[i], gdn_dt_bias[i], gdn_norm_g[i])
            zero_ssd = jnp.zeros((bp, 2, SSD_HEADS, SSD_HEAD_DIM, SSD_STATE), F32)
            zero_gdn = jnp.zeros((bp, 2, GDN_HEADS, GDN_HEAD_DIM, GDN_HEAD_DIM), F32)
            mix_p, st_ssd, st_gdn = even_mixer(hp, zero_ssd, zero_gdn, *ev)
            mix_s, _, _ = even_mixer(hs, state_ssd[:, i], state_gdn[:, i], *ev)
            new_ssd.append(st_ssd)
            new_gdn.append(st_gdn)
        else:
            od = (od_w_in[i], od_w_out[i], attn_sink[i], rwkv_mu[i], rwkv_w0[i], rwkv_w_up[i], rwkv_a0[i],
                  rwkv_a_up[i], rwkv_g_up[i], rwkv_k_k[i], rwkv_k_a[i], rwkv_r_k[i], rwkv_ln_g[i], rwkv_ln_b[i])
            zero_rw = jnp.zeros((bp, 2, RWKV_HEADS, RWKV_HEAD_DIM, RWKV_HEAD_DIM), F32)
            mix_p, kc, vc, st_rw = odd_mixer(hp, False, None, None, zero_rw, *od)
            mix_s, _, _, _ = odd_mixer(hs, True, cache_k[:, i], cache_v[:, i], state_rwkv[:, i], *od)
            new_k.append(kc)
            new_v.append(vc)
            new_rwkv.append(st_rw)
        xp = xp + mp_[2] * mix_p
        xs = xs + ms_[2] * mix_s
        hp = modulate(rms_norm(xp, norm2_g[l]), mp_[3], mp_[4])
        hs = modulate(rms_norm(xs, norm2_g[l]), ms_[3], ms_[4])
        xp = xp + mp_[5] * swiglu(hp, ffn_w1[l], ffn_w2[l])
        xs = xs + ms_[5] * swiglu(hs, ffn_w1[l], ffn_w2[l])
    y_prompt = rms_norm(xp, final_g)
    y_sample = rms_norm(xs, final_g)
    return (y_prompt, y_sample, jnp.stack(new_ssd, axis=1), jnp.stack(new_gdn, axis=1),
            jnp.stack(new_k, axis=1), jnp.stack(new_v, axis=1), jnp.stack(new_rwkv, axis=1))
```

```python
import functools

import numpy as np
import jax
import jax.numpy as jnp
from jax import lax
from jax.experimental import pallas as pl
from jax.experimental.pallas import tpu as pltpu

F32 = jnp.float32
BF16 = jnp.bfloat16
HI = lax.Precision.HIGHEST

D = 1024
NORM_EPS = 1e-6
L = 64
CONV_W = 5
CONV_PAD = 8

SSD_HEADS, SSD_P, SSD_N, SSD_GROUPS = 16, 64, 128, 2
SSD_GW = SSD_HEADS // SSD_GROUPS * SSD_P
GDN_HEADS, GDN_K = 8, 128
ATT_HEADS, ATT_KVH, ATT_DH, ATT_GROUP = 8, 2, 128, 4
WINDOW, QBLOCK, GRID_W, ROPE_BASE = 128, 128, 64, 10000.0
RW_HEADS, RW_K = 16, 64
RWKV_LN_EPS = 64e-5
FFN_H = 2816
FFN_TN = 256

EV_ZA, EV_XS, EV_ZB, EV_Q, EV_K, EV_V, EV_B, EV_C, EV_SM, EV_COLS = 0, 1024, 2048, 3072, 4096, 5120, 6144, 6400, 6656, 6784
SM_DT, SM_ALPHA, SM_BETA = 0, 32, 48
OD_Q, OD_K, OD_V, OD_R, OD_RK, OD_RV, OD_WD, OD_AD, OD_GD, OD_COLS = 0, 1024, 1280, 1536, 2560, 3584, 4608, 4736, 4864, 4992

VMEM_LIMIT = 56 * 1024 * 1024


def _cparams(sem):
    return pltpu.CompilerParams(dimension_semantics=sem, vmem_limit_bytes=VMEM_LIMIT)


def _dot(a, b, prec=None):
    return jnp.dot(a, b, preferred_element_type=F32, precision=prec)


def _dot_nt(a, b, prec=None):
    return lax.dot_general(a, b, (((1,), (1,)), ((), ())), preferred_element_type=F32, precision=prec)


def _dot_tn(a, b, prec=None):
    return lax.dot_general(a, b, (((0,), (0,)), ((), ())), preferred_element_type=F32, precision=prec)


def _bdot(a, b):
    return _dot(a.astype(BF16), b.astype(BF16))


def _bdot_nt(a, b):
    return _dot_nt(a.astype(BF16), b.astype(BF16))


def _bdot_tn(a, b):
    return _dot_tn(a.astype(BF16), b.astype(BF16))


def _silu(x):
    return x * jax.nn.sigmoid(x)


def _softplus(x):
    return jnp.maximum(x, 0.0) + jnp.log(1.0 + jnp.exp(-jnp.abs(x)))


def _iota(shape, dim):
    return lax.broadcasted_iota(jnp.int32, shape, dim)


def _tri_masks(rev):
    r, c = _iota((L, L), 0), _iota((L, L), 1)
    if rev:
        return c >= r, c > r
    return c <= r, c < r


def _inv_unit(a):
    eye = (_iota(a.shape, 0) == _iota(a.shape, 1)).astype(F32)
    x = eye - a
    p = _dot(a, a, HI)
    n = 2
    while True:
        x = x + _dot(x, p, HI)
        n *= 2
        if n >= L:
            break
        p = _dot(p, p, HI)
    return x


def _conv_silu(x, w_ref, b, pad_ref, t):
    c = x.shape[1]
    zeros = jnp.zeros((CONV_PAD, c), F32)
    pad_ref[0:CONV_PAD, 0:c] = zeros
    pad_ref[CONV_PAD + t:2 * CONV_PAD + t, 0:c] = zeros
    pad_ref[CONV_PAD:CONV_PAD + t, 0:c] = x
    acc = None
    for j in range(CONV_W):
        off = CONV_PAD - CONV_W // 2 + j
        term = pad_ref[off:off + t, 0:c] * w_ref[j:j + 1, :]
        acc = term if acc is None else acc + term
    if b is not None:
        acc = acc + b
    return _silu(acc)


def _adaln_kernel(c_ref, w_ref, b_ref, o_ref):
    c = c_ref[...]
    o_ref[0] = _bdot(_silu(c), w_ref[0]) + b_ref[0]


def _adaln(cvec, ada_w, ada_b):
    depth, _, n = ada_w.shape
    tn = 1536
    out = pl.pallas_call(
        _adaln_kernel,
        grid=(depth, n // tn),
        in_specs=[pl.BlockSpec((16, D), lambda l, j: (0, 0)),
                  pl.BlockSpec((1, D, tn), lambda l, j: (l, 0, j)),
                  pl.BlockSpec((1, 1, tn), lambda l, j: (l, 0, j))],
        out_specs=pl.BlockSpec((1, 16, tn), lambda l, j: (l, 0, j)),
        out_shape=jax.ShapeDtypeStruct((depth, 16, n), F32),
        compiler_params=_cparams(("arbitrary", "arbitrary")),
        name="adaln",
    )(cvec, ada_w, ada_b.reshape(depth, 1, n))
    return out.reshape(depth, 16, 6, D)


def _mod_row_map(tm, t):
    if t is None:
        return lambda i: (0, 0, 0)
    per = t // tm
    return lambda i: (1 + i // per, 0, 0)


def _norm_mod(x, g, m, shift_idx, scale_idx):
    var = jnp.mean(x * x, axis=-1, keepdims=True)
    y = x * lax.rsqrt(var + NORM_EPS) * g
    return y * (1.0 + m[scale_idx:scale_idx + 1]) + m[shift_idx:shift_idx + 1]


def _inproj_kernel(x_ref, g_ref, mod_ref, w_ref, o_ref, *, tn):
    h = _norm_mod(x_ref[...], g_ref[...], mod_ref[0], 0, 1).astype(BF16)
    n = w_ref.shape[1]
    for j in range(0, n, tn):
        w = min(tn, n - j)
        o_ref[:, j:j + w] = _dot(h, w_ref[:, j:j + w])


def _inproj(x, g, mod, w, sample_t, tm=256, tn=512):
    rows, n = x.shape[0], w.shape[1]
    return pl.pallas_call(
        functools.partial(_inproj_kernel, tn=tn),
        grid=(rows // tm,),
        in_specs=[pl.BlockSpec((tm, D), lambda i: (i, 0)),
                  pl.BlockSpec((1, D), lambda i: (0, 0)),
                  pl.BlockSpec((1, 6, D), _mod_row_map(tm, sample_t)),
                  pl.BlockSpec((D, n), lambda i: (0, 0))],
        out_specs=pl.BlockSpec((tm, n), lambda i: (i, 0)),
        out_shape=jax.ShapeDtypeStruct((rows, n), F32),
        compiler_params=_cparams(("arbitrary",)),
        name="inproj",
    )(x, g, mod, w)


def _outproj_kernel(a1_ref, a2_ref, x_ref, mod_ref, w_ref, o_ref):
    k1 = a1_ref.shape[1]
    acc = _dot(a1_ref[...].astype(BF16), w_ref[0:k1, :])
    acc = acc + _dot(a2_ref[...].astype(BF16), w_ref[k1:, :])
    o_ref[...] = x_ref[...] + mod_ref[0][2:3] * acc


def _outproj(a1, a2, x, mod, w, sample_t, tm=256):
    rows = x.shape[0]
    return pl.pallas_call(
        _outproj_kernel,
        grid=(rows // tm,),
        in_specs=[pl.BlockSpec((tm, a1.shape[1]), lambda i: (i, 0)),
                  pl.BlockSpec((tm, a2.shape[1]), lambda i: (i, 0)),
                  pl.BlockSpec((tm, D), lambda i: (i, 0)),
                  pl.BlockSpec((1, 6, D), _mod_row_map(tm, sample_t)),
                  pl.BlockSpec(w.shape, lambda i: (0, 0))],
        out_specs=pl.BlockSpec((tm, D), lambda i: (i, 0)),
        out_shape=jax.ShapeDtypeStruct((rows, D), F32),
        compiler_params=_cparams(("arbitrary",)),
        name="outproj",
    )(a1, a2, x, mod, w)


def _ffn_kernel(x_ref, g_ref, mod_ref, w1_ref, w2_ref, fg_ref, o_ref, *, final):
    x = x_ref[...]
    m = mod_ref[0]
    h = _norm_mod(x, g_ref[...], m, 3, 4).astype(BF16)
    acc = jnp.zeros(x.shape, F32)
    for j in range(0, FFN_H, FFN_TN):
        gate = _dot(h, w1_ref[:, j:j + FFN_TN])
        up = _dot(h, w1_ref[:, FFN_H + j:FFN_H + j + FFN_TN])
        acc = acc + _dot((_silu(gate) * up).astype(BF16), w2_ref[j:j + FFN_TN, :])
    y = x + m[5:6] * acc
    if final:
        var = jnp.mean(y * y, axis=-1, keepdims=True)
        y = y * lax.rsqrt(var + NORM_EPS) * fg_ref[...]
    o_ref[...] = y


def _ffn(x, g, mod, w1, w2, final_g, sample_t, final, tm=256):
    rows = x.shape[0]
    return pl.pallas_call(
        functools.partial(_ffn_kernel, final=final),
        grid=(rows // tm,),
        in_specs=[pl.BlockSpec((tm, D), lambda i: (i, 0)),
                  pl.BlockSpec((1, D), lambda i: (0, 0)),
                  pl.BlockSpec((1, 6, D), _mod_row_map(tm, sample_t)),
                  pl.BlockSpec(w1.shape, lambda i: (0, 0)),
                  pl.BlockSpec(w2.shape, lambda i: (0, 0)),
                  pl.BlockSpec((1, D), lambda i: (0, 0))],
        out_specs=pl.BlockSpec((tm, D), lambda i: (i, 0)),
        out_shape=jax.ShapeDtypeStruct((rows, D), F32),
        compiler_params=_cparams(("arbitrary",)),
        name="ffn",
    )(x, g, mod, w1, w2, final_g)


def _ssd_kernel(*refs, t, has_state):
    (za_ref, xs_ref, bm_ref, cm_ref, sm_ref, cwx_ref, cwb_ref, cwc_ref, cbx_ref, cbb_ref, cbc_ref,
     bias_ref, alog_ref, dsk_ref, ng_ref) = refs[:15]
    k = 15
    s0_ref = st_ref = None
    if has_state:
        s0_ref = refs[k]
        k += 1
    y_ref = refs[k]
    k += 1
    if not has_state:
        st_ref = refs[k]
        k += 1
    pad_scr, xs_scr, bb_scr, cc_scr, dt_scr, da_scr, dasm_scr, yacc_scr, state_scr = refs[k:]

    g = pl.program_id(1)
    nc = t // L
    npair = SSD_GW // 128

    xs_scr[...] = _conv_silu(xs_ref[...], cwx_ref, cbx_ref[...], pad_scr, t)
    bb_scr[...] = _conv_silu(bm_ref[...], cwb_ref, cbb_ref[...], pad_scr, t)
    cc_scr[...] = _conv_silu(cm_ref[...], cwc_ref, cbc_ref[...], pad_scr, t)

    dtv = _softplus(sm_ref[...] + bias_ref[...])
    dav = dtv * (-jnp.exp(alog_ref[...]))
    heads_per_group = SSD_HEADS // SSD_GROUPS
    for d in range(2):
        first = SM_DT + d * SSD_HEADS + g * heads_per_group
        src = _iota((128, SSD_GW), 0)
        e = (src == first + jnp.right_shift(_iota((128, SSD_GW), 1), 6)).astype(F32)
        dt_scr[d] = _dot(dtv, e, HI)
        da_scr[d] = _dot(dav, e, HI)
        sel = (_iota((128, 128), 0) == first + _iota((128, 128), 1)) & (_iota((128, 128), 1) < heads_per_group)
        dasm_scr[d] = _dot(dav, sel.astype(F32), HI)

    lane128 = _iota((L, 128), 1)
    for d in range(2):
        rev = d == 1
        incl, _ = _tri_masks(rev)
        tri = incl.astype(F32)
        tri_t = _tri_masks(not rev)[0].astype(F32)
        last = 0 if rev else L - 1
        if has_state:
            state_scr[...] = s0_ref[0, d].reshape(SSD_GW, SSD_N).T
        else:
            state_scr[...] = jnp.zeros((SSD_N, SSD_GW), F32)

        def body(ci, carry, d=d, rev=rev, tri=tri, tri_t=tri_t, incl=incl, last=last):
            c = (nc - 1 - ci) if rev else ci
            r0 = pl.multiple_of(c * L, L)
            rows = pl.ds(r0, L)
            x = xs_scr[rows, :]
            bc = bb_scr[rows, :]
            cc = cc_scr[rows, :]
            acum = _dot(tri, da_scr[d, rows, :], HI)
            acum_t = _dot_tn(dasm_scr[d, rows, :], tri_t, HI)
            xdt = x * dt_scr[d, rows, :]
            cb = _bdot_nt(cc, bc)
            total = acum[last:last + 1, :]
            s_in = state_scr[...]
            y_inter = _bdot(cc, s_in) * jnp.exp(acum)
            pieces = []
            for j in range(npair):
                xp = xdt[:, j * 128:(j + 1) * 128]
                yp = None
                for hh in range(2):
                    h = 2 * j + hh
                    col = acum[:, h * SSD_P:h * SSD_P + 1]
                    row = acum_t[h:h + 1, :]
                    dec = jnp.where(incl, jnp.exp(jnp.minimum(col - row, 0.0)), 0.0)
                    mine = (lane128 >= SSD_P) if hh else (lane128 < SSD_P)
                    term = _bdot(cb * dec, jnp.where(mine, xp, 0.0))
                    yp = term if yp is None else yp + term
                pieces.append(yp)
            y = jnp.concatenate(pieces, axis=1) + y_inter
            if d == 0:
                yacc_scr[rows, :] = y
            else:
                yacc_scr[rows, :] = yacc_scr[rows, :] + y
            wx = jnp.exp(total - acum) * xdt
            state_scr[...] = s_in * jnp.exp(total) + _bdot_tn(bc, wx)
            return carry

        lax.fori_loop(0, nc, body, 0)
        if not has_state:
            st_ref[0, d] = state_scr[...].T.reshape(heads_per_group, SSD_P, SSD_N)

    y = yacc_scr[...] + dsk_ref[...] * xs_scr[...]
    y = y * _silu(za_ref[...])
    var = jnp.mean(y * y, axis=-1, keepdims=True)
    y_ref[...] = y * lax.rsqrt(var + NORM_EPS) * ng_ref[...]


def _ssd(proj, nseq, t, conv_w, conv_b, bias_row, alog_row, dskip_row, norm_g, state):
    has_state = state is not None
    seq = lambda blk: (lambda b, g: (b, blk(g)))
    const = lambda blk: (lambda b, g: (0, blk(g)))
    in_specs = [
        pl.BlockSpec((t, SSD_GW), seq(lambda g: EV_ZA // SSD_GW + g)),
        pl.BlockSpec((t, SSD_GW), seq(lambda g: EV_XS // SSD_GW + g)),
        pl.BlockSpec((t, 128), seq(lambda g: EV_B // 128 + g)),
        pl.BlockSpec((t, 128), seq(lambda g: EV_C // 128 + g)),
        pl.BlockSpec((t, 128), seq(lambda g: EV_SM // 128)),
        pl.BlockSpec((CONV_W, SSD_GW), const(lambda g: g)),
        pl.BlockSpec((CONV_W, 128), const(lambda g: 1024 // 128 + g)),
        pl.BlockSpec((CONV_W, 128), const(lambda g: 1280 // 128 + g)),
        pl.BlockSpec((1, SSD_GW), const(lambda g: g)),
        pl.BlockSpec((1, 128), const(lambda g: 1024 // 128 + g)),
        pl.BlockSpec((1, 128), const(lambda g: 1280 // 128 + g)),
        pl.BlockSpec((1, 128), const(lambda g: 0)),
        pl.BlockSpec((1, 128), const(lambda g: 0)),
        pl.BlockSpec((1, SSD_GW), const(lambda g: g)),
        pl.BlockSpec((1, SSD_GW), const(lambda g: g)),
    ]
    args = [proj, proj, proj, proj, proj, conv_w, conv_w, conv_w, conv_b, conv_b, conv_b,
            bias_row, alog_row, dskip_row, norm_g]
    st_block = (1, 2, SSD_HEADS // SSD_GROUPS, SSD_P, SSD_N)
    st_map = lambda b, g: (b, 0, g, 0, 0)
    out_specs = [pl.BlockSpec((t, SSD_GW), lambda b, g: (b, g))]
    out_shape = [jax.ShapeDtypeStruct((nseq * t, SSD_HEADS * SSD_P), F32)]
    if has_state:
        in_specs.append(pl.BlockSpec(st_block, st_map))
        args.append(state)
    else:
        out_specs.append(pl.BlockSpec(st_block, st_map))
        out_shape.append(jax.ShapeDtypeStruct((nseq, 2, SSD_HEADS, SSD_P, SSD_N), F32))
    scratch = [pltpu.VMEM((t + 2 * CONV_PAD, SSD_GW), F32), pltpu.VMEM((t, SSD_GW), F32),
               pltpu.VMEM((t, 128), F32), pltpu.VMEM((t, 128), F32),
               pltpu.VMEM((2, t, SSD_GW), F32), pltpu.VMEM((2, t, SSD_GW), F32), pltpu.VMEM((2, t, 128), F32),
               pltpu.VMEM((t, SSD_GW), F32), pltpu.VMEM((SSD_N, SSD_GW), F32)]
    return pl.pallas_call(
        functools.partial(_ssd_kernel, t=t, has_state=has_state),
        grid=(nseq, SSD_GROUPS), in_specs=in_specs, out_specs=out_specs, out_shape=out_shape,
        scratch_shapes=scratch, compiler_params=_cparams(("arbitrary", "arbitrary")), name="ssd",
    )(*args)


def _gdn_kernel(*refs, t, has_state):
    (q_ref, k_ref, v_ref, zb_ref, sm_ref, cwq_ref, cwk_ref, cwv_ref, bias_ref, alog_ref, ng_ref) = refs[:11]
    k = 11
    s0_ref = st_ref = None
    if has_state:
        s0_ref = refs[k]
        k += 1
    o_ref = refs[k]
    k += 1
    if not has_state:
        st_ref = refs[k]
        k += 1
    pad_scr, q_scr, k_scr, v_scr, g_scr, b_scr, oacc_scr, state_scr = refs[k:]

    h = pl.program_id(1)
    nc = t // L

    def l2n(x):
        return x * lax.rsqrt(jnp.sum(x * x, axis=-1, keepdims=True) + NORM_EPS)

    q_scr[...] = l2n(_conv_silu(q_ref[...], cwq_ref, None, pad_scr, t)) * (GDN_K ** -0.5)
    k_scr[...] = l2n(_conv_silu(k_ref[...], cwk_ref, None, pad_scr, t))
    v_scr[...] = _conv_silu(v_ref[...], cwv_ref, None, pad_scr, t)

    sm = sm_ref[...]
    gl = -jnp.exp(alog_ref[...]) * _softplus(sm + bias_ref[...])
    bt = jax.nn.sigmoid(sm)
    lane = _iota((t, 128), 1)
    for d in range(2):
        gcol = jnp.sum(jnp.where(lane == SM_ALPHA + d * GDN_HEADS + h, gl, 0.0), axis=1, keepdims=True)
        bcol = jnp.sum(jnp.where(lane == SM_BETA + d * GDN_HEADS + h, bt, 0.0), axis=1, keepdims=True)
        g_scr[d] = jnp.broadcast_to(gcol, (t, 128))
        b_scr[d] = jnp.broadcast_to(bcol, (t, 128))

    for d in range(2):
        rev = d == 1
        incl, strict = _tri_masks(rev)
        tri = incl.astype(F32)
        tri_t = _tri_masks(not rev)[0].astype(F32)
        last = 0 if rev else L - 1
        if has_state:
            state_scr[...] = s0_ref[0, d, 0]
        else:
            state_scr[...] = jnp.zeros((GDN_K, GDN_K), F32)

        def body(ci, carry, d=d, rev=rev, incl=incl, strict=strict, tri=tri, tri_t=tri_t, last=last):
            c = (nc - 1 - ci) if rev else ci
            rows = pl.ds(pl.multiple_of(c * L, L), L)
            qc, kc, vc = q_scr[rows, :], k_scr[rows, :], v_scr[rows, :]
            gb, bb = g_scr[d, rows, :], b_scr[d, rows, :]
            gam = _dot(tri, gb, HI)
            gam_row = _dot_tn(gb, tri_t, HI)[0:1, :]
            dec = jnp.where(incl, jnp.exp(jnp.minimum(gam[:, 0:1] - gam_row, 0.0)), 0.0)
            a_mat = jnp.where(strict, bb[:, 0:1] * _bdot_nt(kc, kc) * dec, 0.0)
            t_inv = _inv_unit(a_mat)
            eg = jnp.exp(gam)
            u = _dot(t_inv, bb * vc, HI)
            w = _dot(t_inv, bb * kc * eg, HI)
            qk = _bdot_nt(qc, kc) * dec
            g_last = gam[last:last + 1, :]
            k_end = kc * jnp.exp(g_last - gam)
            s = state_scr[...]
            v_new = u - _bdot(w, s)
            o = _bdot(qc * eg, s) + _bdot(qk, v_new)
            state_scr[...] = jnp.exp(g_last) * s + _bdot_tn(k_end, v_new)
            if d == 0:
                oacc_scr[rows, :] = o
            else:
                oacc_scr[rows, :] = oacc_scr[rows, :] + o
            return carry

        lax.fori_loop(0, nc, body, 0)
        if not has_state:
            st_ref[0, d, 0] = state_scr[...]

    o = oacc_scr[...]
    var = jnp.mean(o * o, axis=-1, keepdims=True)
    o_ref[...] = o * lax.rsqrt(var + NORM_EPS) * ng_ref[...] * _silu(zb_ref[...])


def _gdn(proj, nseq, t, conv_w, bias_row, alog_row, norm_g, state):
    has_state = state is not None
    seq = lambda base: (lambda b, h: (b, base // 128 + h))
    const = lambda base: (lambda b, h: (0, base // 128 + h))
    in_specs = [
        pl.BlockSpec((t, 128), seq(EV_Q)), pl.BlockSpec((t, 128), seq(EV_K)), pl.BlockSpec((t, 128), seq(EV_V)),
        pl.BlockSpec((t, 128), seq(EV_ZB)), pl.BlockSpec((t, 128), lambda b, h: (b, EV_SM // 128)),
        pl.BlockSpec((CONV_W, 128), const(0)), pl.BlockSpec((CONV_W, 128), const(1024)),
        pl.BlockSpec((CONV_W, 128), const(2048)),
        pl.BlockSpec((1, 128), lambda b, h: (0, 0)), pl.BlockSpec((1, 128), lambda b, h: (0, 0)),
        pl.BlockSpec((1, 128), lambda b, h: (0, 0)),
    ]
    args = [proj, proj, proj, proj, proj, conv_w, conv_w, conv_w, bias_row, alog_row, norm_g]
    st_block = (1, 2, 1, GDN_K, GDN_K)
    st_map = lambda b, h: (b, 0, h, 0, 0)
    out_specs = [pl.BlockSpec((t, 128), lambda b, h: (b, h))]
    out_shape = [jax.ShapeDtypeStruct((nseq * t, GDN_HEADS * GDN_K), F32)]
    if has_state:
        in_specs.append(pl.BlockSpec(st_block, st_map))
        args.append(state)
    else:
        out_specs.append(pl.BlockSpec(st_block, st_map))
        out_shape.append(jax.ShapeDtypeStruct((nseq, 2, GDN_HEADS, GDN_K, GDN_K), F32))
    scratch = [pltpu.VMEM((t + 2 * CONV_PAD, 128), F32), pltpu.VMEM((t, 128), F32), pltpu.VMEM((t, 128), F32),
               pltpu.VMEM((t, 128), F32), pltpu.VMEM((2, t, 128), F32), pltpu.VMEM((2, t, 128), F32),
               pltpu.VMEM((t, 128), F32), pltpu.VMEM((GDN_K, GDN_K), F32)]
    return pl.pallas_call(
        functools.partial(_gdn_kernel, t=t, has_state=has_state),
        grid=(nseq, GDN_HEADS), in_specs=in_specs, out_specs=out_specs, out_shape=out_shape,
        scratch_shapes=scratch, compiler_params=_cparams(("arbitrary", "arbitrary")), name="gdn",
    )(*args)


def _softmax_sink_pv(scores, sink, values):
    m = sink
    for s in scores:
        m = jnp.maximum(m, jnp.max(s, axis=-1, keepdims=True))
    den = jnp.exp(sink - m)
    ps = []
    for s in scores:
        p = jnp.exp(s - m)
        den = den + jnp.sum(p, axis=-1, keepdims=True)
        ps.append(p)
    out = None
    for p, v in zip(ps, values):
        term = _bdot(p / den, v)
        out = term if out is None else out + term
    return out


def _ctx_attn_kernel(sink_ref, q_ref, k_ref, v_ref, o_ref):
    kvh = pl.program_id(1)
    scale = ATT_DH ** -0.5
    kk, vv = k_ref[...], v_ref[...]
    for gq in range(ATT_GROUP):
        q = q_ref[:, gq * ATT_DH:(gq + 1) * ATT_DH]
        s = _bdot_nt(q, kk) * scale
        sink = sink_ref[kvh * ATT_GROUP + gq]
        o_ref[:, gq * ATT_DH:(gq + 1) * ATT_DH] = _softmax_sink_pv([s], sink, [vv])


def _ctx_attn(proj, nseq, t, sink):
    gw = ATT_GROUP * ATT_DH
    return pl.pallas_call(
        _ctx_attn_kernel,
        grid=(nseq, ATT_KVH),
        in_specs=[pl.BlockSpec(memory_space=pltpu.SMEM),
                  pl.BlockSpec((t, gw), lambda b, kv: (b, OD_Q // gw + kv)),
                  pl.BlockSpec((t, ATT_DH), lambda b, kv: (b, OD_K // ATT_DH + kv)),
                  pl.BlockSpec((t, ATT_DH), lambda b, kv: (b, OD_V // ATT_DH + kv))],
        out_specs=pl.BlockSpec((t, gw), lambda b, kv: (b, kv)),
        out_shape=jax.ShapeDtypeStruct((nseq * t, ATT_HEADS * ATT_DH), F32),
        compiler_params=_cparams(("arbitrary", "arbitrary")), name="ctx_attn",
    )(sink, proj, proj, proj)


def _lat_attn_kernel(sink_ref, q_ref, k_ref, v_ref, kc_ref, vc_ref, cos_ref, sin_ref, o_ref, kp_scr, vp_scr, *, t):
    kvh = pl.program_id(1)
    scale = ATT_DH ** -0.5
    band = QBLOCK + 2 * WINDOW
    cos, sin = cos_ref[...], sin_ref[...]

    def rope(x):
        return x * cos + pltpu.roll(x, ATT_DH // 2, 1) * sin

    zeros = jnp.zeros((WINDOW, ATT_DH), F32)
    kp_scr[0:WINDOW, :] = zeros
    kp_scr[WINDOW + t:, :] = zeros
    kp_scr[WINDOW:WINDOW + t, :] = rope(k_ref[...])
    vp_scr[0:WINDOW, :] = zeros
    vp_scr[WINDOW + t:, :] = zeros
    vp_scr[WINDOW:WINDOW + t, :] = v_ref[...]
    kc, vc = kc_ref[...], vc_ref[...]

    def body(i, carry):
        start = pl.multiple_of(i * QBLOCK, QBLOCK)
        ki = kp_scr[pl.ds(start, band), :]
        vi = vp_scr[pl.ds(start, band), :]
        qpos = start + _iota((QBLOCK, band), 0)
        kpos = start - WINDOW + _iota((QBLOCK, band), 1)
        valid = (jnp.abs(qpos - kpos) <= WINDOW) & (kpos >= 0) & (kpos < t)
        cs = cos_ref[pl.ds(start, QBLOCK), :]
        sn = sin_ref[pl.ds(start, QBLOCK), :]
        for gq in range(ATT_GROUP):
            q = q_ref[pl.ds(start, QBLOCK), gq * ATT_DH:(gq + 1) * ATT_DH]
            q = q * cs + pltpu.roll(q, ATT_DH // 2, 1) * sn
            s_band = jnp.where(valid, _bdot_nt(q, ki) * scale, -jnp.inf)
            s_ctx = _bdot_nt(q, kc) * scale
            sink = sink_ref[kvh * ATT_GROUP + gq]
            o_ref[pl.ds(start, QBLOCK), gq * ATT_DH:(gq + 1) * ATT_DH] = _softmax_sink_pv(
                [s_band, s_ctx], sink, [vi, vc])
        return carry

    lax.fori_loop(0, t // QBLOCK, body, 0)


def _lat_attn(proj, nseq, t, sink, cache_k, cache_v, cos2, sin2):
    gw = ATT_GROUP * ATT_DH
    past = cache_k.shape[0] // nseq
    return pl.pallas_call(
        functools.partial(_lat_attn_kernel, t=t),
        grid=(nseq, ATT_KVH),
        in_specs=[pl.BlockSpec(memory_space=pltpu.SMEM),
                  pl.BlockSpec((t, gw), lambda b, kv: (b, OD_Q // gw + kv)),
                  pl.BlockSpec((t, ATT_DH), lambda b, kv: (b, OD_K // ATT_DH + kv)),
                  pl.BlockSpec((t, ATT_DH), lambda b, kv: (b, OD_V // ATT_DH + kv)),
                  pl.BlockSpec((past, ATT_DH), lambda b, kv: (b, kv)),
                  pl.BlockSpec((past, ATT_DH), lambda b, kv: (b, kv)),
                  pl.BlockSpec((t, ATT_DH), lambda b, kv: (0, 0)),
                  pl.BlockSpec((t, ATT_DH), lambda b, kv: (0, 0))],
        out_specs=pl.BlockSpec((t, gw), lambda b, kv: (b, kv)),
        out_shape=jax.ShapeDtypeStruct((nseq * t, ATT_HEADS * ATT_DH), F32),
        scratch_shapes=[pltpu.VMEM((t + 2 * WINDOW, ATT_DH), F32), pltpu.VMEM((t + 2 * WINDOW, ATT_DH), F32)],
        compiler_params=_cparams(("arbitrary", "arbitrary")), name="lat_attn",
    )(sink, proj, proj, proj, cache_k, cache_v, cos2, sin2)


def _rope_tables(t):
    rows = t // GRID_W
    row = jnp.repeat(jnp.arange(rows), GRID_W).astype(F32)
    col = jnp.tile(jnp.arange(GRID_W), rows).astype(F32)
    n_freq = ATT_DH // 4
    inv = ROPE_BASE ** (-jnp.arange(n_freq, dtype=F32) / n_freq)
    ang = jnp.concatenate([row[:, None] * inv, col[:, None] * inv], axis=-1)
    cos, sin = jnp.cos(ang), jnp.sin(ang)
    return jnp.concatenate([cos, cos], axis=-1), jnp.concatenate([-sin, sin], axis=-1)


def _rwkv_kernel(*refs, t, has_state):
    (r_ref, k_ref, v_ref, wd_ref, ad_ref, gd_ref, mur_ref, muk_ref, muv_ref, muw_ref, mua_ref, mug_ref,
     w0_ref, wup_ref, a0_ref, aup_ref, gup_ref, kk_ref, ka_ref, rk_ref, lng_ref, lnb_ref) = refs[:22]
    k = 22
    s0_ref = st_ref = None
    if has_state:
        s0_ref = refs[k]
        k += 1
    o_ref = refs[k]
    k += 1
    if not has_state:
        st_ref = refs[k]
        k += 1
    pad_scr, r_scr, v_scr, aa_scr, wl_scr, kd_scr, bb_scr, yacc_scr, state_scr = refs[k:]
    nc = t // L

    def shift_mix(x_ref, mu_ref):
        x = x_ref[...]
        zeros = jnp.zeros((CONV_PAD, 128), F32)
        pad_scr[0:CONV_PAD, :] = zeros
        pad_scr[CONV_PAD + t:, :] = zeros
        pad_scr[CONV_PAD:CONV_PAD + t, :] = x
        nb = 0.5 * (pad_scr[CONV_PAD - 1:CONV_PAD - 1 + t, :] + pad_scr[CONV_PAD + 1:CONV_PAD + 1 + t, :])
        return x + mu_ref[...] * (nb - x)

    r = shift_mix(r_ref, mur_ref)
    kx = shift_mix(k_ref, muk_ref)
    v = shift_mix(v_ref, muv_ref)
    wd = shift_mix(wd_ref, muw_ref)
    ad = shift_mix(ad_ref, mua_ref)
    gd = shift_mix(gd_ref, mug_ref)

    lane_sq = _iota((128, 128), 1)
    row_sq = _iota((128, 128), 0)
    same_head = ((lane_sq >= RW_K) == (row_sq >= RW_K))
    bd_ones = same_head.astype(F32)

    def head_sum(x):
        return _dot(x, bd_ones, HI)

    gate = _bdot(jax.nn.sigmoid(gd), gup_ref[...])
    kkv = kx * kk_ref[...]
    kkn = kkv * lax.rsqrt(head_sum(kkv * kkv) + NORM_EPS)
    tw = jnp.tanh(wd)
    rows_lo = _iota((128, 128), 0) < RW_K
    kd_sum = jnp.zeros((t, 128), F32)
    for d in range(2):
        half = rows_lo if d == 0 else jnp.logical_not(rows_lo)
        w = w0_ref[d:d + 1, :] + _bdot(tw, jnp.where(half, wup_ref[...], 0.0))
        wl_scr[d] = -jnp.exp(-_softplus(-w) - 0.5)
        a = jax.nn.sigmoid(a0_ref[d:d + 1, :] + _bdot(ad, jnp.where(half, aup_ref[...], 0.0)))
        kd = kx * (1.0 + (a - 1.0) * ka_ref[...])
        kd_scr[d] = kd
        kd_sum = kd_sum + kd
        bb_scr[d] = kkn * a
    r_scr[...] = r
    v_scr[...] = v
    aa_scr[...] = -kkn

    dup = ((_iota((RW_K, 128), 1) & (RW_K - 1)) == _iota((RW_K, 128), 0)).astype(F32)
    row_head1 = _iota((2 * L, 128), 0) >= L
    lane_head1 = _iota((2 * L, 128), 1) >= RW_K
    own_lanes = row_head1 == lane_head1
    sq_r, sq_c = _iota((2 * L, 2 * L), 0), _iota((2 * L, 2 * L), 1)
    same_blk = (sq_r >= L) == (sq_c >= L)
    pos_r, pos_c = sq_r & (L - 1), sq_c & (L - 1)
    ones_l = jnp.ones((L, 128), F32)

    for d in range(2):
        rev = d == 1
        tri = _tri_masks(rev)[0].astype(F32)
        if rev:
            incl_bd, strict_bd = same_blk & (pos_c >= pos_r), same_blk & (pos_c > pos_r)
        else:
            incl_bd, strict_bd = same_blk & (pos_c <= pos_r), same_blk & (pos_c < pos_r)
        if has_state:
            stacked = s0_ref[0, d].reshape(2 * RW_K, RW_K)
            state_scr[...] = jnp.where(same_head, _dot(stacked, dup, HI), 0.0).T
        else:
            state_scr[...] = jnp.zeros((128, 128), F32)

        def body(ci, carry, d=d, rev=rev, incl_bd=incl_bd, strict_bd=strict_bd, tri=tri):
            c = (nc - 1 - ci) if rev else ci
            rows = pl.ds(pl.multiple_of(c * L, L), L)
            wl = wl_scr[d, rows, :]
            cin = _dot(tri, wl, HI)
            e_in, e_out = jnp.exp(cin), jnp.exp(-cin)
            ah = aa_scr[rows, :] * jnp.exp(cin - wl)
            bh = bb_scr[d, rows, :] * e_out
            kh = kd_scr[d, rows, :] * e_out
            rh = r_scr[rows, :] * e_in
            vv = v_scr[rows, :]
            h_st = state_scr[...]
            ah2 = jnp.where(own_lanes, jnp.concatenate([ah, ah], axis=0), 0.0)
            rh2 = jnp.where(own_lanes, jnp.concatenate([rh, rh], axis=0), 0.0)
            vv2 = jnp.concatenate([vv, vv], axis=0)
            prod = _dot_nt(jnp.concatenate([ah2, rh2], axis=0), jnp.concatenate([bh, bh, kh, kh], axis=0), HI)
            a_ab = jnp.where(strict_bd, prod[0:2 * L, 0:2 * L], 0.0)
            a_ak = jnp.where(strict_bd, prod[0:2 * L, 2 * L:], 0.0)
            r_ab = jnp.where(incl_bd, prod[2 * L:, 0:2 * L], 0.0)
            r_ak = jnp.where(incl_bd, prod[2 * L:, 2 * L:], 0.0)
            rhs = _dot(jnp.concatenate([ah2, a_ak], axis=1), jnp.concatenate([h_st, vv2], axis=0), HI)
            u2 = _dot(_inv_unit(-a_ab), rhs, HI)
            y2 = _dot(jnp.concatenate([rh2, r_ab, r_ak], axis=1), jnp.concatenate([h_st, u2, vv2], axis=0), HI)
            lane1 = _iota((L, 128), 1) >= RW_K
            u = jnp.where(lane1, u2[L:, :], u2[0:L, :])
            y = jnp.where(lane1, y2[L:, :], y2[0:L, :])
            upd = _dot_tn(jnp.concatenate([bh, kh], axis=0), jnp.concatenate([u, vv], axis=0), HI)
            decay = jnp.exp(_dot_tn(wl, ones_l, HI))
            state_scr[...] = (h_st + jnp.where(same_head, upd, 0.0)) * decay
            if d == 0:
                yacc_scr[rows, :] = y
            else:
                yacc_scr[rows, :] = yacc_scr[rows, :] + y
            return carry

        lax.fori_loop(0, nc, body, 0)
        if not has_state:
            st_ref[0, d] = _dot_nt(state_scr[...].T, dup, HI).reshape(2, RW_K, RW_K)

    y = yacc_scr[...]
    mean = head_sum(y) * (1.0 / RW_K)
    yc = y - mean
    var = head_sum(yc * yc) * (1.0 / RW_K)
    yn = yc * lax.rsqrt(var + RWKV_LN_EPS) * lng_ref[...] + lnb_ref[...]
    bonus = head_sum(r * (0.5 * kd_sum) * rk_ref[...]) * v
    o_ref[...] = (yn + bonus) * gate


def _rwkv(proj, nseq, t, prm, state):
    has_state = state is not None
    mu, w0, wup, a0, aup, gup, k_k, k_a, r_k, ln_g, ln_b = prm
    seq = lambda base: (lambda b, hp: (b, base // 128 + hp))
    fix = lambda base: (lambda b, hp: (b, base // 128))
    mu_at = lambda base: (lambda b, hp: (0, (base - OD_R) // 128 + hp))
    mu_fix = lambda base: (lambda b, hp: (0, (base - OD_R) // 128))
    col = lambda b, hp: (0, hp)
    in_specs = [
        pl.BlockSpec((t, 128), seq(OD_R)), pl.BlockSpec((t, 128), seq(OD_RK)), pl.BlockSpec((t, 128), seq(OD_RV)),
        pl.BlockSpec((t, 128), fix(OD_WD)), pl.BlockSpec((t, 128), fix(OD_AD)), pl.BlockSpec((t, 128), fix(OD_GD)),
        pl.BlockSpec((1, 128), mu_at(OD_R)), pl.BlockSpec((1, 128), mu_at(OD_RK)), pl.BlockSpec((1, 128), mu_at(OD_RV)),
        pl.BlockSpec((1, 128), mu_fix(OD_WD)), pl.BlockSpec((1, 128), mu_fix(OD_AD)), pl.BlockSpec((1, 128), mu_fix(OD_GD)),
        pl.BlockSpec((2, 128), col), pl.BlockSpec((128, 128), col),
        pl.BlockSpec((2, 128), col), pl.BlockSpec((128, 128), col),
        pl.BlockSpec((128, 128), col),
        pl.BlockSpec((1, 128), col), pl.BlockSpec((1, 128), col), pl.BlockSpec((1, 128), col),
        pl.BlockSpec((1, 128), col), pl.BlockSpec((1, 128), col),
    ]
    args = [proj] * 6 + [mu] * 6 + [w0, wup, a0, aup, gup, k_k, k_a, r_k, ln_g, ln_b]
    st_block = (1, 2, 2, RW_K, RW_K)
    st_map = lambda b, hp: (b, 0, hp, 0, 0)
    out_specs = [pl.BlockSpec((t, 128), lambda b, hp: (b, hp))]
    out_shape = [jax.ShapeDtypeStruct((nseq * t, RW_HEADS * RW_K), F32)]
    if has_state:
        in_specs.append(pl.BlockSpec(st_block, st_map))
        args.append(state)
    else:
        out_specs.append(pl.BlockSpec(st_block, st_map))
        out_shape.append(jax.ShapeDtypeStruct((nseq, 2, RW_HEADS, RW_K, RW_K), F32))
    scratch = [pltpu.VMEM((t + 2 * CONV_PAD, 128), F32), pltpu.VMEM((t, 128), F32), pltpu.VMEM((t, 128), F32),
               pltpu.VMEM((t, 128), F32), pltpu.VMEM((2, t, 128), F32), pltpu.VMEM((2, t, 128), F32),
               pltpu.VMEM((2, t, 128), F32), pltpu.VMEM((t, 128), F32), pltpu.VMEM((128, 128), F32)]
    return pl.pallas_call(
        functools.partial(_rwkv_kernel, t=t, has_state=has_state),
        grid=(nseq, RW_HEADS // 2), in_specs=in_specs, out_specs=out_specs, out_shape=out_shape,
        scratch_shapes=scratch, compiler_params=_cparams(("arbitrary", "arbitrary")), name="rwkv",
    )(*args)


def _even_w_in(w):
    za, xs, bm, cm = w[:, 0:1024], w[:, 1024:2048], w[:, 2048:2304], w[:, 2304:2560]
    dt, q, kk, v = w[:, 2560:2592], w[:, 2592:3616], w[:, 3616:4640], w[:, 4640:5664]
    zb, al, be = w[:, 5664:6688], w[:, 6688:6704], w[:, 6704:6720]
    pad = jnp.zeros((w.shape[0], EV_COLS - EV_SM - 64), w.dtype)
    return jnp.concatenate([za, xs, zb, q, kk, v, bm, cm, dt, al, be, pad], axis=1)


def _pad_row(parts):
    flat = jnp.concatenate([p.reshape(-1) for p in parts])
    return jnp.pad(flat, (0, 128 - flat.shape[0])).reshape(1, 128)


def kernel(x_prompt, x_sample, state_ssd, state_gdn, cache_k, cache_v, state_rwkv, c, c_ctx, ada_w, ada_b, norm1_g, norm2_g, ffn_w1, ffn_w2, final_g, ev_w_in, ev_w_out, ssd_conv_w, ssd_conv_b, ssd_a_log, ssd_dt_bias, ssd_d, ssd_norm_g, gdn_conv_w, gdn_a_log, gdn_dt_bias, gdn_norm_g, od_w_in, od_w_out, attn_sink, rwkv_mu, rwkv_w0, rwkv_w_up, rwkv_a0, rwkv_a_up, rwkv_g_up, rwkv_k_k, rwkv_k_a, rwkv_r_k, rwkv_ln_g, rwkv_ln_b):
    bp, tp, _ = x_prompt.shape
    bs, ts, _ = x_sample.shape
    streams = [(x_prompt.reshape(bp * tp, D), bp, tp, None), (x_sample.reshape(bs * ts, D), bs, ts, ts)]

    cvec = jnp.concatenate([c_ctx[None, :], c, jnp.zeros((16 - 1 - bs, D), F32)], axis=0)
    mod = _adaln(cvec, ada_w, ada_b)

    w_in0 = _even_w_in(ev_w_in[0]).astype(BF16)
    w_out0 = ev_w_out[0].astype(BF16)
    bias_row = _pad_row([ssd_dt_bias[0], gdn_dt_bias[0]])
    alog_row = _pad_row([ssd_a_log[0], gdn_a_log[0]])
    dskip_row = jnp.repeat(ssd_d[0], SSD_P).reshape(1, SSD_HEADS * SSD_P)
    w1_0, w2_0 = ffn_w1[0].astype(BF16), ffn_w2[0].astype(BF16)
    w_in1 = od_w_in[0].astype(BF16)
    w_out1 = od_w_out[0].astype(BF16)
    w1_1, w2_1 = ffn_w1[1].astype(BF16), ffn_w2[1].astype(BF16)
    rw_prm = (rwkv_mu[0].reshape(1, -1), rwkv_w0[0], rwkv_w_up[0].reshape(2 * 64, -1), rwkv_a0[0],
              rwkv_a_up[0].reshape(2 * 64, -1), rwkv_g_up[0], rwkv_k_k[0].reshape(1, -1), rwkv_k_a[0].reshape(1, -1),
              rwkv_r_k[0].reshape(1, -1), rwkv_ln_g[0].reshape(1, -1), rwkv_ln_b[0].reshape(1, -1))
    cos2, sin2 = _rope_tables(ts)
    past = cache_k.shape[2]
    ck = cache_k[:, 0].reshape(bs * past, ATT_KVH * ATT_DH)
    cv = cache_v[:, 0].reshape(bs * past, ATT_KVH * ATT_DH)

    outs = []
    for x, nseq, t, sample_t in streams:
        latent = sample_t is not None
        m0 = mod[0]
        proj = _inproj(x, norm1_g[0:1], m0, w_in0, sample_t)
        s_ssd = state_ssd[:, 0] if latent else None
        s_gdn = state_gdn[:, 0] if latent else None
        r_ssd = _ssd(proj, nseq, t, ssd_conv_w[0], ssd_conv_b[0].reshape(1, -1), bias_row, alog_row, dskip_row,
                     ssd_norm_g[0].reshape(1, -1), s_ssd)
        r_gdn = _gdn(proj, nseq, t, gdn_conv_w[0], bias_row, alog_row, gdn_norm_g[0].reshape(1, -1), s_gdn)
        x = _outproj(r_ssd[0], r_gdn[0], x, m0, w_out0, sample_t)
        x = _ffn(x, norm2_g[0:1], m0, w1_0, w2_0, final_g.reshape(1, D), sample_t, final=False)
        m1 = mod[1]
        proj = _inproj(x, norm1_g[1:2], m1, w_in1, sample_t)
        if latent:
            att = _lat_attn(proj, nseq, t, attn_sink[0], ck, cv, cos2, sin2)
            r_rw = _rwkv(proj, nseq, t, rw_prm, state_rwkv[:, 0])
        else:
            att = _ctx_attn(proj, nseq, t, attn_sink[0])
            r_rw = _rwkv(proj, nseq, t, rw_prm, None)
        x = _outproj(att, r_rw[0], x, m1, w_out1, sample_t)
        x = _ffn(x, norm2_g[1:2], m1, w1_1, w2_1, final_g.reshape(1, D), sample_t, final=True)
        outs.append((x.reshape(nseq, t, D), r_ssd, r_gdn, proj, r_rw))

    (y_prompt, p_ssd, p_gdn, p_proj, p_rw), (y_sample, _, _, _, _) = outs
    new_k = p_proj[:, OD_K:OD_V].reshape(bp, 1, tp, ATT_KVH, ATT_DH)
    new_v = p_proj[:, OD_V:OD_R].reshape(bp, 1, tp, ATT_KVH, ATT_DH)
    return (y_prompt, y_sample, p_ssd[1][:, None], p_gdn[1][:, None], new_k, new_v, p_rw[1][:, None])
```

```python
import functools

import numpy as np
import jax
import jax.numpy as jnp
from jax import lax
from jax.experimental import pallas as pl
from jax.experimental.pallas import tpu as pltpu

F32 = jnp.float32
BF16 = jnp.bfloat16
HI = lax.Precision.HIGHEST

D = 1024
NORM_EPS = 1e-6
L = 64
PREP_GROUP = 4
CONV_W = 5
CONV_PAD = 8

SSD_HEADS, SSD_P, SSD_N, SSD_GROUPS = 16, 64, 128, 2
SSD_GW = SSD_HEADS // SSD_GROUPS * SSD_P
GDN_HEADS, GDN_K = 8, 128
ATT_HEADS, ATT_KVH, ATT_DH, ATT_GROUP = 8, 2, 128, 4
WINDOW, QBLOCK, GRID_W, ROPE_BASE = 128, 128, 64, 10000.0
RW_HEADS, RW_K = 16, 64
RWKV_LN_EPS = 64e-5
FFN_H = 2816
FFN_TN = 256

EV_ZA, EV_XS, EV_ZB, EV_Q, EV_K, EV_V, EV_B, EV_C, EV_SM, EV_COLS = 0, 1024, 2048, 3072, 4096, 5120, 6144, 6400, 6656, 6784
SM_DT, SM_ALPHA, SM_BETA = 0, 32, 48
OD_Q, OD_K, OD_V, OD_R, OD_RK, OD_RV, OD_WD, OD_AD, OD_GD, OD_COLS = 0, 1024, 1280, 1536, 2560, 3584, 4608, 4736, 4864, 4992

VMEM_LIMIT = 56 * 1024 * 1024


def _cparams(sem):
    return pltpu.CompilerParams(dimension_semantics=sem, vmem_limit_bytes=VMEM_LIMIT)


def _dot(a, b, prec=None):
    return jnp.dot(a, b, preferred_element_type=F32, precision=prec)


def _dot_nt(a, b, prec=None):
    return lax.dot_general(a, b, (((1,), (1,)), ((), ())), preferred_element_type=F32, precision=prec)


def _dot_tn(a, b, prec=None):
    return lax.dot_general(a, b, (((0,), (0,)), ((), ())), preferred_element_type=F32, precision=prec)


def _bdot(a, b):
    return _dot(a.astype(BF16), b.astype(BF16))


def _bdot_nt(a, b):
    return _dot_nt(a.astype(BF16), b.astype(BF16))


def _bdot_tn(a, b):
    return _dot_tn(a.astype(BF16), b.astype(BF16))


def _silu(x):
    return x * jax.nn.sigmoid(x)


def _softplus(x):
    return jnp.maximum(x, 0.0) + jnp.log(1.0 + jnp.exp(-jnp.abs(x)))


def _iota(shape, dim):
    return lax.broadcasted_iota(jnp.int32, shape, dim)


def _tri_masks(rev):
    r, c = _iota((L, L), 0), _iota((L, L), 1)
    if rev:
        return c >= r, c > r
    return c <= r, c < r


def _inv_unit(a):
    eye = (_iota(a.shape, 0) == _iota(a.shape, 1)).astype(F32)
    x = eye - a
    p = _dot(a, a, HI)
    n = 2
    while True:
        x = x + _dot(x, p, HI)
        n *= 2
        if n >= L:
            break
        p = _dot(p, p, HI)
    return x


def _split3(x):
    hi = x.astype(BF16)
    r1 = x - hi.astype(F32)
    mid = r1.astype(BF16)
    lo = (r1 - mid.astype(F32)).astype(BF16)
    return hi, mid, lo


def _dot01_l(m01, x):
    mb = m01.astype(BF16)
    return _dot(jnp.concatenate([mb, mb, mb], axis=1), jnp.concatenate(_split3(x), axis=0))


def _dot01_r(x, m01):
    mb = m01.astype(BF16)
    return _dot(jnp.concatenate(_split3(x), axis=1), jnp.concatenate([mb, mb, mb], axis=0))


def _dot01_tn(x, m01):
    mb = m01.astype(BF16)
    return _dot_tn(jnp.concatenate(_split3(x), axis=0), jnp.concatenate([mb, mb, mb], axis=0))


def _dot3(a, b):
    ah = a.astype(BF16)
    al = (a - ah.astype(F32)).astype(BF16)
    bh = b.astype(BF16)
    bl = (b - bh.astype(F32)).astype(BF16)
    return _dot(ah, bh) + _dot(ah, bl) + _dot(al, bh)


def _to_bd(m):
    n = m.shape[1] // L
    blk = jnp.right_shift(_iota(m.shape, 1), L.bit_length() - 1)
    return jnp.concatenate([jnp.where(blk == j, m, 0.0) for j in range(n)], axis=0)


def _inv_unit_packed_multi(mats, mm):
    eye = ((_iota(mats[0].shape, 1) & (L - 1)) == _iota(mats[0].shape, 0)).astype(F32)
    xs = [eye - a for a in mats]
    ps = [mm(a, _to_bd(a)) for a in mats]
    n = 2
    while True:
        xs = [x + mm(x, _to_bd(p)) for x, p in zip(xs, ps)]
        n *= 2
        if n >= L:
            break
        ps = [mm(p, _to_bd(p)) for p in ps]
    return xs


def _conv_silu(x, w_ref, b, pad_ref, t):
    c = x.shape[1]
    zeros = jnp.zeros((CONV_PAD, c), F32)
    pad_ref[0:CONV_PAD, 0:c] = zeros
    pad_ref[CONV_PAD + t:2 * CONV_PAD + t, 0:c] = zeros
    pad_ref[CONV_PAD:CONV_PAD + t, 0:c] = x
    acc = None
    for j in range(CONV_W):
        off = CONV_PAD - CONV_W // 2 + j
        term = pad_ref[off:off + t, 0:c] * w_ref[j:j + 1, :]
        acc = term if acc is None else acc + term
    if b is not None:
        acc = acc + b
    return _silu(acc)


def _adaln_kernel(c_ref, w_ref, b_ref, o_ref):
    c = c_ref[...]
    o_ref[0] = _bdot(_silu(c), w_ref[0]) + b_ref[0]


def _adaln(cvec, ada_w, ada_b):
    depth, _, n = ada_w.shape
    tn = 1536
    out = pl.pallas_call(
        _adaln_kernel,
        grid=(depth, n // tn),
        in_specs=[pl.BlockSpec((16, D), lambda l, j: (0, 0)),
                  pl.BlockSpec((1, D, tn), lambda l, j: (l, 0, j)),
                  pl.BlockSpec((1, 1, tn), lambda l, j: (l, 0, j))],
        out_specs=pl.BlockSpec((1, 16, tn), lambda l, j: (l, 0, j)),
        out_shape=jax.ShapeDtypeStruct((depth, 16, n), F32),
        compiler_params=_cparams(("arbitrary", "arbitrary")),
        name="adaln",
    )(cvec, ada_w, ada_b.reshape(depth, 1, n))
    return out.reshape(depth, 16, 6, D)


def _mod_row_map(tm, t):
    if t is None:
        return lambda i: (0, 0, 0)
    per = t // tm
    return lambda i: (1 + i // per, 0, 0)


def _norm_mod(x, g, m, shift_idx, scale_idx):
    var = jnp.mean(x * x, axis=-1, keepdims=True)
    y = x * lax.rsqrt(var + NORM_EPS) * g
    return y * (1.0 + m[scale_idx:scale_idx + 1]) + m[shift_idx:shift_idx + 1]


def _inproj_kernel(x_ref, g_ref, mod_ref, w_ref, o_ref, *, tn):
    h = _norm_mod(x_ref[...], g_ref[...], mod_ref[0], 0, 1).astype(BF16)
    n = w_ref.shape[1]
    for j in range(0, n, tn):
        w = min(tn, n - j)
        o_ref[:, j:j + w] = _dot(h, w_ref[:, j:j + w])


def _inproj(x, g, mod, w, sample_t, tm=256, tn=512):
    rows, n = x.shape[0], w.shape[1]
    return pl.pallas_call(
        functools.partial(_inproj_kernel, tn=tn),
        grid=(rows // tm,),
        in_specs=[pl.BlockSpec((tm, D), lambda i: (i, 0)),
                  pl.BlockSpec((1, D), lambda i: (0, 0)),
                  pl.BlockSpec((1, 6, D), _mod_row_map(tm, sample_t)),
                  pl.BlockSpec((D, n), lambda i: (0, 0))],
        out_specs=pl.BlockSpec((tm, n), lambda i: (i, 0)),
        out_shape=jax.ShapeDtypeStruct((rows, n), F32),
        compiler_params=_cparams(("arbitrary",)),
        name="inproj",
    )(x, g, mod, w)


def _outproj_kernel(a1_ref, a2_ref, x_ref, mod_ref, w_ref, o_ref):
    k1 = a1_ref.shape[1]
    acc = _dot(a1_ref[...].astype(BF16), w_ref[0:k1, :])
    acc = acc + _dot(a2_ref[...].astype(BF16), w_ref[k1:, :])
    o_ref[...] = x_ref[...] + mod_ref[0][2:3] * acc


def _outproj(a1, a2, x, mod, w, sample_t, tm=256):
    rows = x.shape[0]
    return pl.pallas_call(
        _outproj_kernel,
        grid=(rows // tm,),
        in_specs=[pl.BlockSpec((tm, a1.shape[1]), lambda i: (i, 0)),
                  pl.BlockSpec((tm, a2.shape[1]), lambda i: (i, 0)),
                  pl.BlockSpec((tm, D), lambda i: (i, 0)),
                  pl.BlockSpec((1, 6, D), _mod_row_map(tm, sample_t)),
                  pl.BlockSpec(w.shape, lambda i: (0, 0))],
        out_specs=pl.BlockSpec((tm, D), lambda i: (i, 0)),
        out_shape=jax.ShapeDtypeStruct((rows, D), F32),
        compiler_params=_cparams(("arbitrary",)),
        name="outproj",
    )(a1, a2, x, mod, w)


def _ffn_kernel(x_ref, g_ref, mod_ref, w1_ref, w2_ref, fg_ref, o_ref, *, final):
    x = x_ref[...]
    m = mod_ref[0]
    h = _norm_mod(x, g_ref[...], m, 3, 4).astype(BF16)
    acc = jnp.zeros(x.shape, F32)
    for j in range(0, FFN_H, FFN_TN):
        gate = _dot(h, w1_ref[:, j:j + FFN_TN])
        up = _dot(h, w1_ref[:, FFN_H + j:FFN_H + j + FFN_TN])
        acc = acc + _dot((_silu(gate) * up).astype(BF16), w2_ref[j:j + FFN_TN, :])
    y = x + m[5:6] * acc
    if final:
        var = jnp.mean(y * y, axis=-1, keepdims=True)
        y = y * lax.rsqrt(var + NORM_EPS) * fg_ref[...]
    o_ref[...] = y


def _ffn(x, g, mod, w1, w2, final_g, sample_t, final, tm=256):
    rows = x.shape[0]
    return pl.pallas_call(
        functools.partial(_ffn_kernel, final=final),
        grid=(rows // tm,),
        in_specs=[pl.BlockSpec((tm, D), lambda i: (i, 0)),
                  pl.BlockSpec((1, D), lambda i: (0, 0)),
                  pl.BlockSpec((1, 6, D), _mod_row_map(tm, sample_t)),
                  pl.BlockSpec(w1.shape, lambda i: (0, 0)),
                  pl.BlockSpec(w2.shape, lambda i: (0, 0)),
                  pl.BlockSpec((1, D), lambda i: (0, 0))],
        out_specs=pl.BlockSpec((tm, D), lambda i: (i, 0)),
        out_shape=jax.ShapeDtypeStruct((rows, D), F32),
        compiler_params=_cparams(("arbitrary",)),
        name="ffn",
    )(x, g, mod, w1, w2, final_g)


def _ssd_kernel(*refs, t, has_state):
    (za_ref, xs_ref, bm_ref, cm_ref, sm_ref, cwx_ref, cwb_ref, cwc_ref, cbx_ref, cbb_ref, cbc_ref,
     bias_ref, alog_ref, dsk_ref, ng_ref) = refs[:15]
    k = 15
    s0_ref = st_ref = None
    if has_state:
        s0_ref = refs[k]
        k += 1
    y_ref = refs[k]
    k += 1
    if not has_state:
        st_ref = refs[k]
        k += 1
    pad_scr, xs_scr, bb_scr, cc_scr, dt_scr, da_scr, dasm_scr, yacc_scr, state_scr = refs[k:]

    g = pl.program_id(1)
    nc = t // L
    npair = SSD_GW // 128

    xs_scr[...] = _conv_silu(xs_ref[...], cwx_ref, cbx_ref[...], pad_scr, t)
    bb_scr[...] = _conv_silu(bm_ref[...], cwb_ref, cbb_ref[...], pad_scr, t)
    cc_scr[...] = _conv_silu(cm_ref[...], cwc_ref, cbc_ref[...], pad_scr, t)

    dtv = _softplus(sm_ref[...] + bias_ref[...])
    dav = dtv * (-jnp.exp(alog_ref[...]))
    heads_per_group = SSD_HEADS // SSD_GROUPS
    for d in range(2):
        first = SM_DT + d * SSD_HEADS + g * heads_per_group
        src = _iota((128, SSD_GW), 0)
        e = (src == first + jnp.right_shift(_iota((128, SSD_GW), 1), 6)).astype(F32)
        dt_scr[d] = _dot01_r(dtv, e)
        da_scr[d] = _dot01_r(dav, e)
        sel = (_iota((128, 128), 0) == first + _iota((128, 128), 1)) & (_iota((128, 128), 1) < heads_per_group)
        dasm_scr[d] = _dot01_r(dav, sel.astype(F32))

    lane128 = _iota((L, 128), 1)
    for d in range(2):
        if has_state:
            state_scr[d] = s0_ref[0, d].reshape(SSD_GW, SSD_N).T
        else:
            state_scr[d] = jnp.zeros((SSD_N, SSD_GW), F32)

    def body(ci, carry):
        for d in range(2):
            rev = d == 1
            incl = _tri_masks(rev)[0]
            tri = incl.astype(F32)
            tri_t = _tri_masks(not rev)[0].astype(F32)
            last = 0 if rev else L - 1
            c = (nc - 1 - ci) if rev else ci
            r0 = pl.multiple_of(c * L, L)
            rows = pl.ds(r0, L)
            x = xs_scr[rows, :]
            bc = bb_scr[rows, :]
            cc = cc_scr[rows, :]
            acum = _dot01_l(tri, da_scr[d, rows, :])
            acum_t = _dot01_tn(dasm_scr[d, rows, :], tri_t)
            xdt = x * dt_scr[d, rows, :]
            cb = _bdot_nt(cc, bc)
            total = acum[last:last + 1, :]
            s_in = state_scr[d]
            y_inter = _bdot(cc, s_in) * jnp.exp(acum)
            pieces = []
            for j in range(npair):
                xp = xdt[:, j * 128:(j + 1) * 128]
                yp = None
                for hh in range(2):
                    h = 2 * j + hh
                    col = acum[:, h * SSD_P:h * SSD_P + 1]
                    row = acum_t[h:h + 1, :]
                    dec = jnp.where(incl, jnp.exp(jnp.minimum(col - row, 0.0)), 0.0)
                    mine = (lane128 >= SSD_P) if hh else (lane128 < SSD_P)
                    term = _bdot(cb * dec, jnp.where(mine, xp, 0.0))
                    yp = term if yp is None else yp + term
                pieces.append(yp)
            yacc_scr[d, rows, :] = jnp.concatenate(pieces, axis=1) + y_inter
            wx = jnp.exp(total - acum) * xdt
            state_scr[d] = s_in * jnp.exp(total) + _bdot_tn(bc, wx)
        return carry

    lax.fori_loop(0, nc, body, 0)
    if not has_state:
        for d in range(2):
            st_ref[0, d] = state_scr[d].T.reshape(heads_per_group, SSD_P, SSD_N)

    y = yacc_scr[0] + yacc_scr[1] + dsk_ref[...] * xs_scr[...]
    y = y * _silu(za_ref[...])
    var = jnp.mean(y * y, axis=-1, keepdims=True)
    y_ref[...] = y * lax.rsqrt(var + NORM_EPS) * ng_ref[...]


def _ssd(proj, nseq, t, conv_w, conv_b, bias_row, alog_row, dskip_row, norm_g, state):
    has_state = state is not None
    seq = lambda blk: (lambda b, g: (b, blk(g)))
    const = lambda blk: (lambda b, g: (0, blk(g)))
    in_specs = [
        pl.BlockSpec((t, SSD_GW), seq(lambda g: EV_ZA // SSD_GW + g)),
        pl.BlockSpec((t, SSD_GW), seq(lambda g: EV_XS // SSD_GW + g)),
        pl.BlockSpec((t, 128), seq(lambda g: EV_B // 128 + g)),
        pl.BlockSpec((t, 128), seq(lambda g: EV_C // 128 + g)),
        pl.BlockSpec((t, 128), seq(lambda g: EV_SM // 128)),
        pl.BlockSpec((CONV_W, SSD_GW), const(lambda g: g)),
        pl.BlockSpec((CONV_W, 128), const(lambda g: 1024 // 128 + g)),
        pl.BlockSpec((CONV_W, 128), const(lambda g: 1280 // 128 + g)),
        pl.BlockSpec((1, SSD_GW), const(lambda g: g)),
        pl.BlockSpec((1, 128), const(lambda g: 1024 // 128 + g)),
        pl.BlockSpec((1, 128), const(lambda g: 1280 // 128 + g)),
        pl.BlockSpec((1, 128), const(lambda g: 0)),
        pl.BlockSpec((1, 128), const(lambda g: 0)),
        pl.BlockSpec((1, SSD_GW), const(lambda g: g)),
        pl.BlockSpec((1, SSD_GW), const(lambda g: g)),
    ]
    args = [proj, proj, proj, proj, proj, conv_w, conv_w, conv_w, conv_b, conv_b, conv_b,
            bias_row, alog_row, dskip_row, norm_g]
    st_block = (1, 2, SSD_HEADS // SSD_GROUPS, SSD_P, SSD_N)
    st_map = lambda b, g: (b, 0, g, 0, 0)
    out_specs = [pl.BlockSpec((t, SSD_GW), lambda b, g: (b, g))]
    out_shape = [jax.ShapeDtypeStruct((nseq * t, SSD_HEADS * SSD_P), F32)]
    if has_state:
        in_specs.append(pl.BlockSpec(st_block, st_map))
        args.append(state)
    else:
        out_specs.append(pl.BlockSpec(st_block, st_map))
        out_shape.append(jax.ShapeDtypeStruct((nseq, 2, SSD_HEADS, SSD_P, SSD_N), F32))
    scratch = [pltpu.VMEM((t + 2 * CONV_PAD, SSD_GW), F32), pltpu.VMEM((t, SSD_GW), F32),
               pltpu.VMEM((t, 128), F32), pltpu.VMEM((t, 128), F32),
               pltpu.VMEM((2, t, SSD_GW), F32), pltpu.VMEM((2, t, SSD_GW), F32), pltpu.VMEM((2, t, 128), F32),
               pltpu.VMEM((2, t, SSD_GW), F32), pltpu.VMEM((2, SSD_N, SSD_GW), F32)]
    return pl.pallas_call(
        functools.partial(_ssd_kernel, t=t, has_state=has_state),
        grid=(nseq, SSD_GROUPS), in_specs=in_specs, out_specs=out_specs, out_shape=out_shape,
        scratch_shapes=scratch, compiler_params=_cparams(("arbitrary", "arbitrary")), name="ssd",
    )(*args)


def _gdn_kernel(*refs, t, has_state):
    (q_ref, k_ref, v_ref, zb_ref, sm_ref, cwq_ref, cwk_ref, cwv_ref, bias_ref, alog_ref, ng_ref) = refs[:11]
    k = 11
    s0_ref = st_ref = None
    if has_state:
        s0_ref = refs[k]
        k += 1
    o_ref = refs[k]
    k += 1
    if not has_state:
        st_ref = refs[k]
        k += 1
    (pad_scr, q_scr, k_scr, v_scr, g_scr, b_scr, m2_scr, c2_scr, ah_scr, bc_scr, dg_scr,
     oacc_scr, state_scr) = refs[k:]

    h = pl.program_id(1)
    nc = t // L

    def l2n(x):
        return x * lax.rsqrt(jnp.sum(x * x, axis=-1, keepdims=True) + NORM_EPS)

    q_scr[...] = l2n(_conv_silu(q_ref[...], cwq_ref, None, pad_scr, t)) * (GDN_K ** -0.5)
    k_scr[...] = l2n(_conv_silu(k_ref[...], cwk_ref, None, pad_scr, t))
    v_scr[...] = _conv_silu(v_ref[...], cwv_ref, None, pad_scr, t)

    sm = sm_ref[...]
    gl = -jnp.exp(alog_ref[...]) * _softplus(sm + bias_ref[...])
    bt = jax.nn.sigmoid(sm)
    lane = _iota((t, 128), 1)
    for d in range(2):
        gcol = jnp.sum(jnp.where(lane == SM_ALPHA + d * GDN_HEADS + h, gl, 0.0), axis=1, keepdims=True)
        bcol = jnp.sum(jnp.where(lane == SM_BETA + d * GDN_HEADS + h, bt, 0.0), axis=1, keepdims=True)
        g_scr[d] = jnp.broadcast_to(gcol, (t, 128))
        b_scr[d] = jnp.broadcast_to(bcol, (t, 128))

    masks = [_tri_masks(False), _tri_masks(True)]

    def prep(gi, carry):
        cs = [gi * PREP_GROUP + j for j in range(PREP_GROUP)]
        rows = [pl.ds(pl.multiple_of(c * L, L), L) for c in cs]
        qkv = [(q_scr[r, :], k_scr[r, :], v_scr[r, :]) for r in rows]
        qkk = [_bdot_nt(jnp.concatenate([qc, kc], axis=0), kc) for qc, kc, _ in qkv]
        gbs = [[(g_scr[d, r, :], b_scr[d, r, :]) for d in range(2)] for r in rows]
        gam = [[_dot01_l(masks[d][0].astype(F32), gbs[j][d][0]) for d in range(2)] for j in range(PREP_GROUP)]
        gam_row = [[_dot01_tn(gbs[j][d][0], masks[1 - d][0].astype(F32))[0:1, :] for d in range(2)]
                   for j in range(PREP_GROUP)]
        per = []
        for j in range(PREP_GROUP):
            qc, kc, vc = qkv[j]
            row = []
            for d in range(2):
                incl, strict = masks[d]
                bb = gbs[j][d][1]
                g = gam[j][d]
                dec = jnp.where(incl, jnp.exp(jnp.minimum(g[:, 0:1] - gam_row[j][d], 0.0)), 0.0)
                a_mat = jnp.where(strict, bb[:, 0:1] * qkk[j][L:, :] * dec, 0.0)
                eg = jnp.exp(g)
                last = 0 if d else L - 1
                g_last = g[last:last + 1, :]
                rhs = jnp.concatenate([bb * vc, bb * kc * eg], axis=1)
                row.append((a_mat, rhs, qkk[j][0:L, :] * dec, qc * eg, kc * jnp.exp(g_last - g), g_last))
            per.append(row)
        packs = [jnp.concatenate([per[j][0][0], per[j][1][0], per[j + 1][0][0], per[j + 1][1][0]], axis=1)
                 for j in range(0, PREP_GROUP, 2)]
        inv = _inv_unit_packed_multi(packs, _dot3)
        t_inv = [_to_bd(inv[j // 2][:, (j % 2) * 2 * L:(j % 2 + 1) * 2 * L]) for j in range(PREP_GROUP)]
        uw = [_dot3(t_inv[j], jnp.concatenate([per[j][0][1], per[j][1][1]], axis=0))
              for j in range(PREP_GROUP)]
        qk_uw = [_bdot(_to_bd(jnp.concatenate([per[j][0][2], per[j][1][2]], axis=1)), uw[j])
                 for j in range(PREP_GROUP)]
        ke_uw = [[_bdot_tn(per[j][d][4], uw[j][d * L:(d + 1) * L, :]) for d in range(2)]
                 for j in range(PREP_GROUP)]
        for j, c in enumerate(cs):
            srows = pl.ds(pl.multiple_of(c * GDN_K, GDN_K), GDN_K)
            drows = pl.ds(pl.multiple_of(c * 8, 8), 8)
            for d in range(2):
                half = slice(d * L, (d + 1) * L)
                m2_scr[d, rows[j], :] = per[j][d][3] - qk_uw[j][half, GDN_K:]
                c2_scr[d, rows[j], :] = qk_uw[j][half, 0:GDN_K]
                ah_scr[d, srows, :] = -ke_uw[j][d][:, GDN_K:]
                bc_scr[d, srows, :] = ke_uw[j][d][:, 0:GDN_K]
                dg_scr[d, drows, :] = jnp.broadcast_to(jnp.exp(per[j][d][5]), (8, 128))
        return carry

    lax.fori_loop(0, nc // PREP_GROUP, prep, 0)

    for d in range(2):
        state_scr[d] = s0_ref[0, d, 0] if has_state else jnp.zeros((GDN_K, GDN_K), F32)

    def scan(i, carry):
        for d in range(2):
            c = (nc - 1 - i) if d else i
            rows = pl.ds(pl.multiple_of(c * L, L), L)
            srows = pl.ds(pl.multiple_of(c * GDN_K, GDN_K), GDN_K)
            drows = pl.ds(pl.multiple_of(c * 8, 8), 8)
            s = state_scr[d]
            res = _bdot(jnp.concatenate([m2_scr[d, rows, :], ah_scr[d, srows, :]], axis=0), s)
            oacc_scr[d, rows, :] = res[0:L, :] + c2_scr[d, rows, :]
            state_scr[d] = dg_scr[d, drows, :][0:1, :] * s + res[L:, :] + bc_scr[d, srows, :]
        return carry

    lax.fori_loop(0, nc, scan, 0)
    if not has_state:
        for d in range(2):
            st_ref[0, d, 0] = state_scr[d]

    o = oacc_scr[0] + oacc_scr[1]
    var = jnp.mean(o * o, axis=-1, keepdims=True)
    o_ref[...] = o * lax.rsqrt(var + NORM_EPS) * ng_ref[...] * _silu(zb_ref[...])


def _gdn(proj, nseq, t, conv_w, bias_row, alog_row, norm_g, state):
    has_state = state is not None
    seq = lambda base: (lambda b, h: (b, base // 128 + h))
    const = lambda base: (lambda b, h: (0, base // 128 + h))
    in_specs = [
        pl.BlockSpec((t, 128), seq(EV_Q)), pl.BlockSpec((t, 128), seq(EV_K)), pl.BlockSpec((t, 128), seq(EV_V)),
        pl.BlockSpec((t, 128), seq(EV_ZB)), pl.BlockSpec((t, 128), lambda b, h: (b, EV_SM // 128)),
        pl.BlockSpec((CONV_W, 128), const(0)), pl.BlockSpec((CONV_W, 128), const(1024)),
        pl.BlockSpec((CONV_W, 128), const(2048)),
        pl.BlockSpec((1, 128), lambda b, h: (0, 0)), pl.BlockSpec((1, 128), lambda b, h: (0, 0)),
        pl.BlockSpec((1, 128), lambda b, h: (0, 0)),
    ]
    args = [proj, proj, proj, proj, proj, conv_w, conv_w, conv_w, bias_row, alog_row, norm_g]
    st_block = (1, 2, 1, GDN_K, GDN_K)
    st_map = lambda b, h: (b, 0, h, 0, 0)
    out_specs = [pl.BlockSpec((t, 128), lambda b, h: (b, h))]
    out_shape = [jax.ShapeDtypeStruct((nseq * t, GDN_HEADS * GDN_K), F32)]
    if has_state:
        in_specs.append(pl.BlockSpec(st_block, st_map))
        args.append(state)
    else:
        out_specs.append(pl.BlockSpec(st_block, st_map))
        out_shape.append(jax.ShapeDtypeStruct((nseq, 2, GDN_HEADS, GDN_K, GDN_K), F32))
    nc = t // L
    scratch = [pltpu.VMEM((t + 2 * CONV_PAD, 128), F32), pltpu.VMEM((t, 128), F32), pltpu.VMEM((t, 128), F32),
               pltpu.VMEM((t, 128), F32), pltpu.VMEM((2, t, 128), F32), pltpu.VMEM((2, t, 128), F32),
               pltpu.VMEM((2, t, 128), F32), pltpu.VMEM((2, t, 128), F32),
               pltpu.VMEM((2, nc * GDN_K, GDN_K), F32), pltpu.VMEM((2, nc * GDN_K, GDN_K), F32),
               pltpu.VMEM((2, nc * 8, 128), F32),
               pltpu.VMEM((2, t, 128), F32), pltpu.VMEM((2, GDN_K, GDN_K), F32)]
    return pl.pallas_call(
        functools.partial(_gdn_kernel, t=t, has_state=has_state),
        grid=(nseq, GDN_HEADS), in_specs=in_specs, out_specs=out_specs, out_shape=out_shape,
        scratch_shapes=scratch, compiler_params=_cparams(("arbitrary", "arbitrary")), name="gdn",
    )(*args)


def _softmax_sink_pv(scores, sink, values):
    m = sink
    for s in scores:
        m = jnp.maximum(m, jnp.max(s, axis=-1, keepdims=True))
    den = jnp.exp(sink - m)
    ps = []
    for s in scores:
        p = jnp.exp(s - m)
        den = den + jnp.sum(p, axis=-1, keepdims=True)
        ps.append(p)
    out = None
    for p, v in zip(ps, values):
        term = _bdot(p / den, v)
        out = term if out is None else out + term
    return out


def _ctx_attn_kernel(sink_ref, q_ref, k_ref, v_ref, o_ref):
    kvh = pl.program_id(1)
    scale = ATT_DH ** -0.5
    kk, vv = k_ref[...], v_ref[...]
    for gq in range(ATT_GROUP):
        q = q_ref[:, gq * ATT_DH:(gq + 1) * ATT_DH]
        s = _bdot_nt(q, kk) * scale
        sink = sink_ref[kvh * ATT_GROUP + gq]
        o_ref[:, gq * ATT_DH:(gq + 1) * ATT_DH] = _softmax_sink_pv([s], sink, [vv])


def _ctx_attn(proj, nseq, t, sink):
    gw = ATT_GROUP * ATT_DH
    return pl.pallas_call(
        _ctx_attn_kernel,
        grid=(nseq, ATT_KVH),
        in_specs=[pl.BlockSpec(memory_space=pltpu.SMEM),
                  pl.BlockSpec((t, gw), lambda b, kv: (b, OD_Q // gw + kv)),
                  pl.BlockSpec((t, ATT_DH), lambda b, kv: (b, OD_K // ATT_DH + kv)),
                  pl.BlockSpec((t, ATT_DH), lambda b, kv: (b, OD_V // ATT_DH + kv))],
        out_specs=pl.BlockSpec((t, gw), lambda b, kv: (b, kv)),
        out_shape=jax.ShapeDtypeStruct((nseq * t, ATT_HEADS * ATT_DH), F32),
        compiler_params=_cparams(("arbitrary", "arbitrary")), name="ctx_attn",
    )(sink, proj, proj, proj)


def _lat_attn_kernel(sink_ref, q_ref, k_ref, v_ref, kc_ref, vc_ref, cos_ref, sin_ref, o_ref, kp_scr, vp_scr, *, t):
    kvh = pl.program_id(1)
    scale = ATT_DH ** -0.5
    band = QBLOCK + 2 * WINDOW
    cos, sin = cos_ref[...], sin_ref[...]

    def rope(x):
        return x * cos + pltpu.roll(x, ATT_DH // 2, 1) * sin

    zeros = jnp.zeros((WINDOW, ATT_DH), F32)
    kp_scr[0:WINDOW, :] = zeros
    kp_scr[WINDOW + t:, :] = zeros
    kp_scr[WINDOW:WINDOW + t, :] = rope(k_ref[...])
    vp_scr[0:WINDOW, :] = zeros
    vp_scr[WINDOW + t:, :] = zeros
    vp_scr[WINDOW:WINDOW + t, :] = v_ref[...]
    kc, vc = kc_ref[...], vc_ref[...]

    def body(i, carry):
        start = pl.multiple_of(i * QBLOCK, QBLOCK)
        ki = kp_scr[pl.ds(start, band), :]
        vi = vp_scr[pl.ds(start, band), :]
        qpos = start + _iota((QBLOCK, band), 0)
        kpos = start - WINDOW + _iota((QBLOCK, band), 1)
        valid = (jnp.abs(qpos - kpos) <= WINDOW) & (kpos >= 0) & (kpos < t)
        cs = cos_ref[pl.ds(start, QBLOCK), :]
        sn = sin_ref[pl.ds(start, QBLOCK), :]
        for gq in range(ATT_GROUP):
            q = q_ref[pl.ds(start, QBLOCK), gq * ATT_DH:(gq + 1) * ATT_DH]
            q = q * cs + pltpu.roll(q, ATT_DH // 2, 1) * sn
            s_band = jnp.where(valid, _bdot_nt(q, ki) * scale, -jnp.inf)
            s_ctx = _bdot_nt(q, kc) * scale
            sink = sink_ref[kvh * ATT_GROUP + gq]
            o_ref[pl.ds(start, QBLOCK), gq * ATT_DH:(gq + 1) * ATT_DH] = _softmax_sink_pv(
                [s_band, s_ctx], sink, [vi, vc])
        return carry

    lax.fori_loop(0, t // QBLOCK, body, 0)


def _lat_attn(proj, nseq, t, sink, cache_k, cache_v, cos2, sin2):
    gw = ATT_GROUP * ATT_DH
    past = cache_k.shape[0] // nseq
    return pl.pallas_call(
        functools.partial(_lat_attn_kernel, t=t),
        grid=(nseq, ATT_KVH),
        in_specs=[pl.BlockSpec(memory_space=pltpu.SMEM),
                  pl.BlockSpec((t, gw), lambda b, kv: (b, OD_Q // gw + kv)),
                  pl.BlockSpec((t, ATT_DH), lambda b, kv: (b, OD_K // ATT_DH + kv)),
                  pl.BlockSpec((t, ATT_DH), lambda b, kv: (b, OD_V // ATT_DH + kv)),
                  pl.BlockSpec((past, ATT_DH), lambda b, kv: (b, kv)),
                  pl.BlockSpec((past, ATT_DH), lambda b, kv: (b, kv)),
                  pl.BlockSpec((t, ATT_DH), lambda b, kv: (0, 0)),
                  pl.BlockSpec((t, ATT_DH), lambda b, kv: (0, 0))],
        out_specs=pl.BlockSpec((t, gw), lambda b, kv: (b, kv)),
        out_shape=jax.ShapeDtypeStruct((nseq * t, ATT_HEADS * ATT_DH), F32),
        scratch_shapes=[pltpu.VMEM((t + 2 * WINDOW, ATT_DH), F32), pltpu.VMEM((t + 2 * WINDOW, ATT_DH), F32)],
        compiler_params=_cparams(("arbitrary", "arbitrary")), name="lat_attn",
    )(sink, proj, proj, proj, cache_k, cache_v, cos2, sin2)


def _rope_tables(t):
    rows = t // GRID_W
    row = jnp.repeat(jnp.arange(rows), GRID_W).astype(F32)
    col = jnp.tile(jnp.arange(GRID_W), rows).astype(F32)
    n_freq = ATT_DH // 4
    inv = ROPE_BASE ** (-jnp.arange(n_freq, dtype=F32) / n_freq)
    ang = jnp.concatenate([row[:, None] * inv, col[:, None] * inv], axis=-1)
    cos, sin = jnp.cos(ang), jnp.sin(ang)
    return jnp.concatenate([cos, cos], axis=-1), jnp.concatenate([-sin, sin], axis=-1)


def _rwkv_kernel(*refs, t, has_state):
    (r_ref, k_ref, v_ref, wd_ref, ad_ref, gd_ref, mur_ref, muk_ref, muv_ref, muw_ref, mua_ref, mug_ref,
     w0_ref, wup_ref, a0_ref, aup_ref, gup_ref, kk_ref, ka_ref, rk_ref, lng_ref, lnb_ref) = refs[:22]
    k = 22
    s0_ref = st_ref = None
    if has_state:
        s0_ref = refs[k]
        k += 1
    o_ref = refs[k]
    k += 1
    if not has_state:
        st_ref = refs[k]
        k += 1
    (pad_scr, r_scr, v_scr, aa_scr, wl_scr, kd_scr, bb_scr, m2_scr, c2_scr, ah_scr, bc_scr, dg_scr,
     yacc_scr, state_scr) = refs[k:]
    nc = t // L

    def shift_mix(x_ref, mu_ref):
        x = x_ref[...]
        zeros = jnp.zeros((CONV_PAD, 128), F32)
        pad_scr[0:CONV_PAD, :] = zeros
        pad_scr[CONV_PAD + t:, :] = zeros
        pad_scr[CONV_PAD:CONV_PAD + t, :] = x
        nb = 0.5 * (pad_scr[CONV_PAD - 1:CONV_PAD - 1 + t, :] + pad_scr[CONV_PAD + 1:CONV_PAD + 1 + t, :])
        return x + mu_ref[...] * (nb - x)

    r = shift_mix(r_ref, mur_ref)
    kx = shift_mix(k_ref, muk_ref)
    v = shift_mix(v_ref, muv_ref)
    wd = shift_mix(wd_ref, muw_ref)
    ad = shift_mix(ad_ref, mua_ref)
    gd = shift_mix(gd_ref, mug_ref)

    lane_sq = _iota((128, 128), 1)
    row_sq = _iota((128, 128), 0)
    same_head = ((lane_sq >= RW_K) == (row_sq >= RW_K))
    bd_ones = same_head.astype(F32)

    def head_sum(x):
        return _dot01_r(x, bd_ones)

    gate = _bdot(jax.nn.sigmoid(gd), gup_ref[...])
    kkv = kx * kk_ref[...]
    kkn = kkv * lax.rsqrt(head_sum(kkv * kkv) + NORM_EPS)
    tw = jnp.tanh(wd)
    rows_lo = _iota((128, 128), 0) < RW_K
    kd_sum = jnp.zeros((t, 128), F32)
    for d in range(2):
        half = rows_lo if d == 0 else jnp.logical_not(rows_lo)
        w = w0_ref[d:d + 1, :] + _bdot(tw, jnp.where(half, wup_ref[...], 0.0))
        wl_scr[d] = -jnp.exp(-_softplus(-w) - 0.5)
        a = jax.nn.sigmoid(a0_ref[d:d + 1, :] + _bdot(ad, jnp.where(half, aup_ref[...], 0.0)))
        kd = kx * (1.0 + (a - 1.0) * ka_ref[...])
        kd_scr[d] = kd
        kd_sum = kd_sum + kd
        bb_scr[d] = kkn * a
    r_scr[...] = r
    v_scr[...] = v
    aa_scr[...] = -kkn

    dup = ((_iota((RW_K, 128), 1) & (RW_K - 1)) == _iota((RW_K, 128), 0)).astype(F32)
    row_head1 = _iota((2 * L, 128), 0) >= L
    lane_head1 = _iota((2 * L, 128), 1) >= RW_K
    own_lanes = row_head1 == lane_head1
    sq_r, sq_c = _iota((2 * L, 2 * L), 0), _iota((2 * L, 2 * L), 1)
    same_blk = (sq_r >= L) == (sq_c >= L)
    pos_r, pos_c = sq_r & (L - 1), sq_c & (L - 1)
    ones_l = jnp.ones((L, 128), F32)
    bd_masks = [(same_blk & (pos_c <= pos_r), same_blk & (pos_c < pos_r)),
                (same_blk & (pos_c >= pos_r), same_blk & (pos_c > pos_r))]
    mm, mm_nt, mm_tn = _bdot, _bdot_nt, _bdot_tn

    def two(x):
        return jnp.where(own_lanes, jnp.concatenate([x, x], axis=0), 0.0)

    def prep(gi, carry):
        cs = [gi * PREP_GROUP + j for j in range(PREP_GROUP)]
        cds = [(j, d) for j in range(PREP_GROUP) for d in range(2)]
        rows = [pl.ds(pl.multiple_of(c * L, L), L) for c in cs]
        srows = [pl.ds(pl.multiple_of(c * 2 * L, 2 * L), 2 * L) for c in cs]
        vv2 = [two(v_scr[r, :]) for r in rows]
        wl = {(j, d): wl_scr[d, rows[j], :] for j, d in cds}
        cin = {jd: _dot01_l(_tri_masks(jd[1] == 1)[0].astype(F32), wl[jd]) for jd in cds}
        decay = {jd: jnp.exp(_dot01_tn(wl[jd], ones_l)) for jd in cds}
        ops = {}
        for j, d in cds:
            e_in, e_out = jnp.exp(cin[j, d]), jnp.exp(-cin[j, d])
            ops[j, d] = (two(aa_scr[rows[j], :] * jnp.exp(cin[j, d] - wl[j, d])), two(r_scr[rows[j], :] * e_in),
                         two(bb_scr[d, rows[j], :] * e_out), two(kd_scr[d, rows[j], :] * e_out))
        prod = {jd: mm_nt(jnp.concatenate(ops[jd][0:2], axis=0), jnp.concatenate(ops[jd][2:4], axis=0)) for jd in cds}
        blk = {}
        for j, d in cds:
            incl_bd, strict_bd = bd_masks[d]
            p = prod[j, d]
            blk[j, d] = (jnp.where(strict_bd, p[0:2 * L, 0:2 * L], 0.0), jnp.where(strict_bd, p[0:2 * L, 2 * L:], 0.0),
                         jnp.where(incl_bd, p[2 * L:, 0:2 * L], 0.0), jnp.where(incl_bd, p[2 * L:, 2 * L:], 0.0))
        packs = [-jnp.concatenate([blk[j, d][0][0:L, :] + blk[j, d][0][L:, :] for d in range(2)], axis=1)
                 for j in range(PREP_GROUP)]
        inv = _inv_unit_packed_multi(packs, _dot3)
        t_inv = {(j, d): _to_bd(inv[j][:, d * 2 * L:(d + 1) * 2 * L]) for j, d in cds}
        kv = {(j, d): mm(jnp.concatenate(blk[j, d][1::2], axis=0), vv2[j]) for j, d in cds}
        m1c1 = {jd: _dot3(t_inv[jd], jnp.concatenate([ops[jd][0], kv[jd][0:2 * L, :]], axis=1)) for jd in cds}
        rm = {jd: mm(blk[jd][2], m1c1[jd]) for jd in cds}
        bm = {jd: mm_tn(ops[jd][2], m1c1[jd]) for jd in cds}
        kv2 = {(j, d): mm_tn(ops[j, d][3], vv2[j]) for j, d in cds}
        for j, d in cds:
            m2_scr[d, srows[j], :] = ops[j, d][1] + rm[j, d][:, 0:128]
            c2_scr[d, srows[j], :] = rm[j, d][:, 128:] + kv[j, d][2 * L:, :]
            ah_scr[d, srows[j], :] = decay[j, d] * bm[j, d][:, 0:128]
            bc_scr[d, srows[j], :] = decay[j, d] * (bm[j, d][:, 128:] + kv2[j, d])
            dg_scr[d, srows[j], :] = decay[j, d]
        return carry

    lax.fori_loop(0, nc // PREP_GROUP, prep, 0)

    for d in range(2):
        if has_state:
            stacked = s0_ref[0, d].reshape(2 * RW_K, RW_K)
            state_scr[d] = jnp.where(same_head, _dot01_r(stacked, dup), 0.0).T
        else:
            state_scr[d] = jnp.zeros((128, 128), F32)

    def scan(i, carry):
        for d in range(2):
            c = (nc - 1 - i) if d else i
            rows = pl.ds(pl.multiple_of(c * L, L), L)
            srows = pl.ds(pl.multiple_of(c * 2 * L, 2 * L), 2 * L)
            h_st = state_scr[d]
            res = _dot3(jnp.concatenate([m2_scr[d, srows, :], ah_scr[d, srows, :]], axis=0), h_st)
            y2 = res[0:2 * L, :] + c2_scr[d, srows, :]
            yacc_scr[d, rows, :] = y2[0:L, :] + y2[L:, :]
            state_scr[d] = dg_scr[d, srows, :] * h_st + res[2 * L:, :] + bc_scr[d, srows, :]
        return carry

    lax.fori_loop(0, nc, scan, 0)
    if not has_state:
        dup_t = ((_iota((128, RW_K), 0) & (RW_K - 1)) == _iota((128, RW_K), 1)).astype(F32)
        for d in range(2):
            st_ref[0, d] = _dot01_r(state_scr[d].T, dup_t).reshape(2, RW_K, RW_K)

    y = yacc_scr[0] + yacc_scr[1]
    mean = head_sum(y) * (1.0 / RW_K)
    yc = y - mean
    var = head_sum(yc * yc) * (1.0 / RW_K)
    yn = yc * lax.rsqrt(var + RWKV_LN_EPS) * lng_ref[...] + lnb_ref[...]
    bonus = head_sum(r * (0.5 * kd_sum) * rk_ref[...]) * v
    o_ref[...] = (yn + bonus) * gate


def _rwkv(proj, nseq, t, prm, state):
    has_state = state is not None
    mu, w0, wup, a0, aup, gup, k_k, k_a, r_k, ln_g, ln_b = prm
    seq = lambda base: (lambda b, hp: (b, base // 128 + hp))
    fix = lambda base: (lambda b, hp: (b, base // 128))
    mu_at = lambda base: (lambda b, hp: (0, (base - OD_R) // 128 + hp))
    mu_fix = lambda base: (lambda b, hp: (0, (base - OD_R) // 128))
    col = lambda b, hp: (0, hp)
    in_specs = [
        pl.BlockSpec((t, 128), seq(OD_R)), pl.BlockSpec((t, 128), seq(OD_RK)), pl.BlockSpec((t, 128), seq(OD_RV)),
        pl.BlockSpec((t, 128), fix(OD_WD)), pl.BlockSpec((t, 128), fix(OD_AD)), pl.BlockSpec((t, 128), fix(OD_GD)),
        pl.BlockSpec((1, 128), mu_at(OD_R)), pl.BlockSpec((1, 128), mu_at(OD_RK)), pl.BlockSpec((1, 128), mu_at(OD_RV)),
        pl.BlockSpec((1, 128), mu_fix(OD_WD)), pl.BlockSpec((1, 128), mu_fix(OD_AD)), pl.BlockSpec((1, 128), mu_fix(OD_GD)),
        pl.BlockSpec((2, 128), col), pl.BlockSpec((128, 128), col),
        pl.BlockSpec((2, 128), col), pl.BlockSpec((128, 128), col),
        pl.BlockSpec((128, 128), col),
        pl.BlockSpec((1, 128), col), pl.BlockSpec((1, 128), col), pl.BlockSpec((1, 128), col),
        pl.BlockSpec((1, 128), col), pl.BlockSpec((1, 128), col),
    ]
    args = [proj] * 6 + [mu] * 6 + [w0, wup, a0, aup, gup, k_k, k_a, r_k, ln_g, ln_b]
    st_block = (1, 2, 2, RW_K, RW_K)
    st_map = lambda b, hp: (b, 0, hp, 0, 0)
    out_specs = [pl.BlockSpec((t, 128), lambda b, hp: (b, hp))]
    out_shape = [jax.ShapeDtypeStruct((nseq * t, RW_HEADS * RW_K), F32)]
    if has_state:
        in_specs.append(pl.BlockSpec(st_block, st_map))
        args.append(state)
    else:
        out_specs.append(pl.BlockSpec(st_block, st_map))
        out_shape.append(jax.ShapeDtypeStruct((nseq, 2, RW_HEADS, RW_K, RW_K), F32))
    scratch = [pltpu.VMEM((t + 2 * CONV_PAD, 128), F32), pltpu.VMEM((t, 128), F32), pltpu.VMEM((t, 128), F32),
               pltpu.VMEM((t, 128), F32), pltpu.VMEM((2, t, 128), F32), pltpu.VMEM((2, t, 128), F32),
               pltpu.VMEM((2, t, 128), F32)] + [pltpu.VMEM((2, 2 * t, 128), F32)] * 5 + [
               pltpu.VMEM((2, t, 128), F32), pltpu.VMEM((2, 128, 128), F32)]
    return pl.pallas_call(
        functools.partial(_rwkv_kernel, t=t, has_state=has_state),
        grid=(nseq, RW_HEADS // 2), in_specs=in_specs, out_specs=out_specs, out_shape=out_shape,
        scratch_shapes=scratch, compiler_params=_cparams(("arbitrary", "arbitrary")), name="rwkv",
    )(*args)


def _even_w_in(w):
    za, xs, bm, cm = w[:, 0:1024], w[:, 1024:2048], w[:, 2048:2304], w[:, 2304:2560]
    dt, q, kk, v = w[:, 2560:2592], w[:, 2592:3616], w[:, 3616:4640], w[:, 4640:5664]
    zb, al, be = w[:, 5664:6688], w[:, 6688:6704], w[:, 6704:6720]
    pad = jnp.zeros((w.shape[0], EV_COLS - EV_SM - 64), w.dtype)
    return jnp.concatenate([za, xs, zb, q, kk, v, bm, cm, dt, al, be, pad], axis=1)


def _pad_row(parts):
    flat = jnp.concatenate([p.reshape(-1) for p in parts])
    return jnp.pad(flat, (0, 128 - flat.shape[0])).reshape(1, 128)


def kernel(x_prompt, x_sample, state_ssd, state_gdn, cache_k, cache_v, state_rwkv, c, c_ctx, ada_w, ada_b, norm1_g, norm2_g, ffn_w1, ffn_w2, final_g, ev_w_in, ev_w_out, ssd_conv_w, ssd_conv_b, ssd_a_log, ssd_dt_bias, ssd_d, ssd_norm_g, gdn_conv_w, gdn_a_log, gdn_dt_bias, gdn_norm_g, od_w_in, od_w_out, attn_sink, rwkv_mu, rwkv_w0, rwkv_w_up, rwkv_a0, rwkv_a_up, rwkv_g_up, rwkv_k_k, rwkv_k_a, rwkv_r_k, rwkv_ln_g, rwkv_ln_b):
    bp, tp, _ = x_prompt.shape
    bs, ts, _ = x_sample.shape
    streams = [(x_prompt.reshape(bp * tp, D), bp, tp, None), (x_sample.reshape(bs * ts, D), bs, ts, ts)]

    cvec = jnp.concatenate([c_ctx[None, :], c, jnp.zeros((16 - 1 - bs, D), F32)], axis=0)
    mod = _adaln(cvec, ada_w, ada_b)

    w_in0 = _even_w_in(ev_w_in[0]).astype(BF16)
    w_out0 = ev_w_out[0].astype(BF16)
    bias_row = _pad_row([ssd_dt_bias[0], gdn_dt_bias[0]])
    alog_row = _pad_row([ssd_a_log[0], gdn_a_log[0]])
    dskip_row = jnp.repeat(ssd_d[0], SSD_P).reshape(1, SSD_HEADS * SSD_P)
    w1_0, w2_0 = ffn_w1[0].astype(BF16), ffn_w2[0].astype(BF16)
    w_in1 = od_w_in[0].astype(BF16)
    w_out1 = od_w_out[0].astype(BF16)
    w1_1, w2_1 = ffn_w1[1].astype(BF16), ffn_w2[1].astype(BF16)
    rw_prm = (rwkv_mu[0].reshape(1, -1), rwkv_w0[0], rwkv_w_up[0].reshape(2 * 64, -1), rwkv_a0[0],
              rwkv_a_up[0].reshape(2 * 64, -1), rwkv_g_up[0], rwkv_k_k[0].reshape(1, -1), rwkv_k_a[0].reshape(1, -1),
              rwkv_r_k[0].reshape(1, -1), rwkv_ln_g[0].reshape(1, -1), rwkv_ln_b[0].reshape(1, -1))
    cos2, sin2 = _rope_tables(ts)
    past = cache_k.shape[2]
    ck = cache_k[:, 0].reshape(bs * past, ATT_KVH * ATT_DH)
    cv = cache_v[:, 0].reshape(bs * past, ATT_KVH * ATT_DH)

    outs = []
    for x, nseq, t, sample_t in streams:
        latent = sample_t is not None
        m0 = mod[0]
        proj = _inproj(x, norm1_g[0:1], m0, w_in0, sample_t)
        s_ssd = state_ssd[:, 0] if latent else None
        s_gdn = state_gdn[:, 0] if latent else None
        r_ssd = _ssd(proj, nseq, t, ssd_conv_w[0], ssd_conv_b[0].reshape(1, -1), bias_row, alog_row, dskip_row,
                     ssd_norm_g[0].reshape(1, -1), s_ssd)
        r_gdn = _gdn(proj, nseq, t, gdn_conv_w[0], bias_row, alog_row, gdn_norm_g[0].reshape(1, -1), s_gdn)
        x = _outproj(r_ssd[0], r_gdn[0], x, m0, w_out0, sample_t)
        x = _ffn(x, norm2_g[0:1], m0, w1_0, w2_0, final_g.reshape(1, D), sample_t, final=False)
        m1 = mod[1]
        proj = _inproj(x, norm1_g[1:2], m1, w_in1, sample_t)
        if latent:
            att = _lat_attn(proj, nseq, t, attn_sink[0], ck, cv, cos2, sin2)
            r_rw = _rwkv(proj, nseq, t, rw_prm, state_rwkv[:, 0])
        else:
            att = _ctx_attn(proj, nseq, t, attn_sink[0])
            r_rw = _rwkv(proj, nseq, t, rw_prm, None)
        x = _outproj(att, r_rw[0], x, m1, w_out1, sample_t)
        x = _ffn(x, norm2_g[1:2], m1, w1_1, w2_1, final_g.reshape(1, D), sample_t, final=True)
        outs.append((x.reshape(nseq, t, D), r_ssd, r_gdn, proj, r_rw))

    (y_prompt, p_ssd, p_gdn, p_proj, p_rw), (y_sample, _, _, _, _) = outs
    new_k = p_proj[:, OD_K:OD_V].reshape(bp, 1, tp, ATT_KVH, ATT_DH)
    new_v = p_proj[:, OD_V:OD_R].reshape(bp, 1, tp, ATT_KVH, ATT_DH)
    return (y_prompt, y_sample, p_ssd[1][:, None], p_gdn[1][:, None], new_k, new_v, p_rw[1][:, None])
```

```python
import functools

import numpy as np
import jax
import jax.numpy as jnp
from jax import lax
from jax.experimental import pallas as pl
from jax.experimental.pallas import tpu as pltpu

F32 = jnp.float32
BF16 = jnp.bfloat16
HI = lax.Precision.HIGHEST

D = 1024
NORM_EPS = 1e-6
L = 64
GDN_GROUP, RW_GROUP = 8, 4
CONV_W = 5
CONV_PAD = 8

SSD_HEADS, SSD_P, SSD_N, SSD_GROUPS = 16, 64, 128, 2
SSD_GW = SSD_HEADS // SSD_GROUPS * SSD_P
GDN_HEADS, GDN_K = 8, 128
ATT_HEADS, ATT_KVH, ATT_DH, ATT_GROUP = 8, 2, 128, 4
WINDOW, QBLOCK, GRID_W, ROPE_BASE = 128, 128, 64, 10000.0
RW_HEADS, RW_K = 16, 64
RWKV_LN_EPS = 64e-5
FFN_H = 2816
FFN_TN = 256

EV_ZA, EV_XS, EV_ZB, EV_Q, EV_K, EV_V, EV_B, EV_C, EV_SM, EV_COLS = 0, 1024, 2048, 3072, 4096, 5120, 6144, 6400, 6656, 6784
SM_DT, SM_ALPHA, SM_BETA = 0, 32, 48
OD_Q, OD_K, OD_V, OD_R, OD_RK, OD_RV, OD_WD, OD_AD, OD_GD, OD_COLS = 0, 1024, 1280, 1536, 2560, 3584, 4608, 4736, 4864, 4992

VMEM_LIMIT = 56 * 1024 * 1024


def _cparams(sem):
    return pltpu.CompilerParams(dimension_semantics=sem, vmem_limit_bytes=VMEM_LIMIT)


def _dot(a, b, prec=None):
    return jnp.dot(a, b, preferred_element_type=F32, precision=prec)


def _dot_nt(a, b, prec=None):
    return lax.dot_general(a, b, (((1,), (1,)), ((), ())), preferred_element_type=F32, precision=prec)


def _dot_tn(a, b, prec=None):
    return lax.dot_general(a, b, (((0,), (0,)), ((), ())), preferred_element_type=F32, precision=prec)


def _bdot(a, b):
    return _dot(a.astype(BF16), b.astype(BF16))


def _bdot_nt(a, b):
    return _dot_nt(a.astype(BF16), b.astype(BF16))


def _bdot_tn(a, b):
    return _dot_tn(a.astype(BF16), b.astype(BF16))


def _silu(x):
    return x * jax.nn.sigmoid(x)


def _softplus(x):
    return jnp.maximum(x, 0.0) + jnp.log(1.0 + jnp.exp(-jnp.abs(x)))


def _iota(shape, dim):
    return lax.broadcasted_iota(jnp.int32, shape, dim)


def _tri_masks(rev):
    r, c = _iota((L, L), 0), _iota((L, L), 1)
    if rev:
        return c >= r, c > r
    return c <= r, c < r


def _inv_unit(a):
    eye = (_iota(a.shape, 0) == _iota(a.shape, 1)).astype(F32)
    x = eye - a
    p = _dot(a, a, HI)
    n = 2
    while True:
        x = x + _dot(x, p, HI)
        n *= 2
        if n >= L:
            break
        p = _dot(p, p, HI)
    return x


def _split3(x):
    hi = x.astype(BF16)
    r1 = x - hi.astype(F32)
    mid = r1.astype(BF16)
    lo = (r1 - mid.astype(F32)).astype(BF16)
    return hi, mid, lo


def _dot01_l(m01, x):
    mb = m01.astype(BF16)
    return _dot(jnp.concatenate([mb, mb, mb], axis=1), jnp.concatenate(_split3(x), axis=0))


def _dot01_r(x, m01):
    mb = m01.astype(BF16)
    return _dot(jnp.concatenate(_split3(x), axis=1), jnp.concatenate([mb, mb, mb], axis=0))


def _dot01_tn(x, m01):
    mb = m01.astype(BF16)
    return _dot_tn(jnp.concatenate(_split3(x), axis=0), jnp.concatenate([mb, mb, mb], axis=0))


def _dot3(a, b):
    ah = a.astype(BF16)
    al = (a - ah.astype(F32)).astype(BF16)
    bh = b.astype(BF16)
    bl = (b - bh.astype(F32)).astype(BF16)
    return _dot(ah, bh) + _dot(ah, bl) + _dot(al, bh)


def _to_bd(m):
    n = m.shape[1] // L
    blk = jnp.right_shift(_iota(m.shape, 1), L.bit_length() - 1)
    return jnp.concatenate([jnp.where(blk == j, m, 0.0) for j in range(n)], axis=0)


def _split2(x):
    hi = x.astype(BF16)
    return hi, (x - hi.astype(F32)).astype(BF16)


def _inv_unit_packed_multi(mats):
    eye = ((_iota(mats[0].shape, 1) & (L - 1)) == _iota(mats[0].shape, 0)).astype(F32)

    def mm3(lh, ll, ph, pl_):
        bh, bl = _to_bd(ph), _to_bd(pl_)
        return _dot(lh, bh) + _dot(lh, bl) + _dot(ll, bh)

    xs = [eye - a for a in mats]
    ps = [mm3(h, lo, h, lo) for h, lo in map(_split2, mats)]
    n = 2
    while True:
        sp = [_split2(p) for p in ps]
        sx = [_split2(x) for x in xs]
        n *= 2
        if n >= L:
            return [x + mm3(xh, xl, ph, pl_) for x, (xh, xl), (ph, pl_) in zip(xs, sx, sp)]
        res = [mm3(jnp.concatenate([xh, ph], axis=0), jnp.concatenate([xl, pl_], axis=0), ph, pl_)
               for (xh, xl), (ph, pl_) in zip(sx, sp)]
        xs = [x + r[0:L, :] for x, r in zip(xs, res)]
        ps = [r[L:, :] for r in res]


def _conv_silu(x, w_ref, b, pad_ref, t):
    c = x.shape[1]
    zeros = jnp.zeros((CONV_PAD, c), F32)
    pad_ref[0:CONV_PAD, 0:c] = zeros
    pad_ref[CONV_PAD + t:2 * CONV_PAD + t, 0:c] = zeros
    pad_ref[CONV_PAD:CONV_PAD + t, 0:c] = x
    acc = None
    for j in range(CONV_W):
        off = CONV_PAD - CONV_W // 2 + j
        term = pad_ref[off:off + t, 0:c] * w_ref[j:j + 1, :]
        acc = term if acc is None else acc + term
    if b is not None:
        acc = acc + b
    return _silu(acc)


def _adaln_kernel(c_ref, w_ref, b_ref, o_ref):
    c = c_ref[...]
    o_ref[0] = _bdot(_silu(c), w_ref[0]) + b_ref[0]


def _adaln(cvec, ada_w, ada_b):
    depth, _, n = ada_w.shape
    tn = 1536
    out = pl.pallas_call(
        _adaln_kernel,
        grid=(depth, n // tn),
        in_specs=[pl.BlockSpec((16, D), lambda l, j: (0, 0)),
                  pl.BlockSpec((1, D, tn), lambda l, j: (l, 0, j)),
                  pl.BlockSpec((1, 1, tn), lambda l, j: (l, 0, j))],
        out_specs=pl.BlockSpec((1, 16, tn), lambda l, j: (l, 0, j)),
        out_shape=jax.ShapeDtypeStruct((depth, 16, n), F32),
        compiler_params=_cparams(("arbitrary", "arbitrary")),
        name="adaln",
    )(cvec, ada_w, ada_b.reshape(depth, 1, n))
    return out.reshape(depth, 16, 6, D)


def _mod_row_map(tm, t):
    if t is None:
        return lambda i: (0, 0, 0)
    per = t // tm
    return lambda i: (1 + i // per, 0, 0)


def _norm_mod(x, g, m, shift_idx, scale_idx):
    var = jnp.mean(x * x, axis=-1, keepdims=True)
    y = x * lax.rsqrt(var + NORM_EPS) * g
    return y * (1.0 + m[scale_idx:scale_idx + 1]) + m[shift_idx:shift_idx + 1]


def _inproj_kernel(x_ref, g_ref, mod_ref, w_ref, o_ref, *, tn):
    h = _norm_mod(x_ref[...], g_ref[...], mod_ref[0], 0, 1).astype(BF16)
    n = w_ref.shape[1]
    for j in range(0, n, tn):
        w = min(tn, n - j)
        o_ref[:, j:j + w] = _dot(h, w_ref[:, j:j + w])


def _inproj(x, g, mod, w, sample_t, tm=256, tn=512):
    rows, n = x.shape[0], w.shape[1]
    return pl.pallas_call(
        functools.partial(_inproj_kernel, tn=tn),
        grid=(rows // tm,),
        in_specs=[pl.BlockSpec((tm, D), lambda i: (i, 0)),
                  pl.BlockSpec((1, D), lambda i: (0, 0)),
                  pl.BlockSpec((1, 6, D), _mod_row_map(tm, sample_t)),
                  pl.BlockSpec((D, n), lambda i: (0, 0))],
        out_specs=pl.BlockSpec((tm, n), lambda i: (i, 0)),
        out_shape=jax.ShapeDtypeStruct((rows, n), F32),
        compiler_params=_cparams(("arbitrary",)),
        name="inproj",
    )(x, g, mod, w)


def _outproj_kernel(a1_ref, a2_ref, x_ref, mod_ref, w_ref, o_ref):
    k1 = a1_ref.shape[1]
    acc = _dot(a1_ref[...].astype(BF16), w_ref[0:k1, :])
    acc = acc + _dot(a2_ref[...].astype(BF16), w_ref[k1:, :])
    o_ref[...] = x_ref[...] + mod_ref[0][2:3] * acc


def _outproj(a1, a2, x, mod, w, sample_t, tm=256):
    rows = x.shape[0]
    return pl.pallas_call(
        _outproj_kernel,
        grid=(rows // tm,),
        in_specs=[pl.BlockSpec((tm, a1.shape[1]), lambda i: (i, 0)),
                  pl.BlockSpec((tm, a2.shape[1]), lambda i: (i, 0)),
                  pl.BlockSpec((tm, D), lambda i: (i, 0)),
                  pl.BlockSpec((1, 6, D), _mod_row_map(tm, sample_t)),
                  pl.BlockSpec(w.shape, lambda i: (0, 0))],
        out_specs=pl.BlockSpec((tm, D), lambda i: (i, 0)),
        out_shape=jax.ShapeDtypeStruct((rows, D), F32),
        compiler_params=_cparams(("arbitrary",)),
        name="outproj",
    )(a1, a2, x, mod, w)


def _ffn_kernel(x_ref, g_ref, mod_ref, w1_ref, w2_ref, fg_ref, o_ref, *, final):
    x = x_ref[...]
    m = mod_ref[0]
    h = _norm_mod(x, g_ref[...], m, 3, 4).astype(BF16)
    acc = jnp.zeros(x.shape, F32)
    for j in range(0, FFN_H, FFN_TN):
        gate = _dot(h, w1_ref[:, j:j + FFN_TN])
        up = _dot(h, w1_ref[:, FFN_H + j:FFN_H + j + FFN_TN])
        acc = acc + _dot((_silu(gate) * up).astype(BF16), w2_ref[j:j + FFN_TN, :])
    y = x + m[5:6] * acc
    if final:
        var = jnp.mean(y * y, axis=-1, keepdims=True)
        y = y * lax.rsqrt(var + NORM_EPS) * fg_ref[...]
    o_ref[...] = y


def _ffn(x, g, mod, w1, w2, final_g, sample_t, final, tm=256):
    rows = x.shape[0]
    return pl.pallas_call(
        functools.partial(_ffn_kernel, final=final),
        grid=(rows // tm,),
        in_specs=[pl.BlockSpec((tm, D), lambda i: (i, 0)),
                  pl.BlockSpec((1, D), lambda i: (0, 0)),
                  pl.BlockSpec((1, 6, D), _mod_row_map(tm, sample_t)),
                  pl.BlockSpec(w1.shape, lambda i: (0, 0)),
                  pl.BlockSpec(w2.shape, lambda i: (0, 0)),
                  pl.BlockSpec((1, D), lambda i: (0, 0))],
        out_specs=pl.BlockSpec((tm, D), lambda i: (i, 0)),
        out_shape=jax.ShapeDtypeStruct((rows, D), F32),
        compiler_params=_cparams(("arbitrary",)),
        name="ffn",
    )(x, g, mod, w1, w2, final_g)


def _ssd_kernel(*refs, t, has_state):
    (za_ref, xs_ref, bm_ref, cm_ref, sm_ref, cwx_ref, cwb_ref, cwc_ref, cbx_ref, cbb_ref, cbc_ref,
     bias_ref, alog_ref, dsk_ref, ng_ref) = refs[:15]
    k = 15
    s0_ref = st_ref = None
    if has_state:
        s0_ref = refs[k]
        k += 1
    y_ref = refs[k]
    k += 1
    if not has_state:
        st_ref = refs[k]
        k += 1
    pad_scr, xs_scr, bb_scr, cc_scr, dt_scr, da_scr, dasm_scr, yacc_scr, state_scr = refs[k:]

    g = pl.program_id(1)
    nc = t // L
    npair = SSD_GW // 128

    xs_scr[...] = _conv_silu(xs_ref[...], cwx_ref, cbx_ref[...], pad_scr, t)
    bb_scr[...] = _conv_silu(bm_ref[...], cwb_ref, cbb_ref[...], pad_scr, t)
    cc_scr[...] = _conv_silu(cm_ref[...], cwc_ref, cbc_ref[...], pad_scr, t)

    dtv = _softplus(sm_ref[...] + bias_ref[...])
    dav = dtv * (-jnp.exp(alog_ref[...]))
    heads_per_group = SSD_HEADS // SSD_GROUPS
    for d in range(2):
        first = SM_DT + d * SSD_HEADS + g * heads_per_group
        src = _iota((128, SSD_GW), 0)
        e = (src == first + jnp.right_shift(_iota((128, SSD_GW), 1), 6)).astype(F32)
        dt_scr[d] = _dot01_r(dtv, e)
        da_scr[d] = _dot01_r(dav, e)
        sel = (_iota((128, 128), 0) == first + _iota((128, 128), 1)) & (_iota((128, 128), 1) < heads_per_group)
        dasm_scr[d] = _dot01_r(dav, sel.astype(F32))

    lane128 = _iota((L, 128), 1)
    for d in range(2):
        if has_state:
            state_scr[d] = s0_ref[0, d].reshape(SSD_GW, SSD_N).T
        else:
            state_scr[d] = jnp.zeros((SSD_N, SSD_GW), F32)

    def body(ci, carry):
        for d in range(2):
            rev = d == 1
            incl = _tri_masks(rev)[0]
            tri = incl.astype(F32)
            tri_t = _tri_masks(not rev)[0].astype(F32)
            last = 0 if rev else L - 1
            c = (nc - 1 - ci) if rev else ci
            r0 = pl.multiple_of(c * L, L)
            rows = pl.ds(r0, L)
            x = xs_scr[rows, :]
            bc = bb_scr[rows, :]
            cc = cc_scr[rows, :]
            acum = _dot01_l(tri, da_scr[d, rows, :])
            acum_t = _dot01_tn(dasm_scr[d, rows, :], tri_t)
            xdt = x * dt_scr[d, rows, :]
            cb = _bdot_nt(cc, bc)
            total = acum[last:last + 1, :]
            s_in = state_scr[d]
            y_inter = _bdot(cc, s_in) * jnp.exp(acum)
            pieces = []
            for j in range(npair):
                xp = xdt[:, j * 128:(j + 1) * 128]
                yp = None
                for hh in range(2):
                    h = 2 * j + hh
                    col = acum[:, h * SSD_P:h * SSD_P + 1]
                    row = acum_t[h:h + 1, :]
                    dec = jnp.where(incl, jnp.exp(jnp.minimum(col - row, 0.0)), 0.0)
                    mine = (lane128 >= SSD_P) if hh else (lane128 < SSD_P)
                    term = _bdot(cb * dec, jnp.where(mine, xp, 0.0))
                    yp = term if yp is None else yp + term
                pieces.append(yp)
            yacc_scr[d, rows, :] = jnp.concatenate(pieces, axis=1) + y_inter
            wx = jnp.exp(total - acum) * xdt
            state_scr[d] = s_in * jnp.exp(total) + _bdot_tn(bc, wx)
        return carry

    lax.fori_loop(0, nc, body, 0)
    if not has_state:
        for d in range(2):
            st_ref[0, d] = state_scr[d].T.reshape(heads_per_group, SSD_P, SSD_N)

    y = yacc_scr[0] + yacc_scr[1] + dsk_ref[...] * xs_scr[...]
    y = y * _silu(za_ref[...])
    var = jnp.mean(y * y, axis=-1, keepdims=True)
    y_ref[...] = y * lax.rsqrt(var + NORM_EPS) * ng_ref[...]


def _ssd(proj, nseq, t, conv_w, conv_b, bias_row, alog_row, dskip_row, norm_g, state):
    has_state = state is not None
    seq = lambda blk: (lambda b, g: (b, blk(g)))
    const = lambda blk: (lambda b, g: (0, blk(g)))
    in_specs = [
        pl.BlockSpec((t, SSD_GW), seq(lambda g: EV_ZA // SSD_GW + g)),
        pl.BlockSpec((t, SSD_GW), seq(lambda g: EV_XS // SSD_GW + g)),
        pl.BlockSpec((t, 128), seq(lambda g: EV_B // 128 + g)),
        pl.BlockSpec((t, 128), seq(lambda g: EV_C // 128 + g)),
        pl.BlockSpec((t, 128), seq(lambda g: EV_SM // 128)),
        pl.BlockSpec((CONV_W, SSD_GW), const(lambda g: g)),
        pl.BlockSpec((CONV_W, 128), const(lambda g: 1024 // 128 + g)),
        pl.BlockSpec((CONV_W, 128), const(lambda g: 1280 // 128 + g)),
        pl.BlockSpec((1, SSD_GW), const(lambda g: g)),
        pl.BlockSpec((1, 128), const(lambda g: 1024 // 128 + g)),
        pl.BlockSpec((1, 128), const(lambda g: 1280 // 128 + g)),
        pl.BlockSpec((1, 128), const(lambda g: 0)),
        pl.BlockSpec((1, 128), const(lambda g: 0)),
        pl.BlockSpec((1, SSD_GW), const(lambda g: g)),
        pl.BlockSpec((1, SSD_GW), const(lambda g: g)),
    ]
    args = [proj, proj, proj, proj, proj, conv_w, conv_w, conv_w, conv_b, conv_b, conv_b,
            bias_row, alog_row, dskip_row, norm_g]
    st_block = (1, 2, SSD_HEADS // SSD_GROUPS, SSD_P, SSD_N)
    st_map = lambda b, g: (b, 0, g, 0, 0)
    out_specs = [pl.BlockSpec((t, SSD_GW), lambda b, g: (b, g))]
    out_shape = [jax.ShapeDtypeStruct((nseq * t, SSD_HEADS * SSD_P), F32)]
    if has_state:
        in_specs.append(pl.BlockSpec(st_block, st_map))
        args.append(state)
    else:
        out_specs.append(pl.BlockSpec(st_block, st_map))
        out_shape.append(jax.ShapeDtypeStruct((nseq, 2, SSD_HEADS, SSD_P, SSD_N), F32))
    scratch = [pltpu.VMEM((t + 2 * CONV_PAD, SSD_GW), F32), pltpu.VMEM((t, SSD_GW), F32),
               pltpu.VMEM((t, 128), F32), pltpu.VMEM((t, 128), F32),
               pltpu.VMEM((2, t, SSD_GW), F32), pltpu.VMEM((2, t, SSD_GW), F32), pltpu.VMEM((2, t, 128), F32),
               pltpu.VMEM((2, t, SSD_GW), F32), pltpu.VMEM((2, SSD_N, SSD_GW), F32)]
    return pl.pallas_call(
        functools.partial(_ssd_kernel, t=t, has_state=has_state),
        grid=(nseq, SSD_GROUPS), in_specs=in_specs, out_specs=out_specs, out_shape=out_shape,
        scratch_shapes=scratch, compiler_params=_cparams(("arbitrary", "arbitrary")), name="ssd",
    )(*args)


def _gdn_kernel(*refs, t, has_state):
    (q_ref, k_ref, v_ref, zb_ref, sm_ref, cwq_ref, cwk_ref, cwv_ref, bias_ref, alog_ref, ng_ref) = refs[:11]
    k = 11
    s0_ref = st_ref = None
    if has_state:
        s0_ref = refs[k]
        k += 1
    o_ref = refs[k]
    k += 1
    if not has_state:
        st_ref = refs[k]
        k += 1
    (pad_scr, q_scr, k_scr, v_scr, g_scr, b_scr, ma_scr, c2_scr, bc_scr, dg_scr,
     oacc_scr, state_scr) = refs[k:]

    h = pl.program_id(1)
    nc = t // L

    def l2n(x):
        return x * lax.rsqrt(jnp.sum(x * x, axis=-1, keepdims=True) + NORM_EPS)

    q_scr[...] = l2n(_conv_silu(q_ref[...], cwq_ref, None, pad_scr, t)) * (GDN_K ** -0.5)
    k_scr[...] = l2n(_conv_silu(k_ref[...], cwk_ref, None, pad_scr, t))
    v_scr[...] = _conv_silu(v_ref[...], cwv_ref, None, pad_scr, t)

    sm = sm_ref[...]
    gl = -jnp.exp(alog_ref[...]) * _softplus(sm + bias_ref[...])
    bt = jax.nn.sigmoid(sm)
    lane = _iota((t, 128), 1)
    for d in range(2):
        gcol = jnp.sum(jnp.where(lane == SM_ALPHA + d * GDN_HEADS + h, gl, 0.0), axis=1, keepdims=True)
        bcol = jnp.sum(jnp.where(lane == SM_BETA + d * GDN_HEADS + h, bt, 0.0), axis=1, keepdims=True)
        g_scr[d] = jnp.broadcast_to(gcol, (t, 128))
        b_scr[d] = jnp.broadcast_to(bcol, (t, 128))

    masks = [_tri_masks(False), _tri_masks(True)]
    grp = min(GDN_GROUP, nc)

    def prep(gi, carry):
        cs = [gi * grp + j for j in range(grp)]
        rows = [pl.ds(pl.multiple_of(c * L, L), L) for c in cs]
        qkv = [(q_scr[r, :], k_scr[r, :], v_scr[r, :]) for r in rows]
        qkk = [_bdot_nt(jnp.concatenate([qc, kc], axis=0), kc) for qc, kc, _ in qkv]
        gbs = [[(g_scr[d, r, :], b_scr[d, r, :]) for d in range(2)] for r in rows]
        gam = [[_dot01_l(masks[d][0].astype(F32), gbs[j][d][0]) for d in range(2)] for j in range(grp)]
        gam_row = [[_dot01_tn(gbs[j][d][0], masks[1 - d][0].astype(F32))[0:1, :] for d in range(2)]
                   for j in range(grp)]
        per = []
        for j in range(grp):
            qc, kc, vc = qkv[j]
            row = []
            for d in range(2):
                incl, strict = masks[d]
                bb = gbs[j][d][1]
                g = gam[j][d]
                dec = jnp.where(incl, jnp.exp(jnp.minimum(g[:, 0:1] - gam_row[j][d], 0.0)), 0.0)
                a_mat = jnp.where(strict, bb[:, 0:1] * qkk[j][L:, :] * dec, 0.0)
                eg = jnp.exp(g)
                last = 0 if d else L - 1
                g_last = g[last:last + 1, :]
                rhs = jnp.concatenate([bb * vc, bb * kc * eg], axis=1)
                row.append((a_mat, rhs, qkk[j][0:L, :] * dec, qc * eg, kc * jnp.exp(g_last - g), g_last))
            per.append(row)
        packs = [jnp.concatenate([per[j][0][0], per[j][1][0], per[j + 1][0][0], per[j + 1][1][0]], axis=1)
                 for j in range(0, grp, 2)]
        inv = _inv_unit_packed_multi(packs)
        t_inv = [_to_bd(inv[j // 2][:, (j % 2) * 2 * L:(j % 2 + 1) * 2 * L]) for j in range(grp)]
        uw = [_dot3(t_inv[j], jnp.concatenate([per[j][0][1], per[j][1][1]], axis=0))
              for j in range(grp)]
        qk_uw = [_bdot(_to_bd(jnp.concatenate([per[j][0][2], per[j][1][2]], axis=1)), uw[j])
                 for j in range(grp)]
        ke_uw = [[_bdot_tn(per[j][d][4], uw[j][d * L:(d + 1) * L, :]) for d in range(2)]
                 for j in range(grp)]
        for j, c in enumerate(cs):
            srows = pl.ds(pl.multiple_of(c * GDN_K, GDN_K), GDN_K)
            drows = pl.ds(pl.multiple_of(c * 8, 8), 8)
            for d in range(2):
                half = slice(d * L, (d + 1) * L)
                mrows = pl.ds(pl.multiple_of(c * (L + GDN_K), L + GDN_K), L + GDN_K)
                ma_scr[d, mrows, :] = jnp.concatenate(
                    [per[j][d][3] - qk_uw[j][half, GDN_K:], -ke_uw[j][d][:, GDN_K:]], axis=0).astype(BF16)
                c2_scr[d, rows[j], :] = qk_uw[j][half, 0:GDN_K]
                bc_scr[d, srows, :] = ke_uw[j][d][:, 0:GDN_K]
                dg_scr[d, drows, :] = jnp.broadcast_to(jnp.exp(per[j][d][5]), (8, 128))
        return carry

    lax.fori_loop(0, nc // grp, prep, 0)

    for d in range(2):
        state_scr[d] = s0_ref[0, d, 0] if has_state else jnp.zeros((GDN_K, GDN_K), F32)

    def scan(i, carry):
        for d in range(2):
            c = (nc - 1 - i) if d else i
            rows = pl.ds(pl.multiple_of(c * L, L), L)
            srows = pl.ds(pl.multiple_of(c * GDN_K, GDN_K), GDN_K)
            drows = pl.ds(pl.multiple_of(c * 8, 8), 8)
            s = state_scr[d]
            mrows = pl.ds(pl.multiple_of(c * (L + GDN_K), L + GDN_K), L + GDN_K)
            res = _dot(ma_scr[d, mrows, :], s.astype(BF16))
            oacc_scr[d, rows, :] = res[0:L, :] + c2_scr[d, rows, :]
            state_scr[d] = dg_scr[d, drows, :][0:1, :] * s + res[L:, :] + bc_scr[d, srows, :]
        return carry

    lax.fori_loop(0, nc, scan, 0)
    if not has_state:
        for d in range(2):
            st_ref[0, d, 0] = state_scr[d]

    o = oacc_scr[0] + oacc_scr[1]
    var = jnp.mean(o * o, axis=-1, keepdims=True)
    o_ref[...] = o * lax.rsqrt(var + NORM_EPS) * ng_ref[...] * _silu(zb_ref[...])


def _gdn(proj, nseq, t, conv_w, bias_row, alog_row, norm_g, state):
    has_state = state is not None
    seq = lambda base: (lambda b, h: (b, base // 128 + h))
    const = lambda base: (lambda b, h: (0, base // 128 + h))
    in_specs = [
        pl.BlockSpec((t, 128), seq(EV_Q)), pl.BlockSpec((t, 128), seq(EV_K)), pl.BlockSpec((t, 128), seq(EV_V)),
        pl.BlockSpec((t, 128), seq(EV_ZB)), pl.BlockSpec((t, 128), lambda b, h: (b, EV_SM // 128)),
        pl.BlockSpec((CONV_W, 128), const(0)), pl.BlockSpec((CONV_W, 128), const(1024)),
        pl.BlockSpec((CONV_W, 128), const(2048)),
        pl.BlockSpec((1, 128), lambda b, h: (0, 0)), pl.BlockSpec((1, 128), lambda b, h: (0, 0)),
        pl.BlockSpec((1, 128), lambda b, h: (0, 0)),
    ]
    args = [proj, proj, proj, proj, proj, conv_w, conv_w, conv_w, bias_row, alog_row, norm_g]
    st_block = (1, 2, 1, GDN_K, GDN_K)
    st_map = lambda b, h: (b, 0, h, 0, 0)
    out_specs = [pl.BlockSpec((t, 128), lambda b, h: (b, h))]
    out_shape = [jax.ShapeDtypeStruct((nseq * t, GDN_HEADS * GDN_K), F32)]
    if has_state:
        in_specs.append(pl.BlockSpec(st_block, st_map))
        args.append(state)
    else:
        out_specs.append(pl.BlockSpec(st_block, st_map))
        out_shape.append(jax.ShapeDtypeStruct((nseq, 2, GDN_HEADS, GDN_K, GDN_K), F32))
    nc = t // L
    scratch = [pltpu.VMEM((t + 2 * CONV_PAD, 128), F32), pltpu.VMEM((t, 128), F32), pltpu.VMEM((t, 128), F32),
               pltpu.VMEM((t, 128), F32), pltpu.VMEM((2, t, 128), F32), pltpu.VMEM((2, t, 128), F32),
               pltpu.VMEM((2, nc * (L + GDN_K), 128), BF16), pltpu.VMEM((2, t, 128), F32),
               pltpu.VMEM((2, nc * GDN_K, GDN_K), F32),
               pltpu.VMEM((2, nc * 8, 128), F32),
               pltpu.VMEM((2, t, 128), F32), pltpu.VMEM((2, GDN_K, GDN_K), F32)]
    return pl.pallas_call(
        functools.partial(_gdn_kernel, t=t, has_state=has_state),
        grid=(nseq, GDN_HEADS), in_specs=in_specs, out_specs=out_specs, out_shape=out_shape,
        scratch_shapes=scratch, compiler_params=_cparams(("arbitrary", "arbitrary")), name="gdn",
    )(*args)


def _softmax_sink_pv(scores, sink, values):
    m = sink
    for s in scores:
        m = jnp.maximum(m, jnp.max(s, axis=-1, keepdims=True))
    den = jnp.exp(sink - m)
    ps = []
    for s in scores:
        p = jnp.exp(s - m)
        den = den + jnp.sum(p, axis=-1, keepdims=True)
        ps.append(p)
    out = None
    for p, v in zip(ps, values):
        term = _bdot(p / den, v)
        out = term if out is None else out + term
    return out


def _ctx_attn_kernel(sink_ref, q_ref, k_ref, v_ref, o_ref):
    kvh = pl.program_id(1)
    scale = ATT_DH ** -0.5
    kk, vv = k_ref[...], v_ref[...]
    for gq in range(ATT_GROUP):
        q = q_ref[:, gq * ATT_DH:(gq + 1) * ATT_DH]
        s = _bdot_nt(q, kk) * scale
        sink = sink_ref[kvh * ATT_GROUP + gq]
        o_ref[:, gq * ATT_DH:(gq + 1) * ATT_DH] = _softmax_sink_pv([s], sink, [vv])


def _ctx_attn(proj, nseq, t, sink):
    gw = ATT_GROUP * ATT_DH
    return pl.pallas_call(
        _ctx_attn_kernel,
        grid=(nseq, ATT_KVH),
        in_specs=[pl.BlockSpec(memory_space=pltpu.SMEM),
                  pl.BlockSpec((t, gw), lambda b, kv: (b, OD_Q // gw + kv)),
                  pl.BlockSpec((t, ATT_DH), lambda b, kv: (b, OD_K // ATT_DH + kv)),
                  pl.BlockSpec((t, ATT_DH), lambda b, kv: (b, OD_V // ATT_DH + kv))],
        out_specs=pl.BlockSpec((t, gw), lambda b, kv: (b, kv)),
        out_shape=jax.ShapeDtypeStruct((nseq * t, ATT_HEADS * ATT_DH), F32),
        compiler_params=_cparams(("arbitrary", "arbitrary")), name="ctx_attn",
    )(sink, proj, proj, proj)


def _lat_attn_kernel(sink_ref, q_ref, k_ref, v_ref, kc_ref, vc_ref, cos_ref, sin_ref, o_ref, kp_scr, vp_scr, *, t):
    kvh = pl.program_id(1)
    scale = ATT_DH ** -0.5
    band = QBLOCK + 2 * WINDOW
    cos, sin = cos_ref[...], sin_ref[...]

    def rope(x):
        return x * cos + pltpu.roll(x, ATT_DH // 2, 1) * sin

    zeros = jnp.zeros((WINDOW, ATT_DH), F32)
    kp_scr[0:WINDOW, :] = zeros
    kp_scr[WINDOW + t:, :] = zeros
    kp_scr[WINDOW:WINDOW + t, :] = rope(k_ref[...])
    vp_scr[0:WINDOW, :] = zeros
    vp_scr[WINDOW + t:, :] = zeros
    vp_scr[WINDOW:WINDOW + t, :] = v_ref[...]
    kc, vc = kc_ref[...], vc_ref[...]

    def body(i, carry):
        start = pl.multiple_of(i * QBLOCK, QBLOCK)
        ki = kp_scr[pl.ds(start, band), :]
        vi = vp_scr[pl.ds(start, band), :]
        qpos = start + _iota((QBLOCK, band), 0)
        kpos = start - WINDOW + _iota((QBLOCK, band), 1)
        valid = (jnp.abs(qpos - kpos) <= WINDOW) & (kpos >= 0) & (kpos < t)
        cs = cos_ref[pl.ds(start, QBLOCK), :]
        sn = sin_ref[pl.ds(start, QBLOCK), :]
        for gq in range(ATT_GROUP):
            q = q_ref[pl.ds(start, QBLOCK), gq * ATT_DH:(gq + 1) * ATT_DH]
            q = q * cs + pltpu.roll(q, ATT_DH // 2, 1) * sn
            s_band = jnp.where(valid, _bdot_nt(q, ki) * scale, -jnp.inf)
            s_ctx = _bdot_nt(q, kc) * scale
            sink = sink_ref[kvh * ATT_GROUP + gq]
            o_ref[pl.ds(start, QBLOCK), gq * ATT_DH:(gq + 1) * ATT_DH] = _softmax_sink_pv(
                [s_band, s_ctx], sink, [vi, vc])
        return carry

    lax.fori_loop(0, t // QBLOCK, body, 0)


def _lat_attn(proj, nseq, t, sink, cache_k, cache_v, cos2, sin2):
    gw = ATT_GROUP * ATT_DH
    past = cache_k.shape[0] // nseq
    return pl.pallas_call(
        functools.partial(_lat_attn_kernel, t=t),
        grid=(nseq, ATT_KVH),
        in_specs=[pl.BlockSpec(memory_space=pltpu.SMEM),
                  pl.BlockSpec((t, gw), lambda b, kv: (b, OD_Q // gw + kv)),
                  pl.BlockSpec((t, ATT_DH), lambda b, kv: (b, OD_K // ATT_DH + kv)),
                  pl.BlockSpec((t, ATT_DH), lambda b, kv: (b, OD_V // ATT_DH + kv)),
                  pl.BlockSpec((past, ATT_DH), lambda b, kv: (b, kv)),
                  pl.BlockSpec((past, ATT_DH), lambda b, kv: (b, kv)),
                  pl.BlockSpec((t, ATT_DH), lambda b, kv: (0, 0)),
                  pl.BlockSpec((t, ATT_DH), lambda b, kv: (0, 0))],
        out_specs=pl.BlockSpec((t, gw), lambda b, kv: (b, kv)),
        out_shape=jax.ShapeDtypeStruct((nseq * t, ATT_HEADS * ATT_DH), F32),
        scratch_shapes=[pltpu.VMEM((t + 2 * WINDOW, ATT_DH), F32), pltpu.VMEM((t + 2 * WINDOW, ATT_DH), F32)],
        compiler_params=_cparams(("arbitrary", "arbitrary")), name="lat_attn",
    )(sink, proj, proj, proj, cache_k, cache_v, cos2, sin2)


def _rope_tables(t):
    rows = t // GRID_W
    row = jnp.repeat(jnp.arange(rows), GRID_W).astype(F32)
    col = jnp.tile(jnp.arange(GRID_W), rows).astype(F32)
    n_freq = ATT_DH // 4
    inv = ROPE_BASE ** (-jnp.arange(n_freq, dtype=F32) / n_freq)
    ang = jnp.concatenate([row[:, None] * inv, col[:, None] * inv], axis=-1)
    cos, sin = jnp.cos(ang), jnp.sin(ang)
    return jnp.concatenate([cos, cos], axis=-1), jnp.concatenate([-sin, sin], axis=-1)


def _rwkv_kernel(*refs, t, has_state):
    (r_ref, k_ref, v_ref, wd_ref, ad_ref, gd_ref, mur_ref, muk_ref, muv_ref, muw_ref, mua_ref, mug_ref,
     w0_ref, wup_ref, a0_ref, aup_ref, gup_ref, kk_ref, ka_ref, rk_ref, lng_ref, lnb_ref) = refs[:22]
    k = 22
    s0_ref = st_ref = None
    if has_state:
        s0_ref = refs[k]
        k += 1
    o_ref = refs[k]
    k += 1
    if not has_state:
        st_ref = refs[k]
        k += 1
    (pad_scr, r_scr, v_scr, aa_scr, wl_scr, kd_scr, bb_scr, mah_scr, mal_scr, c2_scr, bc_scr, dg_scr,
     yacc_scr, state_scr) = refs[k:]
    nc = t // L

    def shift_mix(x_ref, mu_ref):
        x = x_ref[...]
        zeros = jnp.zeros((CONV_PAD, 128), F32)
        pad_scr[0:CONV_PAD, :] = zeros
        pad_scr[CONV_PAD + t:, :] = zeros
        pad_scr[CONV_PAD:CONV_PAD + t, :] = x
        nb = 0.5 * (pad_scr[CONV_PAD - 1:CONV_PAD - 1 + t, :] + pad_scr[CONV_PAD + 1:CONV_PAD + 1 + t, :])
        return x + mu_ref[...] * (nb - x)

    r = shift_mix(r_ref, mur_ref)
    kx = shift_mix(k_ref, muk_ref)
    v = shift_mix(v_ref, muv_ref)
    wd = shift_mix(wd_ref, muw_ref)
    ad = shift_mix(ad_ref, mua_ref)
    gd = shift_mix(gd_ref, mug_ref)

    lane_sq = _iota((128, 128), 1)
    row_sq = _iota((128, 128), 0)
    same_head = ((lane_sq >= RW_K) == (row_sq >= RW_K))
    bd_ones = same_head.astype(F32)

    def head_sum(x):
        return _dot01_r(x, bd_ones)

    gate = _bdot(jax.nn.sigmoid(gd), gup_ref[...])
    kkv = kx * kk_ref[...]
    kkn = kkv * lax.rsqrt(head_sum(kkv * kkv) + NORM_EPS)
    tw = jnp.tanh(wd)
    rows_lo = _iota((128, 128), 0) < RW_K
    kd_sum = jnp.zeros((t, 128), F32)
    for d in range(2):
        half = rows_lo if d == 0 else jnp.logical_not(rows_lo)
        w = w0_ref[d:d + 1, :] + _bdot(tw, jnp.where(half, wup_ref[...], 0.0))
        wl_scr[d] = -jnp.exp(-_softplus(-w) - 0.5)
        a = jax.nn.sigmoid(a0_ref[d:d + 1, :] + _bdot(ad, jnp.where(half, aup_ref[...], 0.0)))
        kd = kx * (1.0 + (a - 1.0) * ka_ref[...])
        kd_scr[d] = kd
        kd_sum = kd_sum + kd
        bb_scr[d] = kkn * a
    r_scr[...] = r
    v_scr[...] = v
    aa_scr[...] = -kkn

    dup = ((_iota((RW_K, 128), 1) & (RW_K - 1)) == _iota((RW_K, 128), 0)).astype(F32)
    row_head1 = _iota((2 * L, 128), 0) >= L
    lane_head1 = _iota((2 * L, 128), 1) >= RW_K
    own_lanes = row_head1 == lane_head1
    sq_r, sq_c = _iota((2 * L, 2 * L), 0), _iota((2 * L, 2 * L), 1)
    same_blk = (sq_r >= L) == (sq_c >= L)
    pos_r, pos_c = sq_r & (L - 1), sq_c & (L - 1)
    ones_l = jnp.ones((L, 128), F32)
    bd_masks = [(same_blk & (pos_c <= pos_r), same_blk & (pos_c < pos_r)),
                (same_blk & (pos_c >= pos_r), same_blk & (pos_c > pos_r))]
    mm, mm_nt, mm_tn = _bdot, _bdot_nt, _bdot_tn
    grp = min(RW_GROUP, nc)

    def two(x):
        return jnp.where(own_lanes, jnp.concatenate([x, x], axis=0), 0.0)

    def prep(gi, carry):
        cs = [gi * grp + j for j in range(grp)]
        cds = [(j, d) for j in range(grp) for d in range(2)]
        rows = [pl.ds(pl.multiple_of(c * L, L), L) for c in cs]
        srows = [pl.ds(pl.multiple_of(c * 2 * L, 2 * L), 2 * L) for c in cs]
        vv2 = [two(v_scr[r, :]) for r in rows]
        wl = {(j, d): wl_scr[d, rows[j], :] for j, d in cds}
        cin = {jd: _dot01_l(_tri_masks(jd[1] == 1)[0].astype(F32), wl[jd]) for jd in cds}
        decay = {jd: jnp.exp(_dot01_tn(wl[jd], ones_l)) for jd in cds}
        ops = {}
        for j, d in cds:
            e_in, e_out = jnp.exp(cin[j, d]), jnp.exp(-cin[j, d])
            ops[j, d] = (two(aa_scr[rows[j], :] * jnp.exp(cin[j, d] - wl[j, d])), two(r_scr[rows[j], :] * e_in),
                         two(bb_scr[d, rows[j], :] * e_out), two(kd_scr[d, rows[j], :] * e_out))
        prod = {jd: mm_nt(jnp.concatenate(ops[jd][0:2], axis=0), jnp.concatenate(ops[jd][2:4], axis=0)) for jd in cds}
        blk = {}
        for j, d in cds:
            incl_bd, strict_bd = bd_masks[d]
            p = prod[j, d]
            blk[j, d] = (jnp.where(strict_bd, p[0:2 * L, 0:2 * L], 0.0), jnp.where(strict_bd, p[0:2 * L, 2 * L:], 0.0),
                         jnp.where(incl_bd, p[2 * L:, 0:2 * L], 0.0), jnp.where(incl_bd, p[2 * L:, 2 * L:], 0.0))
        packs = [-jnp.concatenate([blk[j, d][0][0:L, :] + blk[j, d][0][L:, :] for d in range(2)], axis=1)
                 for j in range(grp)]
        inv = _inv_unit_packed_multi(packs)
        t_inv = {(j, d): _to_bd(inv[j][:, d * 2 * L:(d + 1) * 2 * L]) for j, d in cds}
        kv = {(j, d): mm(jnp.concatenate(blk[j, d][1::2], axis=0), vv2[j]) for j, d in cds}
        m1c1 = {jd: _dot3(t_inv[jd], jnp.concatenate([ops[jd][0], kv[jd][0:2 * L, :]], axis=1)) for jd in cds}
        rm = {jd: mm(blk[jd][2], m1c1[jd]) for jd in cds}
        bm = {jd: mm_tn(ops[jd][2], m1c1[jd]) for jd in cds}
        kv2 = {(j, d): mm_tn(ops[j, d][3], vv2[j]) for j, d in cds}
        for j, d in cds:
            m2ah = jnp.concatenate([ops[j, d][1] + rm[j, d][:, 0:128], decay[j, d] * bm[j, d][:, 0:128]], axis=0)
            hi, lo = _split2(m2ah)
            mrows = pl.ds(pl.multiple_of(cs[j] * 4 * L, 4 * L), 4 * L)
            mah_scr[d, mrows, :] = hi
            mal_scr[d, mrows, :] = lo
            c2_scr[d, srows[j], :] = rm[j, d][:, 128:] + kv[j, d][2 * L:, :]
            bc_scr[d, srows[j], :] = decay[j, d] * (bm[j, d][:, 128:] + kv2[j, d])
            dg_scr[d, srows[j], :] = decay[j, d]
        return carry

    lax.fori_loop(0, nc // grp, prep, 0)

    for d in range(2):
        if has_state:
            stacked = s0_ref[0, d].reshape(2 * RW_K, RW_K)
            state_scr[d] = jnp.where(same_head, _dot01_r(stacked, dup), 0.0).T
        else:
            state_scr[d] = jnp.zeros((128, 128), F32)

    def scan(i, carry):
        for d in range(2):
            c = (nc - 1 - i) if d else i
            rows = pl.ds(pl.multiple_of(c * L, L), L)
            srows = pl.ds(pl.multiple_of(c * 2 * L, 2 * L), 2 * L)
            h_st = state_scr[d]
            mrows = pl.ds(pl.multiple_of(c * 4 * L, 4 * L), 4 * L)
            lh, ll = mah_scr[d, mrows, :], mal_scr[d, mrows, :]
            hh, hl = _split2(h_st)
            res = _dot(lh, hh) + _dot(lh, hl) + _dot(ll, hh)
            y2 = res[0:2 * L, :] + c2_scr[d, srows, :]
            yacc_scr[d, rows, :] = y2[0:L, :] + y2[L:, :]
            state_scr[d] = dg_scr[d, srows, :] * h_st + res[2 * L:, :] + bc_scr[d, srows, :]
        return carry

    lax.fori_loop(0, nc, scan, 0)
    if not has_state:
        dup_t = ((_iota((128, RW_K), 0) & (RW_K - 1)) == _iota((128, RW_K), 1)).astype(F32)
        for d in range(2):
            st_ref[0, d] = _dot01_r(state_scr[d].T, dup_t).reshape(2, RW_K, RW_K)

    y = yacc_scr[0] + yacc_scr[1]
    mean = head_sum(y) * (1.0 / RW_K)
    yc = y - mean
    var = head_sum(yc * yc) * (1.0 / RW_K)
    yn = yc * lax.rsqrt(var + RWKV_LN_EPS) * lng_ref[...] + lnb_ref[...]
    bonus = head_sum(r * (0.5 * kd_sum) * rk_ref[...]) * v
    o_ref[...] = (yn + bonus) * gate


def _rwkv(proj, nseq, t, prm, state):
    has_state = state is not None
    mu, w0, wup, a0, aup, gup, k_k, k_a, r_k, ln_g, ln_b = prm
    seq = lambda base: (lambda b, hp: (b, base // 128 + hp))
    fix = lambda base: (lambda b, hp: (b, base // 128))
    mu_at = lambda base: (lambda b, hp: (0, (base - OD_R) // 128 + hp))
    mu_fix = lambda base: (lambda b, hp: (0, (base - OD_R) // 128))
    col = lambda b, hp: (0, hp)
    in_specs = [
        pl.BlockSpec((t, 128), seq(OD_R)), pl.BlockSpec((t, 128), seq(OD_RK)), pl.BlockSpec((t, 128), seq(OD_RV)),
        pl.BlockSpec((t, 128), fix(OD_WD)), pl.BlockSpec((t, 128), fix(OD_AD)), pl.BlockSpec((t, 128), fix(OD_GD)),
        pl.BlockSpec((1, 128), mu_at(OD_R)), pl.BlockSpec((1, 128), mu_at(OD_RK)), pl.BlockSpec((1, 128), mu_at(OD_RV)),
        pl.BlockSpec((1, 128), mu_fix(OD_WD)), pl.BlockSpec((1, 128), mu_fix(OD_AD)), pl.BlockSpec((1, 128), mu_fix(OD_GD)),
        pl.BlockSpec((2, 128), col), pl.BlockSpec((128, 128), col),
        pl.BlockSpec((2, 128), col), pl.BlockSpec((128, 128), col),
        pl.BlockSpec((128, 128), col),
        pl.BlockSpec((1, 128), col), pl.BlockSpec((1, 128), col), pl.BlockSpec((1, 128), col),
        pl.BlockSpec((1, 128), col), pl.BlockSpec((1, 128), col),
    ]
    args = [proj] * 6 + [mu] * 6 + [w0, wup, a0, aup, gup, k_k, k_a, r_k, ln_g, ln_b]
    st_block = (1, 2, 2, RW_K, RW_K)
    st_map = lambda b, hp: (b, 0, hp, 0, 0)
    out_specs = [pl.BlockSpec((t, 128), lambda b, hp: (b, hp))]
    out_shape = [jax.ShapeDtypeStruct((nseq * t, RW_HEADS * RW_K), F32)]
    if has_state:
        in_specs.append(pl.BlockSpec(st_block, st_map))
        args.append(state)
    else:
        out_specs.append(pl.BlockSpec(st_block, st_map))
        out_shape.append(jax.ShapeDtypeStruct((nseq, 2, RW_HEADS, RW_K, RW_K), F32))
    scratch = [pltpu.VMEM((t + 2 * CONV_PAD, 128), F32), pltpu.VMEM((t, 128), F32), pltpu.VMEM((t, 128), F32),
               pltpu.VMEM((t, 128), F32), pltpu.VMEM((2, t, 128), F32), pltpu.VMEM((2, t, 128), F32),
               pltpu.VMEM((2, t, 128), F32)] + [pltpu.VMEM((2, 4 * t, 128), BF16)] * 2 + [
               pltpu.VMEM((2, 2 * t, 128), F32)] * 3 + [
               pltpu.VMEM((2, t, 128), F32), pltpu.VMEM((2, 128, 128), F32)]
    return pl.pallas_call(
        functools.partial(_rwkv_kernel, t=t, has_state=has_state),
        grid=(nseq, RW_HEADS // 2), in_specs=in_specs, out_specs=out_specs, out_shape=out_shape,
        scratch_shapes=scratch, compiler_params=_cparams(("arbitrary", "arbitrary")), name="rwkv",
    )(*args)


def _even_w_in(w):
    za, xs, bm, cm = w[:, 0:1024], w[:, 1024:2048], w[:, 2048:2304], w[:, 2304:2560]
    dt, q, kk, v = w[:, 2560:2592], w[:, 2592:3616], w[:, 3616:4640], w[:, 4640:5664]
    zb, al, be = w[:, 5664:6688], w[:, 6688:6704], w[:, 6704:6720]
    pad = jnp.zeros((w.shape[0], EV_COLS - EV_SM - 64), w.dtype)
    return jnp.concatenate([za, xs, zb, q, kk, v, bm, cm, dt, al, be, pad], axis=1)


def _pad_row(parts):
    flat = jnp.concatenate([p.reshape(-1) for p in parts])
    return jnp.pad(flat, (0, 128 - flat.shape[0])).reshape(1, 128)


def kernel(x_prompt, x_sample, state_ssd, state_gdn, cache_k, cache_v, state_rwkv, c, c_ctx, ada_w, ada_b, norm1_g, norm2_g, ffn_w1, ffn_w2, final_g, ev_w_in, ev_w_out, ssd_conv_w, ssd_conv_b, ssd_a_log, ssd_dt_bias, ssd_d, ssd_norm_g, gdn_conv_w, gdn_a_log, gdn_dt_bias, gdn_norm_g, od_w_in, od_w_out, attn_sink, rwkv_mu, rwkv_w0, rwkv_w_up, rwkv_a0, rwkv_a_up, rwkv_g_up, rwkv_k_k, rwkv_k_a, rwkv_r_k, rwkv_ln_g, rwkv_ln_b):
    bp, tp, _ = x_prompt.shape
    bs, ts, _ = x_sample.shape
    streams = [(x_prompt.reshape(bp * tp, D), bp, tp, None), (x_sample.reshape(bs * ts, D), bs, ts, ts)]

    cvec = jnp.concatenate([c_ctx[None, :], c, jnp.zeros((16 - 1 - bs, D), F32)], axis=0)
    mod = _adaln(cvec, ada_w, ada_b)

    w_in0 = _even_w_in(ev_w_in[0]).astype(BF16)
    w_out0 = ev_w_out[0].astype(BF16)
    bias_row = _pad_row([ssd_dt_bias[0], gdn_dt_bias[0]])
    alog_row = _pad_row([ssd_a_log[0], gdn_a_log[0]])
    dskip_row = jnp.repeat(ssd_d[0], SSD_P).reshape(1, SSD_HEADS * SSD_P)
    w1_0, w2_0 = ffn_w1[0].astype(BF16), ffn_w2[0].astype(BF16)
    w_in1 = od_w_in[0].astype(BF16)
    w_out1 = od_w_out[0].astype(BF16)
    w1_1, w2_1 = ffn_w1[1].astype(BF16), ffn_w2[1].astype(BF16)
    rw_prm = (rwkv_mu[0].reshape(1, -1), rwkv_w0[0], rwkv_w_up[0].reshape(2 * 64, -1), rwkv_a0[0],
              rwkv_a_up[0].reshape(2 * 64, -1), rwkv_g_up[0], rwkv_k_k[0].reshape(1, -1), rwkv_k_a[0].reshape(1, -1),
              rwkv_r_k[0].reshape(1, -1), rwkv_ln_g[0].reshape(1, -1), rwkv_ln_b[0].reshape(1, -1))
    cos2, sin2 = _rope_tables(ts)
    past = cache_k.shape[2]
    ck = cache_k[:, 0].reshape(bs * past, ATT_KVH * ATT_DH)
    cv = cache_v[:, 0].reshape(bs * past, ATT_KVH * ATT_DH)

    outs = []
    for x, nseq, t, sample_t in streams:
        latent = sample_t is not None
        m0 = mod[0]
        proj = _inproj(x, norm1_g[0:1], m0, w_in0, sample_t)
        s_ssd = state_ssd[:, 0] if latent else None
        s_gdn = state_gdn[:, 0] if latent else None
        r_ssd = _ssd(proj, nseq, t, ssd_conv_w[0], ssd_conv_b[0].reshape(1, -1), bias_row, alog_row, dskip_row,
                     ssd_norm_g[0].reshape(1, -1), s_ssd)
        r_gdn = _gdn(proj, nseq, t, gdn_conv_w[0], bias_row, alog_row, gdn_norm_g[0].reshape(1, -1), s_gdn)
        x = _outproj(r_ssd[0], r_gdn[0], x, m0, w_out0, sample_t)
        x = _ffn(x, norm2_g[0:1], m0, w1_0, w2_0, final_g.reshape(1, D), sample_t, final=False)
        m1 = mod[1]
        proj = _inproj(x, norm1_g[1:2], m1, w_in1, sample_t)
        if latent:
            att = _lat_attn(proj, nseq, t, attn_sink[0], ck, cv, cos2, sin2)
            r_rw = _rwkv(proj, nseq, t, rw_prm, state_rwkv[:, 0])
        else:
            att = _ctx_attn(proj, nseq, t, attn_sink[0])
            r_rw = _rwkv(proj, nseq, t, rw_prm, None)
        x = _outproj(att, r_rw[0], x, m1, w_out1, sample_t)
        x = _ffn(x, norm2_g[1:2], m1, w1_1, w2_1, final_g.reshape(1, D), sample_t, final=True)
        outs.append((x.reshape(nseq, t, D), r_ssd, r_gdn, proj, r_rw))

    (y_prompt, p_ssd, p_gdn, p_proj, p_rw), (y_sample, _, _, _, _) = outs
    new_k = p_proj[:, OD_K:OD_V].reshape(bp, 1, tp, ATT_KVH, ATT_DH)
    new_v = p_proj[:, OD_V:OD_R].reshape(bp, 1, tp, ATT_KVH, ATT_DH)
    return (y_prompt, y_sample, p_ssd[1][:, None], p_gdn[1][:, None], new_k, new_v, p_rw[1][:, None])
```

```python
import functools

import numpy as np
import jax
import jax.numpy as jnp
from jax import lax
from jax.experimental import pallas as pl
from jax.experimental.pallas import tpu as pltpu

F32 = jnp.float32
BF16 = jnp.bfloat16
HI = lax.Precision.HIGHEST

D = 1024
NORM_EPS = 1e-6
L = 64
GDN_GROUP, RW_GROUP = 8, 4
GDN_PROMPT_NSUB = 2
CONV_W = 5
CONV_PAD = 8

SSD_HEADS, SSD_P, SSD_N, SSD_GROUPS = 16, 64, 128, 2
SSD_GW = SSD_HEADS // SSD_GROUPS * SSD_P
GDN_HEADS, GDN_K = 8, 128
ATT_HEADS, ATT_KVH, ATT_DH, ATT_GROUP = 8, 2, 128, 4
WINDOW, QBLOCK, GRID_W, ROPE_BASE = 128, 128, 64, 10000.0
RW_HEADS, RW_K = 16, 64
RWKV_LN_EPS = 64e-5
FFN_H = 2816
FFN_TN = 256

EV_ZA, EV_XS, EV_ZB, EV_Q, EV_K, EV_V, EV_B, EV_C, EV_SM, EV_COLS = 0, 1024, 2048, 3072, 4096, 5120, 6144, 6400, 6656, 6784
SM_DT, SM_ALPHA, SM_BETA = 0, 32, 48
OD_Q, OD_K, OD_V, OD_R, OD_RK, OD_RV, OD_WD, OD_AD, OD_GD, OD_COLS = 0, 1024, 1280, 1536, 2560, 3584, 4608, 4736, 4864, 4992

VMEM_LIMIT = 56 * 1024 * 1024


def _cparams(sem):
    return pltpu.CompilerParams(dimension_semantics=sem, vmem_limit_bytes=VMEM_LIMIT)


def _dot(a, b, prec=None):
    return jnp.dot(a, b, preferred_element_type=F32, precision=prec)


def _dot_nt(a, b, prec=None):
    return lax.dot_general(a, b, (((1,), (1,)), ((), ())), preferred_element_type=F32, precision=prec)


def _dot_tn(a, b, prec=None):
    return lax.dot_general(a, b, (((0,), (0,)), ((), ())), preferred_element_type=F32, precision=prec)


def _bdot(a, b):
    return _dot(a.astype(BF16), b.astype(BF16))


def _bdot_nt(a, b):
    return _dot_nt(a.astype(BF16), b.astype(BF16))


def _bdot_tn(a, b):
    return _dot_tn(a.astype(BF16), b.astype(BF16))


def _silu(x):
    return x * jax.nn.sigmoid(x)


def _softplus(x):
    return jnp.maximum(x, 0.0) + jnp.log(1.0 + jnp.exp(-jnp.abs(x)))


def _iota(shape, dim):
    return lax.broadcasted_iota(jnp.int32, shape, dim)


def _tri_masks(rev):
    r, c = _iota((L, L), 0), _iota((L, L), 1)
    if rev:
        return c >= r, c > r
    return c <= r, c < r


def _inv_unit(a):
    eye = (_iota(a.shape, 0) == _iota(a.shape, 1)).astype(F32)
    x = eye - a
    p = _dot(a, a, HI)
    n = 2
    while True:
        x = x + _dot(x, p, HI)
        n *= 2
        if n >= L:
            break
        p = _dot(p, p, HI)
    return x


def _split3(x):
    hi = x.astype(BF16)
    r1 = x - hi.astype(F32)
    mid = r1.astype(BF16)
    lo = (r1 - mid.astype(F32)).astype(BF16)
    return hi, mid, lo


def _dot01_l(m01, x):
    mb = m01.astype(BF16)
    return _dot(jnp.concatenate([mb, mb, mb], axis=1), jnp.concatenate(_split3(x), axis=0))


def _dot01_r(x, m01):
    mb = m01.astype(BF16)
    return _dot(jnp.concatenate(_split3(x), axis=1), jnp.concatenate([mb, mb, mb], axis=0))


def _dot01_tn(x, m01):
    mb = m01.astype(BF16)
    return _dot_tn(jnp.concatenate(_split3(x), axis=0), jnp.concatenate([mb, mb, mb], axis=0))


def _dot3(a, b):
    ah = a.astype(BF16)
    al = (a - ah.astype(F32)).astype(BF16)
    bh = b.astype(BF16)
    bl = (b - bh.astype(F32)).astype(BF16)
    return _dot(ah, bh) + _dot(ah, bl) + _dot(al, bh)


def _to_bd(m):
    n = m.shape[1] // L
    blk = jnp.right_shift(_iota(m.shape, 1), L.bit_length() - 1)
    return jnp.concatenate([jnp.where(blk == j, m, 0.0) for j in range(n)], axis=0)


def _split2(x):
    hi = x.astype(BF16)
    return hi, (x - hi.astype(F32)).astype(BF16)


def _inv_unit_packed_multi(mats):
    eye = ((_iota(mats[0].shape, 1) & (L - 1)) == _iota(mats[0].shape, 0)).astype(F32)

    def mm3(lh, ll, ph, pl_):
        bh, bl = _to_bd(ph), _to_bd(pl_)
        return _dot(lh, bh) + _dot(lh, bl) + _dot(ll, bh)

    xs = [eye - a for a in mats]
    ps = [mm3(h, lo, h, lo) for h, lo in map(_split2, mats)]
    n = 2
    while True:
        sp = [_split2(p) for p in ps]
        sx = [_split2(x) for x in xs]
        n *= 2
        if n >= L:
            return [x + mm3(xh, xl, ph, pl_) for x, (xh, xl), (ph, pl_) in zip(xs, sx, sp)]
        res = [mm3(jnp.concatenate([xh, ph], axis=0), jnp.concatenate([xl, pl_], axis=0), ph, pl_)
               for (xh, xl), (ph, pl_) in zip(sx, sp)]
        xs = [x + r[0:L, :] for x, r in zip(xs, res)]
        ps = [r[L:, :] for r in res]


def _conv_silu(x, w_ref, b, pad_ref, t):
    c = x.shape[1]
    zeros = jnp.zeros((CONV_PAD, c), F32)
    pad_ref[0:CONV_PAD, 0:c] = zeros
    pad_ref[CONV_PAD + t:2 * CONV_PAD + t, 0:c] = zeros
    pad_ref[CONV_PAD:CONV_PAD + t, 0:c] = x
    acc = None
    for j in range(CONV_W):
        off = CONV_PAD - CONV_W // 2 + j
        term = pad_ref[off:off + t, 0:c] * w_ref[j:j + 1, :]
        acc = term if acc is None else acc + term
    if b is not None:
        acc = acc + b
    return _silu(acc)


def _adaln_kernel(c_ref, w_ref, b_ref, o_ref):
    c = c_ref[...]
    o_ref[0] = _bdot(_silu(c), w_ref[0]) + b_ref[0]


def _adaln(cvec, ada_w, ada_b):
    depth, _, n = ada_w.shape
    tn = 1536
    out = pl.pallas_call(
        _adaln_kernel,
        grid=(depth, n // tn),
        in_specs=[pl.BlockSpec((16, D), lambda l, j: (0, 0)),
                  pl.BlockSpec((1, D, tn), lambda l, j: (l, 0, j)),
                  pl.BlockSpec((1, 1, tn), lambda l, j: (l, 0, j))],
        out_specs=pl.BlockSpec((1, 16, tn), lambda l, j: (l, 0, j)),
        out_shape=jax.ShapeDtypeStruct((depth, 16, n), F32),
        compiler_params=_cparams(("arbitrary", "arbitrary")),
        name="adaln",
    )(cvec, ada_w, ada_b.reshape(depth, 1, n))
    return out.reshape(depth, 16, 6, D)


def _mod_row_map(tm, t):
    if t is None:
        return lambda i: (0, 0, 0)
    per = t // tm
    return lambda i: (1 + i // per, 0, 0)


def _norm_mod(x, g, m, shift_idx, scale_idx):
    var = jnp.mean(x * x, axis=-1, keepdims=True)
    y = x * lax.rsqrt(var + NORM_EPS) * g
    return y * (1.0 + m[scale_idx:scale_idx + 1]) + m[shift_idx:shift_idx + 1]


def _inproj_kernel(x_ref, g_ref, mod_ref, w_ref, o_ref, *, tn):
    h = _norm_mod(x_ref[...], g_ref[...], mod_ref[0], 0, 1).astype(BF16)
    n = w_ref.shape[1]
    for j in range(0, n, tn):
        w = min(tn, n - j)
        o_ref[:, j:j + w] = _dot(h, w_ref[:, j:j + w])


def _inproj(x, g, mod, w, sample_t, tm=256, tn=512):
    rows, n = x.shape[0], w.shape[1]
    return pl.pallas_call(
        functools.partial(_inproj_kernel, tn=tn),
        grid=(rows // tm,),
        in_specs=[pl.BlockSpec((tm, D), lambda i: (i, 0)),
                  pl.BlockSpec((1, D), lambda i: (0, 0)),
                  pl.BlockSpec((1, 6, D), _mod_row_map(tm, sample_t)),
                  pl.BlockSpec((D, n), lambda i: (0, 0))],
        out_specs=pl.BlockSpec((tm, n), lambda i: (i, 0)),
        out_shape=jax.ShapeDtypeStruct((rows, n), F32),
        compiler_params=_cparams(("arbitrary",)),
        name="inproj",
    )(x, g, mod, w)


def _outproj_kernel(a1_ref, a2_ref, x_ref, mod_ref, w_ref, o_ref):
    k1 = a1_ref.shape[1]
    acc = _dot(a1_ref[...].astype(BF16), w_ref[0:k1, :])
    acc = acc + _dot(a2_ref[...].astype(BF16), w_ref[k1:, :])
    o_ref[...] = x_ref[...] + mod_ref[0][2:3] * acc


def _outproj(a1, a2, x, mod, w, sample_t, tm=256):
    rows = x.shape[0]
    return pl.pallas_call(
        _outproj_kernel,
        grid=(rows // tm,),
        in_specs=[pl.BlockSpec((tm, a1.shape[1]), lambda i: (i, 0)),
                  pl.BlockSpec((tm, a2.shape[1]), lambda i: (i, 0)),
                  pl.BlockSpec((tm, D), lambda i: (i, 0)),
                  pl.BlockSpec((1, 6, D), _mod_row_map(tm, sample_t)),
                  pl.BlockSpec(w.shape, lambda i: (0, 0))],
        out_specs=pl.BlockSpec((tm, D), lambda i: (i, 0)),
        out_shape=jax.ShapeDtypeStruct((rows, D), F32),
        compiler_params=_cparams(("arbitrary",)),
        name="outproj",
    )(a1, a2, x, mod, w)


def _ffn_kernel(x_ref, g_ref, mod_ref, w1_ref, w2_ref, fg_ref, o_ref, *, final):
    x = x_ref[...]
    m = mod_ref[0]
    h = _norm_mod(x, g_ref[...], m, 3, 4).astype(BF16)
    acc = jnp.zeros(x.shape, F32)
    for j in range(0, FFN_H, FFN_TN):
        gate = _dot(h, w1_ref[:, j:j + FFN_TN])
        up = _dot(h, w1_ref[:, FFN_H + j:FFN_H + j + FFN_TN])
        acc = acc + _dot((_silu(gate) * up).astype(BF16), w2_ref[j:j + FFN_TN, :])
    y = x + m[5:6] * acc
    if final:
        var = jnp.mean(y * y, axis=-1, keepdims=True)
        y = y * lax.rsqrt(var + NORM_EPS) * fg_ref[...]
    o_ref[...] = y


def _ffn(x, g, mod, w1, w2, final_g, sample_t, final, tm=256):
    rows = x.shape[0]
    return pl.pallas_call(
        functools.partial(_ffn_kernel, final=final),
        grid=(rows // tm,),
        in_specs=[pl.BlockSpec((tm, D), lambda i: (i, 0)),
                  pl.BlockSpec((1, D), lambda i: (0, 0)),
                  pl.BlockSpec((1, 6, D), _mod_row_map(tm, sample_t)),
                  pl.BlockSpec(w1.shape, lambda i: (0, 0)),
                  pl.BlockSpec(w2.shape, lambda i: (0, 0)),
                  pl.BlockSpec((1, D), lambda i: (0, 0))],
        out_specs=pl.BlockSpec((tm, D), lambda i: (i, 0)),
        out_shape=jax.ShapeDtypeStruct((rows, D), F32),
        compiler_params=_cparams(("arbitrary",)),
        name="ffn",
    )(x, g, mod, w1, w2, final_g)


def _ssd_kernel(*refs, t, has_state):
    (za_ref, xs_ref, bm_ref, cm_ref, sm_ref, cwx_ref, cwb_ref, cwc_ref, cbx_ref, cbb_ref, cbc_ref,
     bias_ref, alog_ref, dsk_ref, ng_ref) = refs[:15]
    k = 15
    s0_ref = st_ref = None
    if has_state:
        s0_ref = refs[k]
        k += 1
    y_ref = refs[k]
    k += 1
    if not has_state:
        st_ref = refs[k]
        k += 1
    pad_scr, xs_scr, bb_scr, cc_scr, dt_scr, da_scr, dasm_scr, yacc_scr, state_scr = refs[k:]

    g = pl.program_id(1)
    nc = t // L
    npair = SSD_GW // 128

    xs_scr[...] = _conv_silu(xs_ref[...], cwx_ref, cbx_ref[...], pad_scr, t)
    bb_scr[...] = _conv_silu(bm_ref[...], cwb_ref, cbb_ref[...], pad_scr, t)
    cc_scr[...] = _conv_silu(cm_ref[...], cwc_ref, cbc_ref[...], pad_scr, t)

    dtv = _softplus(sm_ref[...] + bias_ref[...])
    dav = dtv * (-jnp.exp(alog_ref[...]))
    heads_per_group = SSD_HEADS // SSD_GROUPS
    for d in range(2):
        first = SM_DT + d * SSD_HEADS + g * heads_per_group
        src = _iota((128, SSD_GW), 0)
        e = (src == first + jnp.right_shift(_iota((128, SSD_GW), 1), 6)).astype(F32)
        dt_scr[d] = _dot01_r(dtv, e)
        da_scr[d] = _dot01_r(dav, e)
        sel = (_iota((128, 128), 0) == first + _iota((128, 128), 1)) & (_iota((128, 128), 1) < heads_per_group)
        dasm_scr[d] = _dot01_r(dav, sel.astype(F32))

    lane128 = _iota((L, 128), 1)
    for d in range(2):
        if has_state:
            state_scr[d] = s0_ref[0, d].reshape(SSD_GW, SSD_N).T
        else:
            state_scr[d] = jnp.zeros((SSD_N, SSD_GW), F32)

    def body(ci, carry):
        for d in range(2):
            rev = d == 1
            incl = _tri_masks(rev)[0]
            tri = incl.astype(F32)
            tri_t = _tri_masks(not rev)[0].astype(F32)
            last = 0 if rev else L - 1
            c = (nc - 1 - ci) if rev else ci
            r0 = pl.multiple_of(c * L, L)
            rows = pl.ds(r0, L)
            x = xs_scr[rows, :]
            bc = bb_scr[rows, :]
            cc = cc_scr[rows, :]
            acum = _dot01_l(tri, da_scr[d, rows, :])
            acum_t = _dot01_tn(dasm_scr[d, rows, :], tri_t)
            xdt = x * dt_scr[d, rows, :]
            cb = _bdot_nt(cc, bc)
            total = acum[last:last + 1, :]
            s_in = state_scr[d]
            y_inter = _bdot(cc, s_in) * jnp.exp(acum)
            pieces = []
            for j in range(npair):
                xp = xdt[:, j * 128:(j + 1) * 128]
                yp = None
                for hh in range(2):
                    h = 2 * j + hh
                    col = acum[:, h * SSD_P:h * SSD_P + 1]
                    row = acum_t[h:h + 1, :]
                    dec = jnp.where(incl, jnp.exp(jnp.minimum(col - row, 0.0)), 0.0)
                    mine = (lane128 >= SSD_P) if hh else (lane128 < SSD_P)
                    term = _bdot(cb * dec, jnp.where(mine, xp, 0.0))
                    yp = term if yp is None else yp + term
                pieces.append(yp)
            yacc_scr[d, rows, :] = jnp.concatenate(pieces, axis=1) + y_inter
            wx = jnp.exp(total - acum) * xdt
            state_scr[d] = s_in * jnp.exp(total) + _bdot_tn(bc, wx)
        return carry

    lax.fori_loop(0, nc, body, 0)
    if not has_state:
        for d in range(2):
            st_ref[0, d] = state_scr[d].T.reshape(heads_per_group, SSD_P, SSD_N)

    y = yacc_scr[0] + yacc_scr[1] + dsk_ref[...] * xs_scr[...]
    y = y * _silu(za_ref[...])
    var = jnp.mean(y * y, axis=-1, keepdims=True)
    y_ref[...] = y * lax.rsqrt(var + NORM_EPS) * ng_ref[...]


def _ssd(proj, nseq, t, conv_w, conv_b, bias_row, alog_row, dskip_row, norm_g, state):
    has_state = state is not None
    seq = lambda blk: (lambda b, g: (b, blk(g)))
    const = lambda blk: (lambda b, g: (0, blk(g)))
    in_specs = [
        pl.BlockSpec((t, SSD_GW), seq(lambda g: EV_ZA // SSD_GW + g)),
        pl.BlockSpec((t, SSD_GW), seq(lambda g: EV_XS // SSD_GW + g)),
        pl.BlockSpec((t, 128), seq(lambda g: EV_B // 128 + g)),
        pl.BlockSpec((t, 128), seq(lambda g: EV_C // 128 + g)),
        pl.BlockSpec((t, 128), seq(lambda g: EV_SM // 128)),
        pl.BlockSpec((CONV_W, SSD_GW), const(lambda g: g)),
        pl.BlockSpec((CONV_W, 128), const(lambda g: 1024 // 128 + g)),
        pl.BlockSpec((CONV_W, 128), const(lambda g: 1280 // 128 + g)),
        pl.BlockSpec((1, SSD_GW), const(lambda g: g)),
        pl.BlockSpec((1, 128), const(lambda g: 1024 // 128 + g)),
        pl.BlockSpec((1, 128), const(lambda g: 1280 // 128 + g)),
        pl.BlockSpec((1, 128), const(lambda g: 0)),
        pl.BlockSpec((1, 128), const(lambda g: 0)),
        pl.BlockSpec((1, SSD_GW), const(lambda g: g)),
        pl.BlockSpec((1, SSD_GW), const(lambda g: g)),
    ]
    args = [proj, proj, proj, proj, proj, conv_w, conv_w, conv_w, conv_b, conv_b, conv_b,
            bias_row, alog_row, dskip_row, norm_g]
    st_block = (1, 2, SSD_HEADS // SSD_GROUPS, SSD_P, SSD_N)
    st_map = lambda b, g: (b, 0, g, 0, 0)
    out_specs = [pl.BlockSpec((t, SSD_GW), lambda b, g: (b, g))]
    out_shape = [jax.ShapeDtypeStruct((nseq * t, SSD_HEADS * SSD_P), F32)]
    if has_state:
        in_specs.append(pl.BlockSpec(st_block, st_map))
        args.append(state)
    else:
        out_specs.append(pl.BlockSpec(st_block, st_map))
        out_shape.append(jax.ShapeDtypeStruct((nseq, 2, SSD_HEADS, SSD_P, SSD_N), F32))
    scratch = [pltpu.VMEM((t + 2 * CONV_PAD, SSD_GW), F32), pltpu.VMEM((t, SSD_GW), F32),
               pltpu.VMEM((t, 128), F32), pltpu.VMEM((t, 128), F32),
               pltpu.VMEM((2, t, SSD_GW), F32), pltpu.VMEM((2, t, SSD_GW), F32), pltpu.VMEM((2, t, 128), F32),
               pltpu.VMEM((2, t, SSD_GW), F32), pltpu.VMEM((2, SSD_N, SSD_GW), F32)]
    return pl.pallas_call(
        functools.partial(_ssd_kernel, t=t, has_state=has_state),
        grid=(nseq, SSD_GROUPS), in_specs=in_specs, out_specs=out_specs, out_shape=out_shape,
        scratch_shapes=scratch, compiler_params=_cparams(("arbitrary", "arbitrary")), name="ssd",
    )(*args)


def _gdn_kernel(*refs, t, nsub, has_state):
    (q_ref, k_ref, v_ref, zb_ref, sm_ref, cwq_ref, cwk_ref, cwv_ref, bias_ref, alog_ref, ng_ref) = refs[:11]
    k = 11
    s0_ref = st_ref = None
    if has_state:
        s0_ref = refs[k]
        k += 1
    o_ref = refs[k]
    k += 1
    if not has_state:
        st_ref = refs[k]
        k += 1
    (pad_scr, q_scr, k_scr, v_scr, g_scr, b_scr, ma_scr, c2_scr, bc_scr, dg_scr,
     oacc_scr, state_scr) = refs[k:]

    h = pl.program_id(1)
    nc_seq = t // L
    nc = nsub * nc_seq
    rows_all = nsub * t

    def l2n(x):
        return x * lax.rsqrt(jnp.sum(x * x, axis=-1, keepdims=True) + NORM_EPS)

    for s in range(nsub):
        rs = slice(s * t, (s + 1) * t)
        q_scr[rs, :] = l2n(_conv_silu(q_ref[rs, :], cwq_ref, None, pad_scr, t)) * (GDN_K ** -0.5)
        k_scr[rs, :] = l2n(_conv_silu(k_ref[rs, :], cwk_ref, None, pad_scr, t))
        v_scr[rs, :] = _conv_silu(v_ref[rs, :], cwv_ref, None, pad_scr, t)

    sm = sm_ref[...]
    gl = -jnp.exp(alog_ref[...]) * _softplus(sm + bias_ref[...])
    bt = jax.nn.sigmoid(sm)
    lane = _iota((rows_all, 128), 1)
    for d in range(2):
        gcol = jnp.sum(jnp.where(lane == SM_ALPHA + d * GDN_HEADS + h, gl, 0.0), axis=1, keepdims=True)
        bcol = jnp.sum(jnp.where(lane == SM_BETA + d * GDN_HEADS + h, bt, 0.0), axis=1, keepdims=True)
        g_scr[d] = jnp.broadcast_to(gcol, (rows_all, 128))
        b_scr[d] = jnp.broadcast_to(bcol, (rows_all, 128))

    masks = [_tri_masks(False), _tri_masks(True)]
    grp = min(GDN_GROUP, nc)

    def prep(gi, carry):
        cs = [gi * grp + j for j in range(grp)]
        rows = [pl.ds(pl.multiple_of(c * L, L), L) for c in cs]
        qkv = [(q_scr[r, :], k_scr[r, :], v_scr[r, :]) for r in rows]
        qkk = [_bdot_nt(jnp.concatenate([qc, kc], axis=0), kc) for qc, kc, _ in qkv]
        gbs = [[(g_scr[d, r, :], b_scr[d, r, :]) for d in range(2)] for r in rows]
        gam = [[_dot01_l(masks[d][0].astype(F32), gbs[j][d][0]) for d in range(2)] for j in range(grp)]
        gam_row = [[_dot01_tn(gbs[j][d][0], masks[1 - d][0].astype(F32))[0:1, :] for d in range(2)]
                   for j in range(grp)]
        per = []
        for j in range(grp):
            qc, kc, vc = qkv[j]
            row = []
            for d in range(2):
                incl, strict = masks[d]
                bb = gbs[j][d][1]
                g = gam[j][d]
                dec = jnp.where(incl, jnp.exp(jnp.minimum(g[:, 0:1] - gam_row[j][d], 0.0)), 0.0)
                a_mat = jnp.where(strict, bb[:, 0:1] * qkk[j][L:, :] * dec, 0.0)
                eg = jnp.exp(g)
                last = 0 if d else L - 1
                g_last = g[last:last + 1, :]
                rhs = jnp.concatenate([bb * vc, bb * kc * eg], axis=1)
                row.append((a_mat, rhs, qkk[j][0:L, :] * dec, qc * eg, kc * jnp.exp(g_last - g), g_last))
            per.append(row)
        packs = [jnp.concatenate([per[j][0][0], per[j][1][0], per[j + 1][0][0], per[j + 1][1][0]], axis=1)
                 for j in range(0, grp, 2)]
        inv = _inv_unit_packed_multi(packs)
        t_inv = [_to_bd(inv[j // 2][:, (j % 2) * 2 * L:(j % 2 + 1) * 2 * L]) for j in range(grp)]
        uw = [_bdot(t_inv[j], jnp.concatenate([per[j][0][1], per[j][1][1]], axis=0))
              for j in range(grp)]
        qk_uw = [_bdot(_to_bd(jnp.concatenate([per[j][0][2], per[j][1][2]], axis=1)), uw[j])
                 for j in range(grp)]
        ke_uw = [[_bdot_tn(per[j][d][4], uw[j][d * L:(d + 1) * L, :]) for d in range(2)]
                 for j in range(grp)]
        for j, c in enumerate(cs):
            srows = pl.ds(pl.multiple_of(c * GDN_K, GDN_K), GDN_K)
            drows = pl.ds(pl.multiple_of(c * 8, 8), 8)
            for d in range(2):
                half = slice(d * L, (d + 1) * L)
                mrows = pl.ds(pl.multiple_of(c * (L + GDN_K), L + GDN_K), L + GDN_K)
                ma_scr[d, mrows, :] = jnp.concatenate(
                    [per[j][d][3] - qk_uw[j][half, GDN_K:], -ke_uw[j][d][:, GDN_K:]], axis=0).astype(BF16)
                c2_scr[d, rows[j], :] = qk_uw[j][half, 0:GDN_K]
                bc_scr[d, srows, :] = ke_uw[j][d][:, 0:GDN_K]
                dg_scr[d, drows, :] = jnp.broadcast_to(jnp.exp(per[j][d][5]), (8, 128))
        return carry

    lax.fori_loop(0, nc // grp, prep, 0)

    chains = [(s, d) for s in range(nsub) for d in range(2)]
    for s, d in chains:
        state_scr[2 * s + d] = s0_ref[s, d, 0] if has_state else jnp.zeros((GDN_K, GDN_K), F32)

    def scan(i, carry):
        for s, d in chains:
            c = s * nc_seq + ((nc_seq - 1 - i) if d else i)
            rows = pl.ds(pl.multiple_of(c * L, L), L)
            srows = pl.ds(pl.multiple_of(c * GDN_K, GDN_K), GDN_K)
            drows = pl.ds(pl.multiple_of(c * 8, 8), 8)
            st = state_scr[2 * s + d]
            mrows = pl.ds(pl.multiple_of(c * (L + GDN_K), L + GDN_K), L + GDN_K)
            res = _dot(ma_scr[d, mrows, :], st.astype(BF16))
            oacc_scr[d, rows, :] = res[0:L, :] + c2_scr[d, rows, :]
            state_scr[2 * s + d] = dg_scr[d, drows, :][0:1, :] * st + res[L:, :] + bc_scr[d, srows, :]
        return carry

    lax.fori_loop(0, nc_seq, scan, 0)
    if not has_state:
        for s, d in chains:
            st_ref[s, d, 0] = state_scr[2 * s + d]

    o = oacc_scr[0] + oacc_scr[1]
    var = jnp.mean(o * o, axis=-1, keepdims=True)
    o_ref[...] = o * lax.rsqrt(var + NORM_EPS) * ng_ref[...] * _silu(zb_ref[...])


def _gdn(proj, nseq, t, conv_w, bias_row, alog_row, norm_g, state, nsub=1):
    has_state = state is not None
    seq = lambda base: (lambda b, h: (b, base // 128 + h))
    const = lambda base: (lambda b, h: (0, base // 128 + h))
    rb = nsub * t
    in_specs = [
        pl.BlockSpec((rb, 128), seq(EV_Q)), pl.BlockSpec((rb, 128), seq(EV_K)), pl.BlockSpec((rb, 128), seq(EV_V)),
        pl.BlockSpec((rb, 128), seq(EV_ZB)), pl.BlockSpec((rb, 128), lambda b, h: (b, EV_SM // 128)),
        pl.BlockSpec((CONV_W, 128), const(0)), pl.BlockSpec((CONV_W, 128), const(1024)),
        pl.BlockSpec((CONV_W, 128), const(2048)),
        pl.BlockSpec((1, 128), lambda b, h: (0, 0)), pl.BlockSpec((1, 128), lambda b, h: (0, 0)),
        pl.BlockSpec((1, 128), lambda b, h: (0, 0)),
    ]
    args = [proj, proj, proj, proj, proj, conv_w, conv_w, conv_w, bias_row, alog_row, norm_g]
    st_block = (nsub, 2, 1, GDN_K, GDN_K)
    st_map = lambda b, h: (b, 0, h, 0, 0)
    out_specs = [pl.BlockSpec((rb, 128), lambda b, h: (b, h))]
    out_shape = [jax.ShapeDtypeStruct((nseq * t, GDN_HEADS * GDN_K), F32)]
    if has_state:
        in_specs.append(pl.BlockSpec(st_block, st_map))
        args.append(state)
    else:
        out_specs.append(pl.BlockSpec(st_block, st_map))
        out_shape.append(jax.ShapeDtypeStruct((nseq, 2, GDN_HEADS, GDN_K, GDN_K), F32))
    nc = rb // L
    scratch = [pltpu.VMEM((t + 2 * CONV_PAD, 128), F32), pltpu.VMEM((rb, 128), F32), pltpu.VMEM((rb, 128), F32),
               pltpu.VMEM((rb, 128), F32), pltpu.VMEM((2, rb, 128), F32), pltpu.VMEM((2, rb, 128), F32),
               pltpu.VMEM((2, nc * (L + GDN_K), 128), BF16), pltpu.VMEM((2, rb, 128), F32),
               pltpu.VMEM((2, nc * GDN_K, GDN_K), F32),
               pltpu.VMEM((2, nc * 8, 128), F32),
               pltpu.VMEM((2, rb, 128), F32), pltpu.VMEM((2 * nsub, GDN_K, GDN_K), F32)]
    return pl.pallas_call(
        functools.partial(_gdn_kernel, t=t, nsub=nsub, has_state=has_state),
        grid=(nseq // nsub, GDN_HEADS), in_specs=in_specs, out_specs=out_specs, out_shape=out_shape,
        scratch_shapes=scratch, compiler_params=_cparams(("arbitrary", "arbitrary")), name="gdn",
    )(*args)


def _softmax_sink_pv(scores, sink, values):
    m = sink
    for s in scores:
        m = jnp.maximum(m, jnp.max(s, axis=-1, keepdims=True))
    den = jnp.exp(sink - m)
    ps = []
    for s in scores:
        p = jnp.exp(s - m)
        den = den + jnp.sum(p, axis=-1, keepdims=True)
        ps.append(p)
    out = None
    for p, v in zip(ps, values):
        term = _bdot(p / den, v)
        out = term if out is None else out + term
    return out


def _ctx_attn_kernel(sink_ref, q_ref, k_ref, v_ref, o_ref):
    kvh = pl.program_id(1)
    scale = ATT_DH ** -0.5
    kk, vv = k_ref[...], v_ref[...]
    for gq in range(ATT_GROUP):
        q = q_ref[:, gq * ATT_DH:(gq + 1) * ATT_DH]
        s = _bdot_nt(q, kk) * scale
        sink = sink_ref[kvh * ATT_GROUP + gq]
        o_ref[:, gq * ATT_DH:(gq + 1) * ATT_DH] = _softmax_sink_pv([s], sink, [vv])


def _ctx_attn(proj, nseq, t, sink):
    gw = ATT_GROUP * ATT_DH
    return pl.pallas_call(
        _ctx_attn_kernel,
        grid=(nseq, ATT_KVH),
        in_specs=[pl.BlockSpec(memory_space=pltpu.SMEM),
                  pl.BlockSpec((t, gw), lambda b, kv: (b, OD_Q // gw + kv)),
                  pl.BlockSpec((t, ATT_DH), lambda b, kv: (b, OD_K // ATT_DH + kv)),
                  pl.BlockSpec((t, ATT_DH), lambda b, kv: (b, OD_V // ATT_DH + kv))],
        out_specs=pl.BlockSpec((t, gw), lambda b, kv: (b, kv)),
        out_shape=jax.ShapeDtypeStruct((nseq * t, ATT_HEADS * ATT_DH), F32),
        compiler_params=_cparams(("arbitrary", "arbitrary")), name="ctx_attn",
    )(sink, proj, proj, proj)


def _lat_attn_kernel(sink_ref, q_ref, k_ref, v_ref, kc_ref, vc_ref, cos_ref, sin_ref, o_ref, kp_scr, vp_scr, *, t):
    kvh = pl.program_id(1)
    scale = ATT_DH ** -0.5
    band = QBLOCK + 2 * WINDOW
    cos, sin = cos_ref[...], sin_ref[...]

    def rope(x):
        return x * cos + pltpu.roll(x, ATT_DH // 2, 1) * sin

    zeros = jnp.zeros((WINDOW, ATT_DH), F32)
    kp_scr[0:WINDOW, :] = zeros
    kp_scr[WINDOW + t:, :] = zeros
    kp_scr[WINDOW:WINDOW + t, :] = rope(k_ref[...])
    vp_scr[0:WINDOW, :] = zeros
    vp_scr[WINDOW + t:, :] = zeros
    vp_scr[WINDOW:WINDOW + t, :] = v_ref[...]
    kc, vc = kc_ref[...], vc_ref[...]

    def body(i, carry):
        start = pl.multiple_of(i * QBLOCK, QBLOCK)
        ki = kp_scr[pl.ds(start, band), :]
        vi = vp_scr[pl.ds(start, band), :]
        qpos = start + _iota((QBLOCK, band), 0)
        kpos = start - WINDOW + _iota((QBLOCK, band), 1)
        valid = (jnp.abs(qpos - kpos) <= WINDOW) & (kpos >= 0) & (kpos < t)
        cs = cos_ref[pl.ds(start, QBLOCK), :]
        sn = sin_ref[pl.ds(start, QBLOCK), :]
        for gq in range(ATT_GROUP):
            q = q_ref[pl.ds(start, QBLOCK), gq * ATT_DH:(gq + 1) * ATT_DH]
            q = q * cs + pltpu.roll(q, ATT_DH // 2, 1) * sn
            s_band = jnp.where(valid, _bdot_nt(q, ki) * scale, -jnp.inf)
            s_ctx = _bdot_nt(q, kc) * scale
            sink = sink_ref[kvh * ATT_GROUP + gq]
            o_ref[pl.ds(start, QBLOCK), gq * ATT_DH:(gq + 1) * ATT_DH] = _softmax_sink_pv(
                [s_band, s_ctx], sink, [vi, vc])
        return carry

    lax.fori_loop(0, t // QBLOCK, body, 0)


def _lat_attn(proj, nseq, t, sink, cache_k, cache_v, cos2, sin2):
    gw = ATT_GROUP * ATT_DH
    past = cache_k.shape[0] // nseq
    return pl.pallas_call(
        functools.partial(_lat_attn_kernel, t=t),
        grid=(nseq, ATT_KVH),
        in_specs=[pl.BlockSpec(memory_space=pltpu.SMEM),
                  pl.BlockSpec((t, gw), lambda b, kv: (b, OD_Q // gw + kv)),
                  pl.BlockSpec((t, ATT_DH), lambda b, kv: (b, OD_K // ATT_DH + kv)),
                  pl.BlockSpec((t, ATT_DH), lambda b, kv: (b, OD_V // ATT_DH + kv)),
                  pl.BlockSpec((past, ATT_DH), lambda b, kv: (b, kv)),
                  pl.BlockSpec((past, ATT_DH), lambda b, kv: (b, kv)),
                  pl.BlockSpec((t, ATT_DH), lambda b, kv: (0, 0)),
                  pl.BlockSpec((t, ATT_DH), lambda b, kv: (0, 0))],
        out_specs=pl.BlockSpec((t, gw), lambda b, kv: (b, kv)),
        out_shape=jax.ShapeDtypeStruct((nseq * t, ATT_HEADS * ATT_DH), F32),
        scratch_shapes=[pltpu.VMEM((t + 2 * WINDOW, ATT_DH), F32), pltpu.VMEM((t + 2 * WINDOW, ATT_DH), F32)],
        compiler_params=_cparams(("arbitrary", "arbitrary")), name="lat_attn",
    )(sink, proj, proj, proj, cache_k, cache_v, cos2, sin2)


def _rope_tables(t):
    rows = t // GRID_W
    row = jnp.repeat(jnp.arange(rows), GRID_W).astype(F32)
    col = jnp.tile(jnp.arange(GRID_W), rows).astype(F32)
    n_freq = ATT_DH // 4
    inv = ROPE_BASE ** (-jnp.arange(n_freq, dtype=F32) / n_freq)
    ang = jnp.concatenate([row[:, None] * inv, col[:, None] * inv], axis=-1)
    cos, sin = jnp.cos(ang), jnp.sin(ang)
    return jnp.concatenate([cos, cos], axis=-1), jnp.concatenate([-sin, sin], axis=-1)


def _rwkv_kernel(*refs, t, has_state):
    (r_ref, k_ref, v_ref, wd_ref, ad_ref, gd_ref, mur_ref, muk_ref, muv_ref, muw_ref, mua_ref, mug_ref,
     w0_ref, wup_ref, a0_ref, aup_ref, gup_ref, kk_ref, ka_ref, rk_ref, lng_ref, lnb_ref) = refs[:22]
    k = 22
    s0_ref = st_ref = None
    if has_state:
        s0_ref = refs[k]
        k += 1
    o_ref = refs[k]
    k += 1
    if not has_state:
        st_ref = refs[k]
        k += 1
    (pad_scr, r_scr, v_scr, aa_scr, wl_scr, kd_scr, bb_scr, mah_scr, mal_scr, c2_scr, bc_scr, dg_scr,
     yacc_scr, state_scr) = refs[k:]
    nc = t // L

    def shift_mix(x_ref, mu_ref):
        x = x_ref[...]
        zeros = jnp.zeros((CONV_PAD, 128), F32)
        pad_scr[0:CONV_PAD, :] = zeros
        pad_scr[CONV_PAD + t:, :] = zeros
        pad_scr[CONV_PAD:CONV_PAD + t, :] = x
        nb = 0.5 * (pad_scr[CONV_PAD - 1:CONV_PAD - 1 + t, :] + pad_scr[CONV_PAD + 1:CONV_PAD + 1 + t, :])
        return x + mu_ref[...] * (nb - x)

    r = shift_mix(r_ref, mur_ref)
    kx = shift_mix(k_ref, muk_ref)
    v = shift_mix(v_ref, muv_ref)
    wd = shift_mix(wd_ref, muw_ref)
    ad = shift_mix(ad_ref, mua_ref)
    gd = shift_mix(gd_ref, mug_ref)

    lane_sq = _iota((128, 128), 1)
    row_sq = _iota((128, 128), 0)
    same_head = ((lane_sq >= RW_K) == (row_sq >= RW_K))
    bd_ones = same_head.astype(F32)

    def head_sum(x):
        return _dot01_r(x, bd_ones)

    gate = _bdot(jax.nn.sigmoid(gd), gup_ref[...])
    kkv = kx * kk_ref[...]
    kkn = kkv * lax.rsqrt(head_sum(kkv * kkv) + NORM_EPS)
    tw = jnp.tanh(wd)
    rows_lo = _iota((128, 128), 0) < RW_K
    kd_sum = jnp.zeros((t, 128), F32)
    for d in range(2):
        half = rows_lo if d == 0 else jnp.logical_not(rows_lo)
        w = w0_ref[d:d + 1, :] + _bdot(tw, jnp.where(half, wup_ref[...], 0.0))
        wl_scr[d] = -jnp.exp(-_softplus(-w) - 0.5)
        a = jax.nn.sigmoid(a0_ref[d:d + 1, :] + _bdot(ad, jnp.where(half, aup_ref[...], 0.0)))
        kd = kx * (1.0 + (a - 1.0) * ka_ref[...])
        kd_scr[d] = kd
        kd_sum = kd_sum + kd
        bb_scr[d] = kkn * a
    r_scr[...] = r
    v_scr[...] = v
    aa_scr[...] = -kkn

    dup = ((_iota((RW_K, 128), 1) & (RW_K - 1)) == _iota((RW_K, 128), 0)).astype(F32)
    row_head1 = _iota((2 * L, 128), 0) >= L
    lane_head1 = _iota((2 * L, 128), 1) >= RW_K
    own_lanes = row_head1 == lane_head1
    sq_r, sq_c = _iota((2 * L, 2 * L), 0), _iota((2 * L, 2 * L), 1)
    same_blk = (sq_r >= L) == (sq_c >= L)
    pos_r, pos_c = sq_r & (L - 1), sq_c & (L - 1)
    ones_l = jnp.ones((L, 128), F32)
    bd_masks = [(same_blk & (pos_c <= pos_r), same_blk & (pos_c < pos_r)),
                (same_blk & (pos_c >= pos_r), same_blk & (pos_c > pos_r))]
    mm, mm_nt, mm_tn = _bdot, _bdot_nt, _bdot_tn
    grp = min(RW_GROUP, nc)

    def two(x):
        return jnp.where(own_lanes, jnp.concatenate([x, x], axis=0), 0.0)

    def prep(gi, carry):
        cs = [gi * grp + j for j in range(grp)]
        cds = [(j, d) for j in range(grp) for d in range(2)]
        rows = [pl.ds(pl.multiple_of(c * L, L), L) for c in cs]
        srows = [pl.ds(pl.multiple_of(c * 2 * L, 2 * L), 2 * L) for c in cs]
        vv2 = [two(v_scr[r, :]) for r in rows]
        wl = {(j, d): wl_scr[d, rows[j], :] for j, d in cds}
        cin = {jd: _dot01_l(_tri_masks(jd[1] == 1)[0].astype(F32), wl[jd]) for jd in cds}
        decay = {jd: jnp.exp(_dot01_tn(wl[jd], ones_l)) for jd in cds}
        ops = {}
        for j, d in cds:
            e_in, e_out = jnp.exp(cin[j, d]), jnp.exp(-cin[j, d])
            ops[j, d] = (two(aa_scr[rows[j], :] * jnp.exp(cin[j, d] - wl[j, d])), two(r_scr[rows[j], :] * e_in),
                         two(bb_scr[d, rows[j], :] * e_out), two(kd_scr[d, rows[j], :] * e_out))
        prod = {jd: mm_nt(jnp.concatenate(ops[jd][0:2], axis=0), jnp.concatenate(ops[jd][2:4], axis=0)) for jd in cds}
        blk = {}
        for j, d in cds:
            incl_bd, strict_bd = bd_masks[d]
            p = prod[j, d]
            blk[j, d] = (jnp.where(strict_bd, p[0:2 * L, 0:2 * L], 0.0), jnp.where(strict_bd, p[0:2 * L, 2 * L:], 0.0),
                         jnp.where(incl_bd, p[2 * L:, 0:2 * L], 0.0), jnp.where(incl_bd, p[2 * L:, 2 * L:], 0.0))
        packs = [-jnp.concatenate([blk[j, d][0][0:L, :] + blk[j, d][0][L:, :] for d in range(2)], axis=1)
                 for j in range(grp)]
        inv = _inv_unit_packed_multi(packs)
        t_inv = {(j, d): _to_bd(inv[j][:, d * 2 * L:(d + 1) * 2 * L]) for j, d in cds}
        kv = {(j, d): mm(jnp.concatenate(blk[j, d][1::2], axis=0), vv2[j]) for j, d in cds}
        m1c1 = {jd: mm(t_inv[jd], jnp.concatenate([ops[jd][0], kv[jd][0:2 * L, :]], axis=1)) for jd in cds}
        rm = {jd: mm(blk[jd][2], m1c1[jd]) for jd in cds}
        bm = {jd: mm_tn(ops[jd][2], m1c1[jd]) for jd in cds}
        kv2 = {(j, d): mm_tn(ops[j, d][3], vv2[j]) for j, d in cds}
        for j, d in cds:
            m2ah = jnp.concatenate([ops[j, d][1] + rm[j, d][:, 0:128], decay[j, d] * bm[j, d][:, 0:128]], axis=0)
            hi, lo = _split2(m2ah)
            mrows = pl.ds(pl.multiple_of(cs[j] * 4 * L, 4 * L), 4 * L)
            mah_scr[d, mrows, :] = hi
            mal_scr[d, mrows, :] = lo
            c2_scr[d, srows[j], :] = rm[j, d][:, 128:] + kv[j, d][2 * L:, :]
            bc_scr[d, srows[j], :] = decay[j, d] * (bm[j, d][:, 128:] + kv2[j, d])
            dg_scr[d, srows[j], :] = decay[j, d]
        return carry

    lax.fori_loop(0, nc // grp, prep, 0)

    for d in range(2):
        if has_state:
            stacked = s0_ref[0, d].reshape(2 * RW_K, RW_K)
            state_scr[d] = jnp.where(same_head, _dot01_r(stacked, dup), 0.0).T
        else:
            state_scr[d] = jnp.zeros((128, 128), F32)

    def scan(i, carry):
        for d in range(2):
            c = (nc - 1 - i) if d else i
            rows = pl.ds(pl.multiple_of(c * L, L), L)
            srows = pl.ds(pl.multiple_of(c * 2 * L, 2 * L), 2 * L)
            h_st = state_scr[d]
            mrows = pl.ds(pl.multiple_of(c * 4 * L, 4 * L), 4 * L)
            lh, ll = mah_scr[d, mrows, :], mal_scr[d, mrows, :]
            hh, hl = _split2(h_st)
            res = _dot(lh, hh)
            corr = res[2 * L:, :] + _dot(lh[2 * L:, :], hl) + _dot(ll[2 * L:, :], hh)
            y2 = res[0:2 * L, :] + c2_scr[d, srows, :]
            yacc_scr[d, rows, :] = y2[0:L, :] + y2[L:, :]
            state_scr[d] = dg_scr[d, srows, :] * h_st + corr + bc_scr[d, srows, :]
        return carry

    lax.fori_loop(0, nc, scan, 0)
    if not has_state:
        dup_t = ((_iota((128, RW_K), 0) & (RW_K - 1)) == _iota((128, RW_K), 1)).astype(F32)
        for d in range(2):
            st_ref[0, d] = _dot01_r(state_scr[d].T, dup_t).reshape(2, RW_K, RW_K)

    y = yacc_scr[0] + yacc_scr[1]
    mean = head_sum(y) * (1.0 / RW_K)
    yc = y - mean
    var = head_sum(yc * yc) * (1.0 / RW_K)
    yn = yc * lax.rsqrt(var + RWKV_LN_EPS) * lng_ref[...] + lnb_ref[...]
    bonus = head_sum(r * (0.5 * kd_sum) * rk_ref[...]) * v
    o_ref[...] = (yn + bonus) * gate


def _rwkv(proj, nseq, t, prm, state):
    has_state = state is not None
    mu, w0, wup, a0, aup, gup, k_k, k_a, r_k, ln_g, ln_b = prm
    seq = lambda base: (lambda b, hp: (b, base // 128 + hp))
    fix = lambda base: (lambda b, hp: (b, base // 128))
    mu_at = lambda base: (lambda b, hp: (0, (base - OD_R) // 128 + hp))
    mu_fix = lambda base: (lambda b, hp: (0, (base - OD_R) // 128))
    col = lambda b, hp: (0, hp)
    in_specs = [
        pl.BlockSpec((t, 128), seq(OD_R)), pl.BlockSpec((t, 128), seq(OD_RK)), pl.BlockSpec((t, 128), seq(OD_RV)),
        pl.BlockSpec((t, 128), fix(OD_WD)), pl.BlockSpec((t, 128), fix(OD_AD)), pl.BlockSpec((t, 128), fix(OD_GD)),
        pl.BlockSpec((1, 128), mu_at(OD_R)), pl.BlockSpec((1, 128), mu_at(OD_RK)), pl.BlockSpec((1, 128), mu_at(OD_RV)),
        pl.BlockSpec((1, 128), mu_fix(OD_WD)), pl.BlockSpec((1, 128), mu_fix(OD_AD)), pl.BlockSpec((1, 128), mu_fix(OD_GD)),
        pl.BlockSpec((2, 128), col), pl.BlockSpec((128, 128), col),
        pl.BlockSpec((2, 128), col), pl.BlockSpec((128, 128), col),
        pl.BlockSpec((128, 128), col),
        pl.BlockSpec((1, 128), col), pl.BlockSpec((1, 128), col), pl.BlockSpec((1, 128), col),
        pl.BlockSpec((1, 128), col), pl.BlockSpec((1, 128), col),
    ]
    args = [proj] * 6 + [mu] * 6 + [w0, wup, a0, aup, gup, k_k, k_a, r_k, ln_g, ln_b]
    st_block = (1, 2, 2, RW_K, RW_K)
    st_map = lambda b, hp: (b, 0, hp, 0, 0)
    out_specs = [pl.BlockSpec((t, 128), lambda b, hp: (b, hp))]
    out_shape = [jax.ShapeDtypeStruct((nseq * t, RW_HEADS * RW_K), F32)]
    if has_state:
        in_specs.append(pl.BlockSpec(st_block, st_map))
        args.append(state)
    else:
        out_specs.append(pl.BlockSpec(st_block, st_map))
        out_shape.append(jax.ShapeDtypeStruct((nseq, 2, RW_HEADS, RW_K, RW_K), F32))
    scratch = [pltpu.VMEM((t + 2 * CONV_PAD, 128), F32), pltpu.VMEM((t, 128), F32), pltpu.VMEM((t, 128), F32),
               pltpu.VMEM((t, 128), F32), pltpu.VMEM((2, t, 128), F32), pltpu.VMEM((2, t, 128), F32),
               pltpu.VMEM((2, t, 128), F32)] + [pltpu.VMEM((2, 4 * t, 128), BF16)] * 2 + [
               pltpu.VMEM((2, 2 * t, 128), F32)] * 3 + [
               pltpu.VMEM((2, t, 128), F32), pltpu.VMEM((2, 128, 128), F32)]
    return pl.pallas_call(
        functools.partial(_rwkv_kernel, t=t, has_state=has_state),
        grid=(nseq, RW_HEADS // 2), in_specs=in_specs, out_specs=out_specs, out_shape=out_shape,
        scratch_shapes=scratch, compiler_params=_cparams(("arbitrary", "arbitrary")), name="rwkv",
    )(*args)


def _even_w_in(w):
    za, xs, bm, cm = w[:, 0:1024], w[:, 1024:2048], w[:, 2048:2304], w[:, 2304:2560]
    dt, q, kk, v = w[:, 2560:2592], w[:, 2592:3616], w[:, 3616:4640], w[:, 4640:5664]
    zb, al, be = w[:, 5664:6688], w[:, 6688:6704], w[:, 6704:6720]
    pad = jnp.zeros((w.shape[0], EV_COLS - EV_SM - 64), w.dtype)
    return jnp.concatenate([za, xs, zb, q, kk, v, bm, cm, dt, al, be, pad], axis=1)


def _pad_row(parts):
    flat = jnp.concatenate([p.reshape(-1) for p in parts])
    return jnp.pad(flat, (0, 128 - flat.shape[0])).reshape(1, 128)


def kernel(x_prompt, x_sample, state_ssd, state_gdn, cache_k, cache_v, state_rwkv, c, c_ctx, ada_w, ada_b, norm1_g, norm2_g, ffn_w1, ffn_w2, final_g, ev_w_in, ev_w_out, ssd_conv_w, ssd_conv_b, ssd_a_log, ssd_dt_bias, ssd_d, ssd_norm_g, gdn_conv_w, gdn_a_log, gdn_dt_bias, gdn_norm_g, od_w_in, od_w_out, attn_sink, rwkv_mu, rwkv_w0, rwkv_w_up, rwkv_a0, rwkv_a_up, rwkv_g_up, rwkv_k_k, rwkv_k_a, rwkv_r_k, rwkv_ln_g, rwkv_ln_b):
    bp, tp, _ = x_prompt.shape
    bs, ts, _ = x_sample.shape
    streams = [(x_prompt.reshape(bp * tp, D), bp, tp, None), (x_sample.reshape(bs * ts, D), bs, ts, ts)]

    cvec = jnp.concatenate([c_ctx[None, :], c, jnp.zeros((16 - 1 - bs, D), F32)], axis=0)
    mod = _adaln(cvec, ada_w, ada_b)

    w_in0 = _even_w_in(ev_w_in[0]).astype(BF16)
    w_out0 = ev_w_out[0].astype(BF16)
    bias_row = _pad_row([ssd_dt_bias[0], gdn_dt_bias[0]])
    alog_row = _pad_row([ssd_a_log[0], gdn_a_log[0]])
    dskip_row = jnp.repeat(ssd_d[0], SSD_P).reshape(1, SSD_HEADS * SSD_P)
    w1_0, w2_0 = ffn_w1[0].astype(BF16), ffn_w2[0].astype(BF16)
    w_in1 = od_w_in[0].astype(BF16)
    w_out1 = od_w_out[0].astype(BF16)
    w1_1, w2_1 = ffn_w1[1].astype(BF16), ffn_w2[1].astype(BF16)
    rw_prm = (rwkv_mu[0].reshape(1, -1), rwkv_w0[0], rwkv_w_up[0].reshape(2 * 64, -1), rwkv_a0[0],
              rwkv_a_up[0].reshape(2 * 64, -1), rwkv_g_up[0], rwkv_k_k[0].reshape(1, -1), rwkv_k_a[0].reshape(1, -1),
              rwkv_r_k[0].reshape(1, -1), rwkv_ln_g[0].reshape(1, -1), rwkv_ln_b[0].reshape(1, -1))
    cos2, sin2 = _rope_tables(ts)
    past = cache_k.shape[2]
    ck = cache_k[:, 0].reshape(bs * past, ATT_KVH * ATT_DH)
    cv = cache_v[:, 0].reshape(bs * past, ATT_KVH * ATT_DH)

    outs = []
    for x, nseq, t, sample_t in streams:
        latent = sample_t is not None
        m0 = mod[0]
        proj = _inproj(x, norm1_g[0:1], m0, w_in0, sample_t)
        s_ssd = state_ssd[:, 0] if latent else None
        s_gdn = state_gdn[:, 0] if latent else None
        r_ssd = _ssd(proj, nseq, t, ssd_conv_w[0], ssd_conv_b[0].reshape(1, -1), bias_row, alog_row, dskip_row,
                     ssd_norm_g[0].reshape(1, -1), s_ssd)
        r_gdn = _gdn(proj, nseq, t, gdn_conv_w[0], bias_row, alog_row, gdn_norm_g[0].reshape(1, -1), s_gdn,
                     nsub=1 if latent else GDN_PROMPT_NSUB)
        x = _outproj(r_ssd[0], r_gdn[0], x, m0, w_out0, sample_t)
        x = _ffn(x, norm2_g[0:1], m0, w1_0, w2_0, final_g.reshape(1, D), sample_t, final=False)
        m1 = mod[1]
        proj = _inproj(x, norm1_g[1:2], m1, w_in1, sample_t)
        if latent:
            att = _lat_attn(proj, nseq, t, attn_sink[0], ck, cv, cos2, sin2)
            r_rw = _rwkv(proj, nseq, t, rw_prm, state_rwkv[:, 0])
        else:
            att = _ctx_attn(proj, nseq, t, attn_sink[0])
            r_rw = _rwkv(proj, nseq, t, rw_prm, None)
        x = _outproj(att, r_rw[0], x, m1, w_out1, sample_t)
        x = _ffn(x, norm2_g[1:2], m1, w1_1, w2_1, final_g.reshape(1, D), sample_t, final=True)
        outs.append((x.reshape(nseq, t, D), r_ssd, r_gdn, proj, r_rw))

    (y_prompt, p_ssd, p_gdn, p_proj, p_rw), (y_sample, _, _, _, _) = outs
    new_k = p_proj[:, OD_K:OD_V].reshape(bp, 1, tp, ATT_KVH, ATT_DH)
    new_v = p_proj[:, OD_V:OD_R].reshape(bp, 1, tp, ATT_KVH, ATT_DH)
    return (y_prompt, y_sample, p_ssd[1][:, None], p_gdn[1][:, None], new_k, new_v, p_rw[1][:, None])
```

```python
import functools

import numpy as np
import jax
import jax.numpy as jnp
from jax import lax
from jax.experimental import pallas as pl
from jax.experimental.pallas import tpu as pltpu

F32 = jnp.float32
BF16 = jnp.bfloat16
HI = lax.Precision.HIGHEST

D = 1024
NORM_EPS = 1e-6
L = 64
GDN_GROUP, RW_GROUP = 8, 4
GDN_PROMPT_NSUB, RW_PROMPT_NSUB = 4, 2
CONV_W = 5
CONV_PAD = 8

SSD_HEADS, SSD_P, SSD_N, SSD_GROUPS = 16, 64, 128, 2
SSD_GW = SSD_HEADS // SSD_GROUPS * SSD_P
GDN_HEADS, GDN_K = 8, 128
ATT_HEADS, ATT_KVH, ATT_DH, ATT_GROUP = 8, 2, 128, 4
WINDOW, QBLOCK, GRID_W, ROPE_BASE = 128, 128, 64, 10000.0
RW_HEADS, RW_K = 16, 64
RWKV_LN_EPS = 64e-5
FFN_H = 2816
FFN_TN = 256

EV_ZA, EV_XS, EV_ZB, EV_Q, EV_K, EV_V, EV_B, EV_C, EV_SM, EV_COLS = 0, 1024, 2048, 3072, 4096, 5120, 6144, 6400, 6656, 6784
SM_DT, SM_ALPHA, SM_BETA = 0, 32, 48
OD_Q, OD_K, OD_V, OD_R, OD_RK, OD_RV, OD_WD, OD_AD, OD_GD, OD_COLS = 0, 1024, 1280, 1536, 2560, 3584, 4608, 4736, 4864, 4992

VMEM_LIMIT = 56 * 1024 * 1024


def _cparams(sem):
    return pltpu.CompilerParams(dimension_semantics=sem, vmem_limit_bytes=VMEM_LIMIT)


def _dot(a, b, prec=None):
    return jnp.dot(a, b, preferred_element_type=F32, precision=prec)


def _dot_nt(a, b, prec=None):
    return lax.dot_general(a, b, (((1,), (1,)), ((), ())), preferred_element_type=F32, precision=prec)


def _dot_tn(a, b, prec=None):
    return lax.dot_general(a, b, (((0,), (0,)), ((), ())), preferred_element_type=F32, precision=prec)


def _bdot(a, b):
    return _dot(a.astype(BF16), b.astype(BF16))


def _bdot_nt(a, b):
    return _dot_nt(a.astype(BF16), b.astype(BF16))


def _bdot_tn(a, b):
    return _dot_tn(a.astype(BF16), b.astype(BF16))


def _silu(x):
    return x * jax.nn.sigmoid(x)


def _softplus(x):
    return jnp.maximum(x, 0.0) + jnp.log(1.0 + jnp.exp(-jnp.abs(x)))


def _iota(shape, dim):
    return lax.broadcasted_iota(jnp.int32, shape, dim)


def _tri_masks(rev):
    r, c = _iota((L, L), 0), _iota((L, L), 1)
    if rev:
        return c >= r, c > r
    return c <= r, c < r


def _inv_unit(a):
    eye = (_iota(a.shape, 0) == _iota(a.shape, 1)).astype(F32)
    x = eye - a
    p = _dot(a, a, HI)
    n = 2
    while True:
        x = x + _dot(x, p, HI)
        n *= 2
        if n >= L:
            break
        p = _dot(p, p, HI)
    return x


def _split3(x):
    hi = x.astype(BF16)
    r1 = x - hi.astype(F32)
    mid = r1.astype(BF16)
    lo = (r1 - mid.astype(F32)).astype(BF16)
    return hi, mid, lo


def _dot01_l(m01, x):
    mb = m01.astype(BF16)
    return _dot(jnp.concatenate([mb, mb, mb], axis=1), jnp.concatenate(_split3(x), axis=0))


def _dot01_r(x, m01):
    mb = m01.astype(BF16)
    return _dot(jnp.concatenate(_split3(x), axis=1), jnp.concatenate([mb, mb, mb], axis=0))


def _dot01_tn(x, m01):
    mb = m01.astype(BF16)
    return _dot_tn(jnp.concatenate(_split3(x), axis=0), jnp.concatenate([mb, mb, mb], axis=0))


def _dot3(a, b):
    ah = a.astype(BF16)
    al = (a - ah.astype(F32)).astype(BF16)
    bh = b.astype(BF16)
    bl = (b - bh.astype(F32)).astype(BF16)
    return _dot(ah, bh) + _dot(ah, bl) + _dot(al, bh)


def _to_bd(m):
    n = m.shape[1] // L
    blk = jnp.right_shift(_iota(m.shape, 1), L.bit_length() - 1)
    return jnp.concatenate([jnp.where(blk == j, m, 0.0) for j in range(n)], axis=0)


def _split2(x):
    hi = x.astype(BF16)
    return hi, (x - hi.astype(F32)).astype(BF16)


def _inv_unit_packed_multi(mats):
    eye = ((_iota(mats[0].shape, 1) & (L - 1)) == _iota(mats[0].shape, 0)).astype(F32)

    def mm3(lh, ll, ph, pl_):
        bh, bl = _to_bd(ph), _to_bd(pl_)
        return _dot(lh, bh) + _dot(lh, bl) + _dot(ll, bh)

    xs = [eye - a for a in mats]
    ps = [mm3(h, lo, h, lo) for h, lo in map(_split2, mats)]
    n = 2
    while True:
        sp = [_split2(p) for p in ps]
        sx = [_split2(x) for x in xs]
        n *= 2
        if n >= L:
            return [x + mm3(xh, xl, ph, pl_) for x, (xh, xl), (ph, pl_) in zip(xs, sx, sp)]
        res = [mm3(jnp.concatenate([xh, ph], axis=0), jnp.concatenate([xl, pl_], axis=0), ph, pl_)
               for (xh, xl), (ph, pl_) in zip(sx, sp)]
        xs = [x + r[0:L, :] for x, r in zip(xs, res)]
        ps = [r[L:, :] for r in res]


def _conv_silu(x, w_ref, b, pad_ref, t):
    c = x.shape[1]
    zeros = jnp.zeros((CONV_PAD, c), F32)
    pad_ref[0:CONV_PAD, 0:c] = zeros
    pad_ref[CONV_PAD + t:2 * CONV_PAD + t, 0:c] = zeros
    pad_ref[CONV_PAD:CONV_PAD + t, 0:c] = x
    acc = None
    for j in range(CONV_W):
        off = CONV_PAD - CONV_W // 2 + j
        term = pad_ref[off:off + t, 0:c] * w_ref[j:j + 1, :]
        acc = term if acc is None else acc + term
    if b is not None:
        acc = acc + b
    return _silu(acc)


def _adaln_kernel(c_ref, w_ref, b_ref, o_ref):
    c = c_ref[...]
    o_ref[0] = _bdot(_silu(c), w_ref[0]) + b_ref[0]


def _adaln(cvec, ada_w, ada_b):
    depth, _, n = ada_w.shape
    tn = 1536
    out = pl.pallas_call(
        _adaln_kernel,
        grid=(depth, n // tn),
        in_specs=[pl.BlockSpec((16, D), lambda l, j: (0, 0)),
                  pl.BlockSpec((1, D, tn), lambda l, j: (l, 0, j)),
                  pl.BlockSpec((1, 1, tn), lambda l, j: (l, 0, j))],
        out_specs=pl.BlockSpec((1, 16, tn), lambda l, j: (l, 0, j)),
        out_shape=jax.ShapeDtypeStruct((depth, 16, n), F32),
        compiler_params=_cparams(("arbitrary", "arbitrary")),
        name="adaln",
    )(cvec, ada_w, ada_b.reshape(depth, 1, n))
    return out.reshape(depth, 16, 6, D)


def _mod_row_map(tm, t):
    if t is None:
        return lambda i: (0, 0, 0)
    per = t // tm
    return lambda i: (1 + i // per, 0, 0)


def _norm_mod(x, g, m, shift_idx, scale_idx):
    var = jnp.mean(x * x, axis=-1, keepdims=True)
    y = x * lax.rsqrt(var + NORM_EPS) * g
    return y * (1.0 + m[scale_idx:scale_idx + 1]) + m[shift_idx:shift_idx + 1]


def _inproj_kernel(x_ref, g_ref, mod_ref, w_ref, o_ref, *, tn):
    h = _norm_mod(x_ref[...], g_ref[...], mod_ref[0], 0, 1).astype(BF16)
    n = w_ref.shape[1]
    for j in range(0, n, tn):
        w = min(tn, n - j)
        o_ref[:, j:j + w] = _dot(h, w_ref[:, j:j + w])


def _inproj(x, g, mod, w, sample_t, tm=256, tn=512):
    rows, n = x.shape[0], w.shape[1]
    return pl.pallas_call(
        functools.partial(_inproj_kernel, tn=tn),
        grid=(rows // tm,),
        in_specs=[pl.BlockSpec((tm, D), lambda i: (i, 0)),
                  pl.BlockSpec((1, D), lambda i: (0, 0)),
                  pl.BlockSpec((1, 6, D), _mod_row_map(tm, sample_t)),
                  pl.BlockSpec((D, n), lambda i: (0, 0))],
        out_specs=pl.BlockSpec((tm, n), lambda i: (i, 0)),
        out_shape=jax.ShapeDtypeStruct((rows, n), F32),
        compiler_params=_cparams(("arbitrary",)),
        name="inproj",
    )(x, g, mod, w)


def _mix_ffn_kernel(a1_ref, a2_ref, x_ref, g_ref, mod_ref, wo_ref, w1_ref, w2_ref, fg_ref, o_ref, *, final):
    m = mod_ref[0]
    k1 = a1_ref.shape[1]
    mix = _dot(a1_ref[...], wo_ref[0:k1, :]) + _dot(a2_ref[...], wo_ref[k1:, :])
    x = x_ref[...] + m[2:3] * mix
    h = _norm_mod(x, g_ref[...], m, 3, 4).astype(BF16)
    acc = jnp.zeros(x.shape, F32)
    for j in range(0, FFN_H, FFN_TN):
        gate = _dot(h, w1_ref[:, j:j + FFN_TN])
        up = _dot(h, w1_ref[:, FFN_H + j:FFN_H + j + FFN_TN])
        acc = acc + _dot((_silu(gate) * up).astype(BF16), w2_ref[j:j + FFN_TN, :])
    y = x + m[5:6] * acc
    if final:
        var = jnp.mean(y * y, axis=-1, keepdims=True)
        y = y * lax.rsqrt(var + NORM_EPS) * fg_ref[...]
    o_ref[...] = y


def _mix_ffn(a1, a2, x, g, mod, w_out, w1, w2, final_g, sample_t, final, tm=256):
    rows = x.shape[0]
    once = dict(pipeline_mode=pl.Buffered(1))
    return pl.pallas_call(
        functools.partial(_mix_ffn_kernel, final=final),
        grid=(rows // tm,),
        in_specs=[pl.BlockSpec((tm, a1.shape[1]), lambda i: (i, 0)),
                  pl.BlockSpec((tm, a2.shape[1]), lambda i: (i, 0)),
                  pl.BlockSpec((tm, D), lambda i: (i, 0)),
                  pl.BlockSpec((1, D), lambda i: (0, 0)),
                  pl.BlockSpec((1, 6, D), _mod_row_map(tm, sample_t)),
                  pl.BlockSpec(w_out.shape, lambda i: (0, 0), **once),
                  pl.BlockSpec(w1.shape, lambda i: (0, 0), **once),
                  pl.BlockSpec(w2.shape, lambda i: (0, 0), **once),
                  pl.BlockSpec((1, D), lambda i: (0, 0))],
        out_specs=pl.BlockSpec((tm, D), lambda i: (i, 0)),
        out_shape=jax.ShapeDtypeStruct((rows, D), F32),
        compiler_params=_cparams(("arbitrary",)),
        name="mix_ffn",
    )(a1, a2, x, g, mod, w_out, w1, w2, final_g)


def _ssd_kernel(*refs, t, has_state):
    (za_ref, xs_ref, bm_ref, cm_ref, sm_ref, cwx_ref, cwb_ref, cwc_ref, cbx_ref, cbb_ref, cbc_ref,
     bias_ref, alog_ref, dsk_ref, ng_ref) = refs[:15]
    k = 15
    s0_ref = st_ref = None
    if has_state:
        s0_ref = refs[k]
        k += 1
    y_ref = refs[k]
    k += 1
    if not has_state:
        st_ref = refs[k]
        k += 1
    pad_scr, xs_scr, bb_scr, cc_scr, dt_scr, da_scr, dasm_scr, yacc_scr, state_scr = refs[k:]

    g = pl.program_id(1)
    nc = t // L
    npair = SSD_GW // 128

    xs_scr[...] = _conv_silu(xs_ref[...], cwx_ref, cbx_ref[...], pad_scr, t)
    bb_scr[...] = _conv_silu(bm_ref[...], cwb_ref, cbb_ref[...], pad_scr, t)
    cc_scr[...] = _conv_silu(cm_ref[...], cwc_ref, cbc_ref[...], pad_scr, t)

    dtv = _softplus(sm_ref[...] + bias_ref[...])
    dav = dtv * (-jnp.exp(alog_ref[...]))
    heads_per_group = SSD_HEADS // SSD_GROUPS
    for d in range(2):
        first = SM_DT + d * SSD_HEADS + g * heads_per_group
        src = _iota((128, SSD_GW), 0)
        e = (src == first + jnp.right_shift(_iota((128, SSD_GW), 1), 6)).astype(F32)
        dt_scr[d] = _dot01_r(dtv, e)
        da_scr[d] = _dot01_r(dav, e)
        sel = (_iota((128, 128), 0) == first + _iota((128, 128), 1)) & (_iota((128, 128), 1) < heads_per_group)
        dasm_scr[d] = _dot01_r(dav, sel.astype(F32))

    lane128 = _iota((L, 128), 1)
    for d in range(2):
        if has_state:
            state_scr[d] = s0_ref[0, d].reshape(SSD_GW, SSD_N).T
        else:
            state_scr[d] = jnp.zeros((SSD_N, SSD_GW), F32)

    def body(ci, carry):
        for d in range(2):
            rev = d == 1
            incl = _tri_masks(rev)[0]
            tri = incl.astype(F32)
            tri_t = _tri_masks(not rev)[0].astype(F32)
            last = 0 if rev else L - 1
            c = (nc - 1 - ci) if rev else ci
            r0 = pl.multiple_of(c * L, L)
            rows = pl.ds(r0, L)
            x = xs_scr[rows, :]
            bc = bb_scr[rows, :]
            cc = cc_scr[rows, :]
            acum = _dot01_l(tri, da_scr[d, rows, :])
            acum_t = _dot01_tn(dasm_scr[d, rows, :], tri_t)
            xdt = x * dt_scr[d, rows, :]
            cb = _bdot_nt(cc, bc)
            total = acum[last:last + 1, :]
            s_in = state_scr[d]
            y_inter = _bdot(cc, s_in) * jnp.exp(acum)
            pieces = []
            for j in range(npair):
                xp = xdt[:, j * 128:(j + 1) * 128]
                yp = None
                for hh in range(2):
                    h = 2 * j + hh
                    col = acum[:, h * SSD_P:h * SSD_P + 1]
                    row = acum_t[h:h + 1, :]
                    dec = jnp.where(incl, jnp.exp(jnp.minimum(col - row, 0.0)), 0.0)
                    mine = (lane128 >= SSD_P) if hh else (lane128 < SSD_P)
                    term = _bdot(cb * dec, jnp.where(mine, xp, 0.0))
                    yp = term if yp is None else yp + term
                pieces.append(yp)
            yacc_scr[d, rows, :] = jnp.concatenate(pieces, axis=1) + y_inter
            wx = jnp.exp(total - acum) * xdt
            state_scr[d] = s_in * jnp.exp(total) + _bdot_tn(bc, wx)
        return carry

    lax.fori_loop(0, nc, body, 0)
    if not has_state:
        for d in range(2):
            st_ref[0, d] = state_scr[d].T.reshape(heads_per_group, SSD_P, SSD_N)

    y = yacc_scr[0] + yacc_scr[1] + dsk_ref[...] * xs_scr[...]
    y = y * _silu(za_ref[...])
    var = jnp.mean(y * y, axis=-1, keepdims=True)
    y_ref[...] = (y * lax.rsqrt(var + NORM_EPS) * ng_ref[...]).astype(BF16)


def _ssd(proj, nseq, t, conv_w, conv_b, bias_row, alog_row, dskip_row, norm_g, state):
    has_state = state is not None
    seq = lambda blk: (lambda b, g: (b, blk(g)))
    const = lambda blk: (lambda b, g: (0, blk(g)))
    in_specs = [
        pl.BlockSpec((t, SSD_GW), seq(lambda g: EV_ZA // SSD_GW + g)),
        pl.BlockSpec((t, SSD_GW), seq(lambda g: EV_XS // SSD_GW + g)),
        pl.BlockSpec((t, 128), seq(lambda g: EV_B // 128 + g)),
        pl.BlockSpec((t, 128), seq(lambda g: EV_C // 128 + g)),
        pl.BlockSpec((t, 128), seq(lambda g: EV_SM // 128)),
        pl.BlockSpec((CONV_W, SSD_GW), const(lambda g: g)),
        pl.BlockSpec((CONV_W, 128), const(lambda g: 1024 // 128 + g)),
        pl.BlockSpec((CONV_W, 128), const(lambda g: 1280 // 128 + g)),
        pl.BlockSpec((1, SSD_GW), const(lambda g: g)),
        pl.BlockSpec((1, 128), const(lambda g: 1024 // 128 + g)),
        pl.BlockSpec((1, 128), const(lambda g: 1280 // 128 + g)),
        pl.BlockSpec((1, 128), const(lambda g: 0)),
        pl.BlockSpec((1, 128), const(lambda g: 0)),
        pl.BlockSpec((1, SSD_GW), const(lambda g: g)),
        pl.BlockSpec((1, SSD_GW), const(lambda g: g)),
    ]
    args = [proj, proj, proj, proj, proj, conv_w, conv_w, conv_w, conv_b, conv_b, conv_b,
            bias_row, alog_row, dskip_row, norm_g]
    st_block = (1, 2, SSD_HEADS // SSD_GROUPS, SSD_P, SSD_N)
    st_map = lambda b, g: (b, 0, g, 0, 0)
    out_specs = [pl.BlockSpec((t, SSD_GW), lambda b, g: (b, g))]
    out_shape = [jax.ShapeDtypeStruct((nseq * t, SSD_HEADS * SSD_P), BF16)]
    if has_state:
        in_specs.append(pl.BlockSpec(st_block, st_map))
        args.append(state)
    else:
        out_specs.append(pl.BlockSpec(st_block, st_map))
        out_shape.append(jax.ShapeDtypeStruct((nseq, 2, SSD_HEADS, SSD_P, SSD_N), F32))
    scratch = [pltpu.VMEM((t + 2 * CONV_PAD, SSD_GW), F32), pltpu.VMEM((t, SSD_GW), F32),
               pltpu.VMEM((t, 128), F32), pltpu.VMEM((t, 128), F32),
               pltpu.VMEM((2, t, SSD_GW), F32), pltpu.VMEM((2, t, SSD_GW), F32), pltpu.VMEM((2, t, 128), F32),
               pltpu.VMEM((2, t, SSD_GW), F32), pltpu.VMEM((2, SSD_N, SSD_GW), F32)]
    return pl.pallas_call(
        functools.partial(_ssd_kernel, t=t, has_state=has_state),
        grid=(nseq, SSD_GROUPS), in_specs=in_specs, out_specs=out_specs, out_shape=out_shape,
        scratch_shapes=scratch, compiler_params=_cparams(("arbitrary", "arbitrary")), name="ssd",
    )(*args)


def _gdn_kernel(*refs, t, nsub, has_state):
    (q_ref, k_ref, v_ref, zb_ref, sm_ref, cwq_ref, cwk_ref, cwv_ref, bias_ref, alog_ref, ng_ref) = refs[:11]
    k = 11
    s0_ref = st_ref = None
    if has_state:
        s0_ref = refs[k]
        k += 1
    o_ref = refs[k]
    k += 1
    if not has_state:
        st_ref = refs[k]
        k += 1
    (pad_scr, q_scr, k_scr, v_scr, g_scr, b_scr, ma_scr, c2_scr, bc_scr, dg_scr,
     oacc_scr, state_scr) = refs[k:]

    h = pl.program_id(1)
    nc_seq = t // L
    nc = nsub * nc_seq
    rows_all = nsub * t

    def l2n(x):
        return x * lax.rsqrt(jnp.sum(x * x, axis=-1, keepdims=True) + NORM_EPS)

    for s in range(nsub):
        rs = slice(s * t, (s + 1) * t)
        q_scr[rs, :] = l2n(_conv_silu(q_ref[rs, :], cwq_ref, None, pad_scr, t)) * (GDN_K ** -0.5)
        k_scr[rs, :] = l2n(_conv_silu(k_ref[rs, :], cwk_ref, None, pad_scr, t))
        v_scr[rs, :] = _conv_silu(v_ref[rs, :], cwv_ref, None, pad_scr, t)

    sm = sm_ref[...]
    gl = -jnp.exp(alog_ref[...]) * _softplus(sm + bias_ref[...])
    bt = jax.nn.sigmoid(sm)
    lane = _iota((rows_all, 128), 1)
    for d in range(2):
        gcol = jnp.sum(jnp.where(lane == SM_ALPHA + d * GDN_HEADS + h, gl, 0.0), axis=1, keepdims=True)
        bcol = jnp.sum(jnp.where(lane == SM_BETA + d * GDN_HEADS + h, bt, 0.0), axis=1, keepdims=True)
        g_scr[d] = jnp.broadcast_to(gcol, (rows_all, 128))
        b_scr[d] = jnp.broadcast_to(bcol, (rows_all, 128))

    masks = [_tri_masks(False), _tri_masks(True)]
    grp = min(GDN_GROUP, nc)

    def prep(gi, carry):
        cs = [gi * grp + j for j in range(grp)]
        rows = [pl.ds(pl.multiple_of(c * L, L), L) for c in cs]
        qkv = [(q_scr[r, :], k_scr[r, :], v_scr[r, :]) for r in rows]
        qkk = [_bdot_nt(jnp.concatenate([qc, kc], axis=0), kc) for qc, kc, _ in qkv]
        gbs = [[(g_scr[d, r, :], b_scr[d, r, :]) for d in range(2)] for r in rows]
        gam = [[_dot01_l(masks[d][0].astype(F32), gbs[j][d][0]) for d in range(2)] for j in range(grp)]
        gam_row = [[_dot01_tn(gbs[j][d][0], masks[1 - d][0].astype(F32))[0:1, :] for d in range(2)]
                   for j in range(grp)]
        per = []
        for j in range(grp):
            qc, kc, vc = qkv[j]
            row = []
            for d in range(2):
                incl, strict = masks[d]
                bb = gbs[j][d][1]
                g = gam[j][d]
                dec = jnp.where(incl, jnp.exp(jnp.minimum(g[:, 0:1] - gam_row[j][d], 0.0)), 0.0)
                a_mat = jnp.where(strict, bb[:, 0:1] * qkk[j][L:, :] * dec, 0.0)
                eg = jnp.exp(g)
                last = 0 if d else L - 1
                g_last = g[last:last + 1, :]
                rhs = jnp.concatenate([bb * vc, bb * kc * eg], axis=1)
                row.append((a_mat, rhs, qkk[j][0:L, :] * dec, qc * eg, kc * jnp.exp(g_last - g), g_last))
            per.append(row)
        packs = [jnp.concatenate([per[j][0][0], per[j][1][0], per[j + 1][0][0], per[j + 1][1][0]], axis=1)
                 for j in range(0, grp, 2)]
        inv = _inv_unit_packed_multi(packs)
        t_inv = [_to_bd(inv[j // 2][:, (j % 2) * 2 * L:(j % 2 + 1) * 2 * L]) for j in range(grp)]
        uw = [_bdot(t_inv[j], jnp.concatenate([per[j][0][1], per[j][1][1]], axis=0))
              for j in range(grp)]
        qk_uw = [_bdot(_to_bd(jnp.concatenate([per[j][0][2], per[j][1][2]], axis=1)), uw[j])
                 for j in range(grp)]
        ke_uw = [[_bdot_tn(per[j][d][4], uw[j][d * L:(d + 1) * L, :]) for d in range(2)]
                 for j in range(grp)]
        for j, c in enumerate(cs):
            srows = pl.ds(pl.multiple_of(c * GDN_K, GDN_K), GDN_K)
            drows = pl.ds(pl.multiple_of(c * 8, 8), 8)
            for d in range(2):
                half = slice(d * L, (d + 1) * L)
                mrows = pl.ds(pl.multiple_of(c * (L + GDN_K), L + GDN_K), L + GDN_K)
                ma_scr[d, mrows, :] = jnp.concatenate(
                    [per[j][d][3] - qk_uw[j][half, GDN_K:], -ke_uw[j][d][:, GDN_K:]], axis=0).astype(BF16)
                c2_scr[d, rows[j], :] = qk_uw[j][half, 0:GDN_K]
                bc_scr[d, srows, :] = ke_uw[j][d][:, 0:GDN_K]
                dg_scr[d, drows, :] = jnp.broadcast_to(jnp.exp(per[j][d][5]), (8, 128))
        return carry

    lax.fori_loop(0, nc // grp, prep, 0)

    chains = [(s, d) for s in range(nsub) for d in range(2)]
    for s, d in chains:
        state_scr[2 * s + d] = s0_ref[s, d, 0] if has_state else jnp.zeros((GDN_K, GDN_K), F32)

    def scan(i, carry):
        for s, d in chains:
            c = s * nc_seq + ((nc_seq - 1 - i) if d else i)
            rows = pl.ds(pl.multiple_of(c * L, L), L)
            srows = pl.ds(pl.multiple_of(c * GDN_K, GDN_K), GDN_K)
            drows = pl.ds(pl.multiple_of(c * 8, 8), 8)
            st = state_scr[2 * s + d]
            mrows = pl.ds(pl.multiple_of(c * (L + GDN_K), L + GDN_K), L + GDN_K)
            res = _dot(ma_scr[d, mrows, :], st.astype(BF16))
            oacc_scr[d, rows, :] = res[0:L, :] + c2_scr[d, rows, :]
            state_scr[2 * s + d] = dg_scr[d, drows, :][0:1, :] * st + res[L:, :] + bc_scr[d, srows, :]
        return carry

    lax.fori_loop(0, nc_seq, scan, 0)
    if not has_state:
        for s, d in chains:
            st_ref[s, d, 0] = state_scr[2 * s + d]

    o = oacc_scr[0] + oacc_scr[1]
    var = jnp.mean(o * o, axis=-1, keepdims=True)
    o_ref[...] = (o * lax.rsqrt(var + NORM_EPS) * ng_ref[...] * _silu(zb_ref[...])).astype(BF16)


def _gdn(proj, nseq, t, conv_w, bias_row, alog_row, norm_g, state, nsub=1):
    has_state = state is not None
    seq = lambda base: (lambda b, h: (b, base // 128 + h))
    const = lambda base: (lambda b, h: (0, base // 128 + h))
    rb = nsub * t
    in_specs = [
        pl.BlockSpec((rb, 128), seq(EV_Q)), pl.BlockSpec((rb, 128), seq(EV_K)), pl.BlockSpec((rb, 128), seq(EV_V)),
        pl.BlockSpec((rb, 128), seq(EV_ZB)), pl.BlockSpec((rb, 128), lambda b, h: (b, EV_SM // 128)),
        pl.BlockSpec((CONV_W, 128), const(0)), pl.BlockSpec((CONV_W, 128), const(1024)),
        pl.BlockSpec((CONV_W, 128), const(2048)),
        pl.BlockSpec((1, 128), lambda b, h: (0, 0)), pl.BlockSpec((1, 128), lambda b, h: (0, 0)),
        pl.BlockSpec((1, 128), lambda b, h: (0, 0)),
    ]
    args = [proj, proj, proj, proj, proj, conv_w, conv_w, conv_w, bias_row, alog_row, norm_g]
    st_block = (nsub, 2, 1, GDN_K, GDN_K)
    st_map = lambda b, h: (b, 0, h, 0, 0)
    out_specs = [pl.BlockSpec((rb, 128), lambda b, h: (b, h))]
    out_shape = [jax.ShapeDtypeStruct((nseq * t, GDN_HEADS * GDN_K), BF16)]
    if has_state:
        in_specs.append(pl.BlockSpec(st_block, st_map))
        args.append(state)
    else:
        out_specs.append(pl.BlockSpec(st_block, st_map))
        out_shape.append(jax.ShapeDtypeStruct((nseq, 2, GDN_HEADS, GDN_K, GDN_K), F32))
    nc = rb // L
    scratch = [pltpu.VMEM((t + 2 * CONV_PAD, 128), F32), pltpu.VMEM((rb, 128), F32), pltpu.VMEM((rb, 128), F32),
               pltpu.VMEM((rb, 128), F32), pltpu.VMEM((2, rb, 128), F32), pltpu.VMEM((2, rb, 128), F32),
               pltpu.VMEM((2, nc * (L + GDN_K), 128), BF16), pltpu.VMEM((2, rb, 128), F32),
               pltpu.VMEM((2, nc * GDN_K, GDN_K), F32),
               pltpu.VMEM((2, nc * 8, 128), F32),
               pltpu.VMEM((2, rb, 128), F32), pltpu.VMEM((2 * nsub, GDN_K, GDN_K), F32)]
    return pl.pallas_call(
        functools.partial(_gdn_kernel, t=t, nsub=nsub, has_state=has_state),
        grid=(nseq // nsub, GDN_HEADS), in_specs=in_specs, out_specs=out_specs, out_shape=out_shape,
        scratch_shapes=scratch, compiler_params=_cparams(("arbitrary", "arbitrary")), name="gdn",
    )(*args)


def _softmax_sink_pv(scores, sink, values):
    m = sink
    for s in scores:
        m = jnp.maximum(m, jnp.max(s, axis=-1, keepdims=True))
    den = jnp.exp(sink - m)
    ps = []
    for s in scores:
        p = jnp.exp(s - m)
        den = den + jnp.sum(p, axis=-1, keepdims=True)
        ps.append(p)
    out = None
    for p, v in zip(ps, values):
        term = _bdot(p / den, v)
        out = term if out is None else out + term
    return out


def _ctx_attn_kernel(sink_ref, q_ref, k_ref, v_ref, o_ref):
    kvh = pl.program_id(1)
    scale = ATT_DH ** -0.5
    kk, vv = k_ref[...], v_ref[...]
    for gq in range(ATT_GROUP):
        q = q_ref[:, gq * ATT_DH:(gq + 1) * ATT_DH]
        s = _bdot_nt(q, kk) * scale
        sink = sink_ref[kvh * ATT_GROUP + gq]
        o_ref[:, gq * ATT_DH:(gq + 1) * ATT_DH] = _softmax_sink_pv([s], sink, [vv]).astype(BF16)


def _ctx_attn(proj, nseq, t, sink):
    gw = ATT_GROUP * ATT_DH
    return pl.pallas_call(
        _ctx_attn_kernel,
        grid=(nseq, ATT_KVH),
        in_specs=[pl.BlockSpec(memory_space=pltpu.SMEM),
                  pl.BlockSpec((t, gw), lambda b, kv: (b, OD_Q // gw + kv)),
                  pl.BlockSpec((t, ATT_DH), lambda b, kv: (b, OD_K // ATT_DH + kv)),
                  pl.BlockSpec((t, ATT_DH), lambda b, kv: (b, OD_V // ATT_DH + kv))],
        out_specs=pl.BlockSpec((t, gw), lambda b, kv: (b, kv)),
        out_shape=jax.ShapeDtypeStruct((nseq * t, ATT_HEADS * ATT_DH), BF16),
        compiler_params=_cparams(("arbitrary", "arbitrary")), name="ctx_attn",
    )(sink, proj, proj, proj)


def _lat_attn_kernel(sink_ref, q_ref, k_ref, v_ref, kc_ref, vc_ref, cos_ref, sin_ref, o_ref, kp_scr, vp_scr, *, t):
    kvh = pl.program_id(1)
    scale = ATT_DH ** -0.5
    band = QBLOCK + 2 * WINDOW
    cos, sin = cos_ref[...], sin_ref[...]

    def rope(x):
        return x * cos + pltpu.roll(x, ATT_DH // 2, 1) * sin

    zeros = jnp.zeros((WINDOW, ATT_DH), F32)
    kp_scr[0:WINDOW, :] = zeros
    kp_scr[WINDOW + t:, :] = zeros
    kp_scr[WINDOW:WINDOW + t, :] = rope(k_ref[...])
    vp_scr[0:WINDOW, :] = zeros
    vp_scr[WINDOW + t:, :] = zeros
    vp_scr[WINDOW:WINDOW + t, :] = v_ref[...]
    kc, vc = kc_ref[...], vc_ref[...]

    def body(i, carry):
        start = pl.multiple_of(i * QBLOCK, QBLOCK)
        ki = kp_scr[pl.ds(start, band), :]
        vi = vp_scr[pl.ds(start, band), :]
        qpos = start + _iota((QBLOCK, band), 0)
        kpos = start - WINDOW + _iota((QBLOCK, band), 1)
        valid = (jnp.abs(qpos - kpos) <= WINDOW) & (kpos >= 0) & (kpos < t)
        cs = cos_ref[pl.ds(start, QBLOCK), :]
        sn = sin_ref[pl.ds(start, QBLOCK), :]
        for gq in range(ATT_GROUP):
            q = q_ref[pl.ds(start, QBLOCK), gq * ATT_DH:(gq + 1) * ATT_DH]
            q = q * cs + pltpu.roll(q, ATT_DH // 2, 1) * sn
            s_band = jnp.where(valid, _bdot_nt(q, ki) * scale, -jnp.inf)
            s_ctx = _bdot_nt(q, kc) * scale
            sink = sink_ref[kvh * ATT_GROUP + gq]
            o_ref[pl.ds(start, QBLOCK), gq * ATT_DH:(gq + 1) * ATT_DH] = _softmax_sink_pv(
                [s_band, s_ctx], sink, [vi, vc]).astype(BF16)
        return carry

    lax.fori_loop(0, t // QBLOCK, body, 0)


def _lat_attn(proj, nseq, t, sink, cache_k, cache_v, cos2, sin2):
    gw = ATT_GROUP * ATT_DH
    past = cache_k.shape[0] // nseq
    return pl.pallas_call(
        functools.partial(_lat_attn_kernel, t=t),
        grid=(nseq, ATT_KVH),
        in_specs=[pl.BlockSpec(memory_space=pltpu.SMEM),
                  pl.BlockSpec((t, gw), lambda b, kv: (b, OD_Q // gw + kv)),
                  pl.BlockSpec((t, ATT_DH), lambda b, kv: (b, OD_K // ATT_DH + kv)),
                  pl.BlockSpec((t, ATT_DH), lambda b, kv: (b, OD_V // ATT_DH + kv)),
                  pl.BlockSpec((past, ATT_DH), lambda b, kv: (b, kv)),
                  pl.BlockSpec((past, ATT_DH), lambda b, kv: (b, kv)),
                  pl.BlockSpec((t, ATT_DH), lambda b, kv: (0, 0)),
                  pl.BlockSpec((t, ATT_DH), lambda b, kv: (0, 0))],
        out_specs=pl.BlockSpec((t, gw), lambda b, kv: (b, kv)),
        out_shape=jax.ShapeDtypeStruct((nseq * t, ATT_HEADS * ATT_DH), BF16),
        scratch_shapes=[pltpu.VMEM((t + 2 * WINDOW, ATT_DH), F32), pltpu.VMEM((t + 2 * WINDOW, ATT_DH), F32)],
        compiler_params=_cparams(("arbitrary", "arbitrary")), name="lat_attn",
    )(sink, proj, proj, proj, cache_k, cache_v, cos2, sin2)


def _rope_tables(t):
    rows = t // GRID_W
    row = jnp.repeat(jnp.arange(rows), GRID_W).astype(F32)
    col = jnp.tile(jnp.arange(GRID_W), rows).astype(F32)
    n_freq = ATT_DH // 4
    inv = ROPE_BASE ** (-jnp.arange(n_freq, dtype=F32) / n_freq)
    ang = jnp.concatenate([row[:, None] * inv, col[:, None] * inv], axis=-1)
    cos, sin = jnp.cos(ang), jnp.sin(ang)
    return jnp.concatenate([cos, cos], axis=-1), jnp.concatenate([-sin, sin], axis=-1)


def _rwkv_kernel(*refs, t, nsub, has_state):
    (r_ref, k_ref, v_ref, wd_ref, ad_ref, gd_ref, mur_ref, muk_ref, muv_ref, muw_ref, mua_ref, mug_ref,
     w0_ref, wup_ref, a0_ref, aup_ref, gup_ref, kk_ref, ka_ref, rk_ref, lng_ref, lnb_ref) = refs[:22]
    k = 22
    s0_ref = st_ref = None
    if has_state:
        s0_ref = refs[k]
        k += 1
    o_ref = refs[k]
    k += 1
    if not has_state:
        st_ref = refs[k]
        k += 1
    (pad_scr, r_scr, v_scr, aa_scr, wl_scr, kd_scr, bb_scr, mah_scr, mal_scr, c2_scr, bc_scr, dg_scr,
     yacc_scr, state_scr) = refs[k:]
    nc_seq = t // L
    nc = nsub * nc_seq

    def shift_mix(x_ref, mu_ref):
        parts = []
        for s in range(nsub):
            x = x_ref[s * t:(s + 1) * t, :]
            zeros = jnp.zeros((CONV_PAD, 128), F32)
            pad_scr[0:CONV_PAD, :] = zeros
            pad_scr[CONV_PAD + t:, :] = zeros
            pad_scr[CONV_PAD:CONV_PAD + t, :] = x
            nb = 0.5 * (pad_scr[CONV_PAD - 1:CONV_PAD - 1 + t, :] + pad_scr[CONV_PAD + 1:CONV_PAD + 1 + t, :])
            parts.append(x + mu_ref[...] * (nb - x))
        return parts[0] if nsub == 1 else jnp.concatenate(parts, axis=0)

    r = shift_mix(r_ref, mur_ref)
    kx = shift_mix(k_ref, muk_ref)
    v = shift_mix(v_ref, muv_ref)
    wd = shift_mix(wd_ref, muw_ref)
    ad = shift_mix(ad_ref, mua_ref)
    gd = shift_mix(gd_ref, mug_ref)

    lane_sq = _iota((128, 128), 1)
    row_sq = _iota((128, 128), 0)
    same_head = ((lane_sq >= RW_K) == (row_sq >= RW_K))
    bd_ones = same_head.astype(F32)

    def head_sum(x):
        return _dot01_r(x, bd_ones)

    gate = _bdot(jax.nn.sigmoid(gd), gup_ref[...])
    kkv = kx * kk_ref[...]
    kkn = kkv * lax.rsqrt(head_sum(kkv * kkv) + NORM_EPS)
    tw = jnp.tanh(wd)
    rows_lo = _iota((128, 128), 0) < RW_K
    kd_sum = jnp.zeros((nsub * t, 128), F32)
    for d in range(2):
        half = rows_lo if d == 0 else jnp.logical_not(rows_lo)
        w = w0_ref[d:d + 1, :] + _bdot(tw, jnp.where(half, wup_ref[...], 0.0))
        wl_scr[d] = -jnp.exp(-_softplus(-w) - 0.5)
        a = jax.nn.sigmoid(a0_ref[d:d + 1, :] + _bdot(ad, jnp.where(half, aup_ref[...], 0.0)))
        kd = kx * (1.0 + (a - 1.0) * ka_ref[...])
        kd_scr[d] = kd
        kd_sum = kd_sum + kd
        bb_scr[d] = kkn * a
    r_scr[...] = r
    v_scr[...] = v
    aa_scr[...] = -kkn

    dup = ((_iota((RW_K, 128), 1) & (RW_K - 1)) == _iota((RW_K, 128), 0)).astype(F32)
    row_head1 = _iota((2 * L, 128), 0) >= L
    lane_head1 = _iota((2 * L, 128), 1) >= RW_K
    own_lanes = row_head1 == lane_head1
    sq_r, sq_c = _iota((2 * L, 2 * L), 0), _iota((2 * L, 2 * L), 1)
    same_blk = (sq_r >= L) == (sq_c >= L)
    pos_r, pos_c = sq_r & (L - 1), sq_c & (L - 1)
    ones_l = jnp.ones((L, 128), F32)
    bd_masks = [(same_blk & (pos_c <= pos_r), same_blk & (pos_c < pos_r)),
                (same_blk & (pos_c >= pos_r), same_blk & (pos_c > pos_r))]
    mm, mm_nt, mm_tn = _bdot, _bdot_nt, _bdot_tn
    grp = min(RW_GROUP, nc)

    def two(x):
        return jnp.where(own_lanes, jnp.concatenate([x, x], axis=0), 0.0)

    def prep(gi, carry):
        cs = [gi * grp + j for j in range(grp)]
        cds = [(j, d) for j in range(grp) for d in range(2)]
        rows = [pl.ds(pl.multiple_of(c * L, L), L) for c in cs]
        srows = [pl.ds(pl.multiple_of(c * 2 * L, 2 * L), 2 * L) for c in cs]
        vv2 = [two(v_scr[r, :]) for r in rows]
        wl = {(j, d): wl_scr[d, rows[j], :] for j, d in cds}
        cin = {jd: _dot01_l(_tri_masks(jd[1] == 1)[0].astype(F32), wl[jd]) for jd in cds}
        decay = {jd: jnp.exp(_dot01_tn(wl[jd], ones_l)) for jd in cds}
        ops = {}
        for j, d in cds:
            e_in, e_out = jnp.exp(cin[j, d]), jnp.exp(-cin[j, d])
            ops[j, d] = (two(aa_scr[rows[j], :] * jnp.exp(cin[j, d] - wl[j, d])), two(r_scr[rows[j], :] * e_in),
                         two(bb_scr[d, rows[j], :] * e_out), two(kd_scr[d, rows[j], :] * e_out))
        prod = {jd: mm_nt(jnp.concatenate(ops[jd][0:2], axis=0), jnp.concatenate(ops[jd][2:4], axis=0)) for jd in cds}
        blk = {}
        for j, d in cds:
            incl_bd, strict_bd = bd_masks[d]
            p = prod[j, d]
            blk[j, d] = (jnp.where(strict_bd, p[0:2 * L, 0:2 * L], 0.0), jnp.where(strict_bd, p[0:2 * L, 2 * L:], 0.0),
                         jnp.where(incl_bd, p[2 * L:, 0:2 * L], 0.0), jnp.where(incl_bd, p[2 * L:, 2 * L:], 0.0))
        packs = [-jnp.concatenate([blk[j, d][0][0:L, :] + blk[j, d][0][L:, :] for d in range(2)], axis=1)
                 for j in range(grp)]
        inv = _inv_unit_packed_multi(packs)
        t_inv = {(j, d): _to_bd(inv[j][:, d * 2 * L:(d + 1) * 2 * L]) for j, d in cds}
        kv = {(j, d): mm(jnp.concatenate(blk[j, d][1::2], axis=0), vv2[j]) for j, d in cds}
        m1c1 = {jd: mm(t_inv[jd], jnp.concatenate([ops[jd][0], kv[jd][0:2 * L, :]], axis=1)) for jd in cds}
        rm = {jd: mm(blk[jd][2], m1c1[jd]) for jd in cds}
        bm = {jd: mm_tn(ops[jd][2], m1c1[jd]) for jd in cds}
        kv2 = {(j, d): mm_tn(ops[j, d][3], vv2[j]) for j, d in cds}
        for j, d in cds:
            m2ah = jnp.concatenate([ops[j, d][1] + rm[j, d][:, 0:128], decay[j, d] * bm[j, d][:, 0:128]], axis=0)
            hi, lo = _split2(m2ah)
            mrows = pl.ds(pl.multiple_of(cs[j] * 4 * L, 4 * L), 4 * L)
            mah_scr[d, mrows, :] = hi
            mal_scr[d, mrows, :] = lo
            c2_scr[d, srows[j], :] = rm[j, d][:, 128:] + kv[j, d][2 * L:, :]
            bc_scr[d, srows[j], :] = decay[j, d] * (bm[j, d][:, 128:] + kv2[j, d])
            dg_scr[d, srows[j], :] = decay[j, d]
        return carry

    lax.fori_loop(0, nc // grp, prep, 0)

    chains = [(s, d) for s in range(nsub) for d in range(2)]
    for s, d in chains:
        if has_state:
            stacked = s0_ref[s, d].reshape(2 * RW_K, RW_K)
            state_scr[2 * s + d] = jnp.where(same_head, _dot01_r(stacked, dup), 0.0).T
        else:
            state_scr[2 * s + d] = jnp.zeros((128, 128), F32)

    def scan(i, carry):
        for s, d in chains:
            c = s * nc_seq + ((nc_seq - 1 - i) if d else i)
            rows = pl.ds(pl.multiple_of(c * L, L), L)
            srows = pl.ds(pl.multiple_of(c * 2 * L, 2 * L), 2 * L)
            h_st = state_scr[2 * s + d]
            mrows = pl.ds(pl.multiple_of(c * 4 * L, 4 * L), 4 * L)
            lh, ll = mah_scr[d, mrows, :], mal_scr[d, mrows, :]
            hh, hl = _split2(h_st)
            res = _dot(lh, hh)
            corr = res[2 * L:, :] + _dot(lh[2 * L:, :], hl) + _dot(ll[2 * L:, :], hh)
            y2 = res[0:2 * L, :] + c2_scr[d, srows, :]
            yacc_scr[d, rows, :] = y2[0:L, :] + y2[L:, :]
            state_scr[2 * s + d] = dg_scr[d, srows, :] * h_st + corr + bc_scr[d, srows, :]
        return carry

    lax.fori_loop(0, nc_seq, scan, 0)
    if not has_state:
        dup_t = ((_iota((128, RW_K), 0) & (RW_K - 1)) == _iota((128, RW_K), 1)).astype(F32)
        for s, d in chains:
            st_ref[s, d] = _dot01_r(state_scr[2 * s + d].T, dup_t).reshape(2, RW_K, RW_K)

    y = yacc_scr[0] + yacc_scr[1]
    mean = head_sum(y) * (1.0 / RW_K)
    yc = y - mean
    var = head_sum(yc * yc) * (1.0 / RW_K)
    yn = yc * lax.rsqrt(var + RWKV_LN_EPS) * lng_ref[...] + lnb_ref[...]
    bonus = head_sum(r * (0.5 * kd_sum) * rk_ref[...]) * v
    o_ref[...] = ((yn + bonus) * gate).astype(BF16)


def _rwkv(proj, nseq, t, prm, state, nsub=1):
    has_state = state is not None
    mu, w0, wup, a0, aup, gup, k_k, k_a, r_k, ln_g, ln_b = prm
    seq = lambda base: (lambda b, hp: (b, base // 128 + hp))
    fix = lambda base: (lambda b, hp: (b, base // 128))
    mu_at = lambda base: (lambda b, hp: (0, (base - OD_R) // 128 + hp))
    mu_fix = lambda base: (lambda b, hp: (0, (base - OD_R) // 128))
    col = lambda b, hp: (0, hp)
    rb = nsub * t
    in_specs = [
        pl.BlockSpec((rb, 128), seq(OD_R)), pl.BlockSpec((rb, 128), seq(OD_RK)), pl.BlockSpec((rb, 128), seq(OD_RV)),
        pl.BlockSpec((rb, 128), fix(OD_WD)), pl.BlockSpec((rb, 128), fix(OD_AD)), pl.BlockSpec((rb, 128), fix(OD_GD)),
        pl.BlockSpec((1, 128), mu_at(OD_R)), pl.BlockSpec((1, 128), mu_at(OD_RK)), pl.BlockSpec((1, 128), mu_at(OD_RV)),
        pl.BlockSpec((1, 128), mu_fix(OD_WD)), pl.BlockSpec((1, 128), mu_fix(OD_AD)), pl.BlockSpec((1, 128), mu_fix(OD_GD)),
        pl.BlockSpec((2, 128), col), pl.BlockSpec((128, 128), col),
        pl.BlockSpec((2, 128), col), pl.BlockSpec((128, 128), col),
        pl.BlockSpec((128, 128), col),
        pl.BlockSpec((1, 128), col), pl.BlockSpec((1, 128), col), pl.BlockSpec((1, 128), col),
        pl.BlockSpec((1, 128), col), pl.BlockSpec((1, 128), col),
    ]
    args = [proj] * 6 + [mu] * 6 + [w0, wup, a0, aup, gup, k_k, k_a, r_k, ln_g, ln_b]
    st_block = (nsub, 2, 2, RW_K, RW_K)
    st_map = lambda b, hp: (b, 0, hp, 0, 0)
    out_specs = [pl.BlockSpec((rb, 128), lambda b, hp: (b, hp))]
    out_shape = [jax.ShapeDtypeStruct((nseq * t, RW_HEADS * RW_K), BF16)]
    if has_state:
        in_specs.append(pl.BlockSpec(st_block, st_map))
        args.append(state)
    else:
        out_specs.append(pl.BlockSpec(st_block, st_map))
        out_shape.append(jax.ShapeDtypeStruct((nseq, 2, RW_HEADS, RW_K, RW_K), F32))
    scratch = [pltpu.VMEM((t + 2 * CONV_PAD, 128), F32), pltpu.VMEM((rb, 128), F32), pltpu.VMEM((rb, 128), F32),
               pltpu.VMEM((rb, 128), F32), pltpu.VMEM((2, rb, 128), F32), pltpu.VMEM((2, rb, 128), F32),
               pltpu.VMEM((2, rb, 128), F32)] + [pltpu.VMEM((2, 4 * rb, 128), BF16)] * 2 + [
               pltpu.VMEM((2, 2 * rb, 128), F32)] * 3 + [
               pltpu.VMEM((2, rb, 128), F32), pltpu.VMEM((2 * nsub, 128, 128), F32)]
    return pl.pallas_call(
        functools.partial(_rwkv_kernel, t=t, nsub=nsub, has_state=has_state),
        grid=(nseq // nsub, RW_HEADS // 2), in_specs=in_specs, out_specs=out_specs, out_shape=out_shape,
        scratch_shapes=scratch, compiler_params=_cparams(("arbitrary", "arbitrary")), name="rwkv",
    )(*args)


def _even_w_in(w):
    za, xs, bm, cm = w[:, 0:1024], w[:, 1024:2048], w[:, 2048:2304], w[:, 2304:2560]
    dt, q, kk, v = w[:, 2560:2592], w[:, 2592:3616], w[:, 3616:4640], w[:, 4640:5664]
    zb, al, be = w[:, 5664:6688], w[:, 6688:6704], w[:, 6704:6720]
    pad = jnp.zeros((w.shape[0], EV_COLS - EV_SM - 64), w.dtype)
    return jnp.concatenate([za, xs, zb, q, kk, v, bm, cm, dt, al, be, pad], axis=1)


def _pad_row(parts):
    flat = jnp.concatenate([p.reshape(-1) for p in parts])
    return jnp.pad(flat, (0, 128 - flat.shape[0])).reshape(1, 128)


def kernel(x_prompt, x_sample, state_ssd, state_gdn, cache_k, cache_v, state_rwkv, c, c_ctx, ada_w, ada_b, norm1_g, norm2_g, ffn_w1, ffn_w2, final_g, ev_w_in, ev_w_out, ssd_conv_w, ssd_conv_b, ssd_a_log, ssd_dt_bias, ssd_d, ssd_norm_g, gdn_conv_w, gdn_a_log, gdn_dt_bias, gdn_norm_g, od_w_in, od_w_out, attn_sink, rwkv_mu, rwkv_w0, rwkv_w_up, rwkv_a0, rwkv_a_up, rwkv_g_up, rwkv_k_k, rwkv_k_a, rwkv_r_k, rwkv_ln_g, rwkv_ln_b):
    bp, tp, _ = x_prompt.shape
    bs, ts, _ = x_sample.shape
    streams = [(x_prompt.reshape(bp * tp, D), bp, tp, None), (x_sample.reshape(bs * ts, D), bs, ts, ts)]

    cvec = jnp.concatenate([c_ctx[None, :], c, jnp.zeros((16 - 1 - bs, D), F32)], axis=0)
    mod = _adaln(cvec, ada_w, ada_b)

    w_in0 = _even_w_in(ev_w_in[0]).astype(BF16)
    w_out0 = ev_w_out[0].astype(BF16)
    bias_row = _pad_row([ssd_dt_bias[0], gdn_dt_bias[0]])
    alog_row = _pad_row([ssd_a_log[0], gdn_a_log[0]])
    dskip_row = jnp.repeat(ssd_d[0], SSD_P).reshape(1, SSD_HEADS * SSD_P)
    w1_0, w2_0 = ffn_w1[0].astype(BF16), ffn_w2[0].astype(BF16)
    w_in1 = od_w_in[0].astype(BF16)
    w_out1 = od_w_out[0].astype(BF16)
    w1_1, w2_1 = ffn_w1[1].astype(BF16), ffn_w2[1].astype(BF16)
    rw_prm = (rwkv_mu[0].reshape(1, -1), rwkv_w0[0], rwkv_w_up[0].reshape(2 * 64, -1), rwkv_a0[0],
              rwkv_a_up[0].reshape(2 * 64, -1), rwkv_g_up[0], rwkv_k_k[0].reshape(1, -1), rwkv_k_a[0].reshape(1, -1),
              rwkv_r_k[0].reshape(1, -1), rwkv_ln_g[0].reshape(1, -1), rwkv_ln_b[0].reshape(1, -1))
    cos2, sin2 = _rope_tables(ts)
    past = cache_k.shape[2]
    ck = cache_k[:, 0].reshape(bs * past, ATT_KVH * ATT_DH)
    cv = cache_v[:, 0].reshape(bs * past, ATT_KVH * ATT_DH)

    outs = []
    for x, nseq, t, sample_t in streams:
        latent = sample_t is not None
        m0 = mod[0]
        proj = _inproj(x, norm1_g[0:1], m0, w_in0, sample_t)
        s_ssd = state_ssd[:, 0] if latent else None
        s_gdn = state_gdn[:, 0] if latent else None
        r_ssd = _ssd(proj, nseq, t, ssd_conv_w[0], ssd_conv_b[0].reshape(1, -1), bias_row, alog_row, dskip_row,
                     ssd_norm_g[0].reshape(1, -1), s_ssd)
        r_gdn = _gdn(proj, nseq, t, gdn_conv_w[0], bias_row, alog_row, gdn_norm_g[0].reshape(1, -1), s_gdn,
                     nsub=1 if latent else GDN_PROMPT_NSUB)
        x = _mix_ffn(r_ssd[0], r_gdn[0], x, norm2_g[0:1], m0, w_out0, w1_0, w2_0, final_g.reshape(1, D), sample_t,
                     final=False)
        m1 = mod[1]
        proj = _inproj(x, norm1_g[1:2], m1, w_in1, sample_t)
        if latent:
            att = _lat_attn(proj, nseq, t, attn_sink[0], ck, cv, cos2, sin2)
            r_rw = _rwkv(proj, nseq, t, rw_prm, state_rwkv[:, 0])
        else:
            att = _ctx_attn(proj, nseq, t, attn_sink[0])
            r_rw = _rwkv(proj, nseq, t, rw_prm, None, nsub=RW_PROMPT_NSUB)
        x = _mix_ffn(att, r_rw[0], x, norm2_g[1:2], m1, w_out1, w1_1, w2_1, final_g.reshape(1, D), sample_t,
                     final=True)
        outs.append((x.reshape(nseq, t, D), r_ssd, r_gdn, proj, r_rw))

    (y_prompt, p_ssd, p_gdn, p_proj, p_rw), (y_sample, _, _, _, _) = outs
    new_k = p_proj[:, OD_K:OD_V].reshape(bp, 1, tp, ATT_KVH, ATT_DH)
    new_v = p_proj[:, OD_V:OD_R].reshape(bp, 1, tp, ATT_KVH, ATT_DH)
    return (y_prompt, y_sample, p_ssd[1][:, None], p_gdn[1][:, None], new_k, new_v, p_rw[1][:, None])
```

```python
import functools

import numpy as np
import jax
import jax.numpy as jnp
from jax import lax
from jax.experimental import pallas as pl
from jax.experimental.pallas import tpu as pltpu

F32 = jnp.float32
BF16 = jnp.bfloat16
HI = lax.Precision.HIGHEST

D = 1024
NORM_EPS = 1e-6
L = 64
GDN_GROUP, RW_GROUP = 8, 8
GDN_PROMPT_NSUB, RW_PROMPT_NSUB = 4, 2
CONV_W = 5
CONV_PAD = 8

SSD_HEADS, SSD_P, SSD_N, SSD_GROUPS = 16, 64, 128, 2
SSD_GW = SSD_HEADS // SSD_GROUPS * SSD_P
GDN_HEADS, GDN_K = 8, 128
ATT_HEADS, ATT_KVH, ATT_DH, ATT_GROUP = 8, 2, 128, 4
WINDOW, QBLOCK, GRID_W, ROPE_BASE = 128, 128, 64, 10000.0
RW_HEADS, RW_K = 16, 64
RWKV_LN_EPS = 64e-5
FFN_H = 2816
FFN_TN = 256

EV_ZA, EV_XS, EV_ZB, EV_Q, EV_K, EV_V, EV_B, EV_C, EV_SM, EV_COLS = 0, 1024, 2048, 3072, 4096, 5120, 6144, 6400, 6656, 6784
SM_DT, SM_ALPHA, SM_BETA = 0, 32, 48
OD_Q, OD_K, OD_V, OD_R, OD_RK, OD_RV, OD_WD, OD_AD, OD_GD, OD_COLS = 0, 1024, 1280, 1536, 2560, 3584, 4608, 4736, 4864, 4992

VMEM_LIMIT = 56 * 1024 * 1024


def _cparams(sem):
    return pltpu.CompilerParams(dimension_semantics=sem, vmem_limit_bytes=VMEM_LIMIT)


def _dot(a, b, prec=None):
    return jnp.dot(a, b, preferred_element_type=F32, precision=prec)


def _dot_nt(a, b, prec=None):
    return lax.dot_general(a, b, (((1,), (1,)), ((), ())), preferred_element_type=F32, precision=prec)


def _dot_tn(a, b, prec=None):
    return lax.dot_general(a, b, (((0,), (0,)), ((), ())), preferred_element_type=F32, precision=prec)


def _bdot(a, b):
    return _dot(a.astype(BF16), b.astype(BF16))


def _bdot_nt(a, b):
    return _dot_nt(a.astype(BF16), b.astype(BF16))


def _bdot_tn(a, b):
    return _dot_tn(a.astype(BF16), b.astype(BF16))


def _silu(x):
    return x * jax.nn.sigmoid(x)


def _softplus(x):
    return jnp.maximum(x, 0.0) + jnp.log(1.0 + jnp.exp(-jnp.abs(x)))


def _iota(shape, dim):
    return lax.broadcasted_iota(jnp.int32, shape, dim)


def _tri_masks(rev):
    r, c = _iota((L, L), 0), _iota((L, L), 1)
    if rev:
        return c >= r, c > r
    return c <= r, c < r


def _inv_unit(a):
    eye = (_iota(a.shape, 0) == _iota(a.shape, 1)).astype(F32)
    x = eye - a
    p = _dot(a, a, HI)
    n = 2
    while True:
        x = x + _dot(x, p, HI)
        n *= 2
        if n >= L:
            break
        p = _dot(p, p, HI)
    return x


def _split3(x):
    hi = x.astype(BF16)
    r1 = x - hi.astype(F32)
    mid = r1.astype(BF16)
    lo = (r1 - mid.astype(F32)).astype(BF16)
    return hi, mid, lo


def _dot01_l(m01, x):
    mb = m01.astype(BF16)
    return _dot(jnp.concatenate([mb, mb, mb], axis=1), jnp.concatenate(_split3(x), axis=0))


def _dot01_r(x, m01):
    mb = m01.astype(BF16)
    return _dot(jnp.concatenate(_split3(x), axis=1), jnp.concatenate([mb, mb, mb], axis=0))


def _dot01_tn(x, m01):
    mb = m01.astype(BF16)
    return _dot_tn(jnp.concatenate(_split3(x), axis=0), jnp.concatenate([mb, mb, mb], axis=0))


def _dot3(a, b):
    ah = a.astype(BF16)
    al = (a - ah.astype(F32)).astype(BF16)
    bh = b.astype(BF16)
    bl = (b - bh.astype(F32)).astype(BF16)
    return _dot(ah, bh) + _dot(ah, bl) + _dot(al, bh)


def _to_bd(m):
    n = m.shape[1] // L
    blk = jnp.right_shift(_iota(m.shape, 1), L.bit_length() - 1)
    return jnp.concatenate([jnp.where(blk == j, m, 0.0) for j in range(n)], axis=0)


def _split2(x):
    hi = x.astype(BF16)
    return hi, (x - hi.astype(F32)).astype(BF16)


def _inv_unit_packed_multi(mats):
    eye = ((_iota(mats[0].shape, 1) & (L - 1)) == _iota(mats[0].shape, 0)).astype(F32)

    def mm3(lh, ll, ph, pl_):
        bh, bl = _to_bd(ph), _to_bd(pl_)
        return _dot(lh, bh) + _dot(lh, bl) + _dot(ll, bh)

    xs = [eye - a for a in mats]
    ps = [mm3(h, lo, h, lo) for h, lo in map(_split2, mats)]
    n = 2
    while True:
        sp = [_split2(p) for p in ps]
        sx = [_split2(x) for x in xs]
        n *= 2
        if n >= L:
            return [x + mm3(xh, xl, ph, pl_) for x, (xh, xl), (ph, pl_) in zip(xs, sx, sp)]
        res = [mm3(jnp.concatenate([xh, ph], axis=0), jnp.concatenate([xl, pl_], axis=0), ph, pl_)
               for (xh, xl), (ph, pl_) in zip(sx, sp)]
        xs = [x + r[0:L, :] for x, r in zip(xs, res)]
        ps = [r[L:, :] for r in res]


def _conv_silu(x, w_ref, b, pad_ref, t):
    c = x.shape[1]
    zeros = jnp.zeros((CONV_PAD, c), F32)
    pad_ref[0:CONV_PAD, 0:c] = zeros
    pad_ref[CONV_PAD + t:2 * CONV_PAD + t, 0:c] = zeros
    pad_ref[CONV_PAD:CONV_PAD + t, 0:c] = x
    acc = None
    for j in range(CONV_W):
        off = CONV_PAD - CONV_W // 2 + j
        term = pad_ref[off:off + t, 0:c] * w_ref[j:j + 1, :]
        acc = term if acc is None else acc + term
    if b is not None:
        acc = acc + b
    return _silu(acc)


def _adaln_kernel(c_ref, w_ref, b_ref, o_ref):
    c = c_ref[...]
    o_ref[0] = _bdot(_silu(c), w_ref[0]) + b_ref[0]


def _adaln(cvec, ada_w, ada_b):
    depth, _, n = ada_w.shape
    tn = 1536
    out = pl.pallas_call(
        _adaln_kernel,
        grid=(depth, n // tn),
        in_specs=[pl.BlockSpec((16, D), lambda l, j: (0, 0)),
                  pl.BlockSpec((1, D, tn), lambda l, j: (l, 0, j)),
                  pl.BlockSpec((1, 1, tn), lambda l, j: (l, 0, j))],
        out_specs=pl.BlockSpec((1, 16, tn), lambda l, j: (l, 0, j)),
        out_shape=jax.ShapeDtypeStruct((depth, 16, n), F32),
        compiler_params=_cparams(("arbitrary", "arbitrary")),
        name="adaln",
    )(cvec, ada_w, ada_b.reshape(depth, 1, n))
    return out.reshape(depth, 16, 6, D)


def _mod_row_map(tm, t):
    if t is None:
        return lambda i: (0, 0, 0)
    per = t // tm
    return lambda i: (1 + i // per, 0, 0)


def _norm_mod(x, g, m, shift_idx, scale_idx):
    var = jnp.mean(x * x, axis=-1, keepdims=True)
    y = x * lax.rsqrt(var + NORM_EPS) * g
    return y * (1.0 + m[scale_idx:scale_idx + 1]) + m[shift_idx:shift_idx + 1]


def _inproj_kernel(x_ref, g_ref, mod_ref, w_ref, o_ref, *, tn):
    h = _norm_mod(x_ref[...], g_ref[...], mod_ref[0], 0, 1).astype(BF16)
    n = w_ref.shape[1]
    for j in range(0, n, tn):
        w = min(tn, n - j)
        o_ref[:, j:j + w] = _dot(h, w_ref[:, j:j + w])


def _inproj(x, g, mod, w, sample_t, tm=256, tn=512):
    rows, n = x.shape[0], w.shape[1]
    return pl.pallas_call(
        functools.partial(_inproj_kernel, tn=tn),
        grid=(rows // tm,),
        in_specs=[pl.BlockSpec((tm, D), lambda i: (i, 0)),
                  pl.BlockSpec((1, D), lambda i: (0, 0)),
                  pl.BlockSpec((1, 6, D), _mod_row_map(tm, sample_t)),
                  pl.BlockSpec((D, n), lambda i: (0, 0))],
        out_specs=pl.BlockSpec((tm, n), lambda i: (i, 0)),
        out_shape=jax.ShapeDtypeStruct((rows, n), F32),
        compiler_params=_cparams(("arbitrary",)),
        name="inproj",
    )(x, g, mod, w)


def _mix_ffn_kernel(a1_ref, a2_ref, x_ref, g_ref, mod_ref, wo_ref, w1_ref, w2_ref, fg_ref, o_ref, *, final):
    m = mod_ref[0]
    k1 = a1_ref.shape[1]
    mix = _dot(a1_ref[...], wo_ref[0:k1, :]) + _dot(a2_ref[...], wo_ref[k1:, :])
    x = x_ref[...] + m[2:3] * mix
    h = _norm_mod(x, g_ref[...], m, 3, 4).astype(BF16)
    acc = jnp.zeros(x.shape, F32)
    for j in range(0, FFN_H, FFN_TN):
        gate = _dot(h, w1_ref[:, j:j + FFN_TN])
        up = _dot(h, w1_ref[:, FFN_H + j:FFN_H + j + FFN_TN])
        acc = acc + _dot((_silu(gate) * up).astype(BF16), w2_ref[j:j + FFN_TN, :])
    y = x + m[5:6] * acc
    if final:
        var = jnp.mean(y * y, axis=-1, keepdims=True)
        y = y * lax.rsqrt(var + NORM_EPS) * fg_ref[...]
    o_ref[...] = y


def _mix_ffn(a1, a2, x, g, mod, w_out, w1, w2, final_g, sample_t, final, tm=256):
    rows = x.shape[0]
    once = dict(pipeline_mode=pl.Buffered(1))
    return pl.pallas_call(
        functools.partial(_mix_ffn_kernel, final=final),
        grid=(rows // tm,),
        in_specs=[pl.BlockSpec((tm, a1.shape[1]), lambda i: (i, 0)),
                  pl.BlockSpec((tm, a2.shape[1]), lambda i: (i, 0)),
                  pl.BlockSpec((tm, D), lambda i: (i, 0)),
                  pl.BlockSpec((1, D), lambda i: (0, 0)),
                  pl.BlockSpec((1, 6, D), _mod_row_map(tm, sample_t)),
                  pl.BlockSpec(w_out.shape, lambda i: (0, 0), **once),
                  pl.BlockSpec(w1.shape, lambda i: (0, 0), **once),
                  pl.BlockSpec(w2.shape, lambda i: (0, 0), **once),
                  pl.BlockSpec((1, D), lambda i: (0, 0))],
        out_specs=pl.BlockSpec((tm, D), lambda i: (i, 0)),
        out_shape=jax.ShapeDtypeStruct((rows, D), F32),
        compiler_params=_cparams(("arbitrary",)),
        name="mix_ffn",
    )(a1, a2, x, g, mod, w_out, w1, w2, final_g)


def _ssd_kernel(*refs, t, has_state):
    (za_ref, xs_ref, bm_ref, cm_ref, sm_ref, cwx_ref, cwb_ref, cwc_ref, cbx_ref, cbb_ref, cbc_ref,
     bias_ref, alog_ref, dsk_ref, ng_ref) = refs[:15]
    k = 15
    s0_ref = st_ref = None
    if has_state:
        s0_ref = refs[k]
        k += 1
    y_ref = refs[k]
    k += 1
    if not has_state:
        st_ref = refs[k]
        k += 1
    pad_scr, xs_scr, bb_scr, cc_scr, dt_scr, da_scr, dasm_scr, yacc_scr, state_scr = refs[k:]

    g = pl.program_id(1)
    nc = t // L
    npair = SSD_GW // 128

    xs_scr[...] = _conv_silu(xs_ref[...], cwx_ref, cbx_ref[...], pad_scr, t)
    bb_scr[...] = _conv_silu(bm_ref[...], cwb_ref, cbb_ref[...], pad_scr, t)
    cc_scr[...] = _conv_silu(cm_ref[...], cwc_ref, cbc_ref[...], pad_scr, t)

    dtv = _softplus(sm_ref[...] + bias_ref[...])
    dav = dtv * (-jnp.exp(alog_ref[...]))
    heads_per_group = SSD_HEADS // SSD_GROUPS
    for d in range(2):
        first = SM_DT + d * SSD_HEADS + g * heads_per_group
        src = _iota((128, SSD_GW), 0)
        e = (src == first + jnp.right_shift(_iota((128, SSD_GW), 1), 6)).astype(F32)
        dt_scr[d] = _dot01_r(dtv, e)
        da_scr[d] = _dot01_r(dav, e)
        sel = (_iota((128, 128), 0) == first + _iota((128, 128), 1)) & (_iota((128, 128), 1) < heads_per_group)
        dasm_scr[d] = _dot01_r(dav, sel.astype(F32))

    lane128 = _iota((L, 128), 1)
    for d in range(2):
        if has_state:
            state_scr[d] = s0_ref[0, d].reshape(SSD_GW, SSD_N).T
        else:
            state_scr[d] = jnp.zeros((SSD_N, SSD_GW), F32)

    def body(ci, carry):
        for d in range(2):
            rev = d == 1
            incl = _tri_masks(rev)[0]
            tri = incl.astype(F32)
            tri_t = _tri_masks(not rev)[0].astype(F32)
            last = 0 if rev else L - 1
            c = (nc - 1 - ci) if rev else ci
            r0 = pl.multiple_of(c * L, L)
            rows = pl.ds(r0, L)
            x = xs_scr[rows, :]
            bc = bb_scr[rows, :]
            cc = cc_scr[rows, :]
            acum = _dot01_l(tri, da_scr[d, rows, :])
            acum_t = _dot01_tn(dasm_scr[d, rows, :], tri_t)
            xdt = x * dt_scr[d, rows, :]
            cb = _bdot_nt(cc, bc)
            total = acum[last:last + 1, :]
            s_in = state_scr[d]
            y_inter = _bdot(cc, s_in) * jnp.exp(acum)
            pieces = []
            for j in range(npair):
                xp = xdt[:, j * 128:(j + 1) * 128]
                yp = None
                for hh in range(2):
                    h = 2 * j + hh
                    col = acum[:, h * SSD_P:h * SSD_P + 1]
                    row = acum_t[h:h + 1, :]
                    dec = jnp.where(incl, jnp.exp(jnp.minimum(col - row, 0.0)), 0.0)
                    mine = (lane128 >= SSD_P) if hh else (lane128 < SSD_P)
                    term = _bdot(cb * dec, jnp.where(mine, xp, 0.0))
                    yp = term if yp is None else yp + term
                pieces.append(yp)
            yacc_scr[d, rows, :] = jnp.concatenate(pieces, axis=1) + y_inter
            wx = jnp.exp(total - acum) * xdt
            state_scr[d] = s_in * jnp.exp(total) + _bdot_tn(bc, wx)
        return carry

    lax.fori_loop(0, nc, body, 0)
    if not has_state:
        for d in range(2):
            st_ref[0, d] = state_scr[d].T.reshape(heads_per_group, SSD_P, SSD_N)

    y = yacc_scr[0] + yacc_scr[1] + dsk_ref[...] * xs_scr[...]
    y = y * _silu(za_ref[...])
    var = jnp.mean(y * y, axis=-1, keepdims=True)
    y_ref[...] = (y * lax.rsqrt(var + NORM_EPS) * ng_ref[...]).astype(BF16)


def _ssd(proj, nseq, t, conv_w, conv_b, bias_row, alog_row, dskip_row, norm_g, state):
    has_state = state is not None
    seq = lambda blk: (lambda b, g: (b, blk(g)))
    const = lambda blk: (lambda b, g: (0, blk(g)))
    in_specs = [
        pl.BlockSpec((t, SSD_GW), seq(lambda g: EV_ZA // SSD_GW + g)),
        pl.BlockSpec((t, SSD_GW), seq(lambda g: EV_XS // SSD_GW + g)),
        pl.BlockSpec((t, 128), seq(lambda g: EV_B // 128 + g)),
        pl.BlockSpec((t, 128), seq(lambda g: EV_C // 128 + g)),
        pl.BlockSpec((t, 128), seq(lambda g: EV_SM // 128)),
        pl.BlockSpec((CONV_W, SSD_GW), const(lambda g: g)),
        pl.BlockSpec((CONV_W, 128), const(lambda g: 1024 // 128 + g)),
        pl.BlockSpec((CONV_W, 128), const(lambda g: 1280 // 128 + g)),
        pl.BlockSpec((1, SSD_GW), const(lambda g: g)),
        pl.BlockSpec((1, 128), const(lambda g: 1024 // 128 + g)),
        pl.BlockSpec((1, 128), const(lambda g: 1280 // 128 + g)),
        pl.BlockSpec((1, 128), const(lambda g: 0)),
        pl.BlockSpec((1, 128), const(lambda g: 0)),
        pl.BlockSpec((1, SSD_GW), const(lambda g: g)),
        pl.BlockSpec((1, SSD_GW), const(lambda g: g)),
    ]
    args = [proj, proj, proj, proj, proj, conv_w, conv_w, conv_w, conv_b, conv_b, conv_b,
            bias_row, alog_row, dskip_row, norm_g]
    st_block = (1, 2, SSD_HEADS // SSD_GROUPS, SSD_P, SSD_N)
    st_map = lambda b, g: (b, 0, g, 0, 0)
    out_specs = [pl.BlockSpec((t, SSD_GW), lambda b, g: (b, g))]
    out_shape = [jax.ShapeDtypeStruct((nseq * t, SSD_HEADS * SSD_P), BF16)]
    if has_state:
        in_specs.append(pl.BlockSpec(st_block, st_map))
        args.append(state)
    else:
        out_specs.append(pl.BlockSpec(st_block, st_map))
        out_shape.append(jax.ShapeDtypeStruct((nseq, 2, SSD_HEADS, SSD_P, SSD_N), F32))
    scratch = [pltpu.VMEM((t + 2 * CONV_PAD, SSD_GW), F32), pltpu.VMEM((t, SSD_GW), F32),
               pltpu.VMEM((t, 128), F32), pltpu.VMEM((t, 128), F32),
               pltpu.VMEM((2, t, SSD_GW), F32), pltpu.VMEM((2, t, SSD_GW), F32), pltpu.VMEM((2, t, 128), F32),
               pltpu.VMEM((2, t, SSD_GW), F32), pltpu.VMEM((2, SSD_N, SSD_GW), F32)]
    return pl.pallas_call(
        functools.partial(_ssd_kernel, t=t, has_state=has_state),
        grid=(nseq, SSD_GROUPS), in_specs=in_specs, out_specs=out_specs, out_shape=out_shape,
        scratch_shapes=scratch, compiler_params=_cparams(("arbitrary", "arbitrary")), name="ssd",
    )(*args)


def _gdn_kernel(*refs, t, nsub, has_state):
    (q_ref, k_ref, v_ref, zb_ref, sm_ref, cwq_ref, cwk_ref, cwv_ref, bias_ref, alog_ref, ng_ref) = refs[:11]
    k = 11
    s0_ref = st_ref = None
    if has_state:
        s0_ref = refs[k]
        k += 1
    o_ref = refs[k]
    k += 1
    if not has_state:
        st_ref = refs[k]
        k += 1
    (pad_scr, q_scr, k_scr, v_scr, g_scr, b_scr, ma_scr, c2_scr, bc_scr, dg_scr,
     oacc_scr, state_scr) = refs[k:]

    h = pl.program_id(1)
    nc_seq = t // L
    nc = nsub * nc_seq
    rows_all = nsub * t

    def l2n(x):
        return x * lax.rsqrt(jnp.sum(x * x, axis=-1, keepdims=True) + NORM_EPS)

    for s in range(nsub):
        rs = slice(s * t, (s + 1) * t)
        q_scr[rs, :] = l2n(_conv_silu(q_ref[rs, :], cwq_ref, None, pad_scr, t)) * (GDN_K ** -0.5)
        k_scr[rs, :] = l2n(_conv_silu(k_ref[rs, :], cwk_ref, None, pad_scr, t))
        v_scr[rs, :] = _conv_silu(v_ref[rs, :], cwv_ref, None, pad_scr, t)

    sm = sm_ref[...]
    gl = -jnp.exp(alog_ref[...]) * _softplus(sm + bias_ref[...])
    bt = jax.nn.sigmoid(sm)
    lane = _iota((rows_all, 128), 1)
    for d in range(2):
        gcol = jnp.sum(jnp.where(lane == SM_ALPHA + d * GDN_HEADS + h, gl, 0.0), axis=1, keepdims=True)
        bcol = jnp.sum(jnp.where(lane == SM_BETA + d * GDN_HEADS + h, bt, 0.0), axis=1, keepdims=True)
        g_scr[d] = jnp.broadcast_to(gcol, (rows_all, 128))
        b_scr[d] = jnp.broadcast_to(bcol, (rows_all, 128))

    masks = [_tri_masks(False), _tri_masks(True)]
    grp = min(GDN_GROUP, nc)

    def prep(gi, carry):
        cs = [gi * grp + j for j in range(grp)]
        rows = [pl.ds(pl.multiple_of(c * L, L), L) for c in cs]
        qkv = [(q_scr[r, :], k_scr[r, :], v_scr[r, :]) for r in rows]
        qkk = [_bdot_nt(jnp.concatenate([qc, kc], axis=0), kc) for qc, kc, _ in qkv]
        gbs = [[(g_scr[d, r, :], b_scr[d, r, :]) for d in range(2)] for r in rows]
        gam = [[_dot01_l(masks[d][0].astype(F32), gbs[j][d][0]) for d in range(2)] for j in range(grp)]
        gam_row = [[_dot01_tn(gbs[j][d][0], masks[1 - d][0].astype(F32))[0:1, :] for d in range(2)]
                   for j in range(grp)]
        per = []
        for j in range(grp):
            qc, kc, vc = qkv[j]
            row = []
            for d in range(2):
                incl, strict = masks[d]
                bb = gbs[j][d][1]
                g = gam[j][d]
                dec = jnp.where(incl, jnp.exp(jnp.minimum(g[:, 0:1] - gam_row[j][d], 0.0)), 0.0)
                a_mat = jnp.where(strict, bb[:, 0:1] * qkk[j][L:, :] * dec, 0.0)
                eg = jnp.exp(g)
                last = 0 if d else L - 1
                g_last = g[last:last + 1, :]
                rhs = jnp.concatenate([bb * vc, bb * kc * eg], axis=1)
                row.append((a_mat, rhs, qkk[j][0:L, :] * dec, qc * eg, kc * jnp.exp(g_last - g), g_last))
            per.append(row)
        packs = [jnp.concatenate([per[j][0][0], per[j][1][0], per[j + 1][0][0], per[j + 1][1][0]], axis=1)
                 for j in range(0, grp, 2)]
        inv = _inv_unit_packed_multi(packs)
        t_inv = [_to_bd(inv[j // 2][:, (j % 2) * 2 * L:(j % 2 + 1) * 2 * L]) for j in range(grp)]
        uw = [_bdot(t_inv[j], jnp.concatenate([per[j][0][1], per[j][1][1]], axis=0))
              for j in range(grp)]
        qk_uw = [_bdot(_to_bd(jnp.concatenate([per[j][0][2], per[j][1][2]], axis=1)), uw[j])
                 for j in range(grp)]
        ke_uw = [[_bdot_tn(per[j][d][4], uw[j][d * L:(d + 1) * L, :]) for d in range(2)]
                 for j in range(grp)]
        for j, c in enumerate(cs):
            srows = pl.ds(pl.multiple_of(c * GDN_K, GDN_K), GDN_K)
            drows = pl.ds(pl.multiple_of(c * 8, 8), 8)
            for d in range(2):
                half = slice(d * L, (d + 1) * L)
                mrows = pl.ds(pl.multiple_of(c * (L + GDN_K), L + GDN_K), L + GDN_K)
                ma_scr[d, mrows, :] = jnp.concatenate(
                    [per[j][d][3] - qk_uw[j][half, GDN_K:], -ke_uw[j][d][:, GDN_K:]], axis=0).astype(BF16)
                c2_scr[d, rows[j], :] = qk_uw[j][half, 0:GDN_K]
                bc_scr[d, srows, :] = ke_uw[j][d][:, 0:GDN_K]
                dg_scr[d, drows, :] = jnp.broadcast_to(jnp.exp(per[j][d][5]), (8, 128))
        return carry

    lax.fori_loop(0, nc // grp, prep, 0)

    chains = [(s, d) for s in range(nsub) for d in range(2)]
    for s, d in chains:
        state_scr[2 * s + d] = s0_ref[s, d, 0] if has_state else jnp.zeros((GDN_K, GDN_K), F32)

    def scan(i, carry):
        for s, d in chains:
            c = s * nc_seq + ((nc_seq - 1 - i) if d else i)
            rows = pl.ds(pl.multiple_of(c * L, L), L)
            srows = pl.ds(pl.multiple_of(c * GDN_K, GDN_K), GDN_K)
            drows = pl.ds(pl.multiple_of(c * 8, 8), 8)
            st = state_scr[2 * s + d]
            mrows = pl.ds(pl.multiple_of(c * (L + GDN_K), L + GDN_K), L + GDN_K)
            res = _dot(ma_scr[d, mrows, :], st.astype(BF16))
            oacc_scr[d, rows, :] = res[0:L, :] + c2_scr[d, rows, :]
            state_scr[2 * s + d] = dg_scr[d, drows, :][0:1, :] * st + res[L:, :] + bc_scr[d, srows, :]
        return carry

    lax.fori_loop(0, nc_seq, scan, 0)
    if not has_state:
        for s, d in chains:
            st_ref[s, d, 0] = state_scr[2 * s + d]

    o = oacc_scr[0] + oacc_scr[1]
    var = jnp.mean(o * o, axis=-1, keepdims=True)
    o_ref[...] = (o * lax.rsqrt(var + NORM_EPS) * ng_ref[...] * _silu(zb_ref[...])).astype(BF16)


def _gdn(proj, nseq, t, conv_w, bias_row, alog_row, norm_g, state, nsub=1):
    has_state = state is not None
    seq = lambda base: (lambda b, h: (b, base // 128 + h))
    const = lambda base: (lambda b, h: (0, base // 128 + h))
    rb = nsub * t
    in_specs = [
        pl.BlockSpec((rb, 128), seq(EV_Q)), pl.BlockSpec((rb, 128), seq(EV_K)), pl.BlockSpec((rb, 128), seq(EV_V)),
        pl.BlockSpec((rb, 128), seq(EV_ZB)), pl.BlockSpec((rb, 128), lambda b, h: (b, EV_SM // 128)),
        pl.BlockSpec((CONV_W, 128), const(0)), pl.BlockSpec((CONV_W, 128), const(1024)),
        pl.BlockSpec((CONV_W, 128), const(2048)),
        pl.BlockSpec((1, 128), lambda b, h: (0, 0)), pl.BlockSpec((1, 128), lambda b, h: (0, 0)),
        pl.BlockSpec((1, 128), lambda b, h: (0, 0)),
    ]
    args = [proj, proj, proj, proj, proj, conv_w, conv_w, conv_w, bias_row, alog_row, norm_g]
    st_block = (nsub, 2, 1, GDN_K, GDN_K)
    st_map = lambda b, h: (b, 0, h, 0, 0)
    out_specs = [pl.BlockSpec((rb, 128), lambda b, h: (b, h))]
    out_shape = [jax.ShapeDtypeStruct((nseq * t, GDN_HEADS * GDN_K), BF16)]
    if has_state:
        in_specs.append(pl.BlockSpec(st_block, st_map))
        args.append(state)
    else:
        out_specs.append(pl.BlockSpec(st_block, st_map))
        out_shape.append(jax.ShapeDtypeStruct((nseq, 2, GDN_HEADS, GDN_K, GDN_K), F32))
    nc = rb // L
    scratch = [pltpu.VMEM((t + 2 * CONV_PAD, 128), F32), pltpu.VMEM((rb, 128), F32), pltpu.VMEM((rb, 128), F32),
               pltpu.VMEM((rb, 128), F32), pltpu.VMEM((2, rb, 128), F32), pltpu.VMEM((2, rb, 128), F32),
               pltpu.VMEM((2, nc * (L + GDN_K), 128), BF16), pltpu.VMEM((2, rb, 128), F32),
               pltpu.VMEM((2, nc * GDN_K, GDN_K), F32),
               pltpu.VMEM((2, nc * 8, 128), F32),
               pltpu.VMEM((2, rb, 128), F32), pltpu.VMEM((2 * nsub, GDN_K, GDN_K), F32)]
    return pl.pallas_call(
        functools.partial(_gdn_kernel, t=t, nsub=nsub, has_state=has_state),
        grid=(nseq // nsub, GDN_HEADS), in_specs=in_specs, out_specs=out_specs, out_shape=out_shape,
        scratch_shapes=scratch, compiler_params=_cparams(("arbitrary", "arbitrary")), name="gdn",
    )(*args)


def _softmax_sink_pv(scores, sink, values):
    m = sink
    for s in scores:
        m = jnp.maximum(m, jnp.max(s, axis=-1, keepdims=True))
    den = jnp.exp(sink - m)
    ps = []
    for s in scores:
        p = jnp.exp(s - m)
        den = den + jnp.sum(p, axis=-1, keepdims=True)
        ps.append(p)
    out = None
    for p, v in zip(ps, values):
        term = _bdot(p / den, v)
        out = term if out is None else out + term
    return out


def _ctx_attn_kernel(sink_ref, q_ref, k_ref, v_ref, o_ref):
    kvh = pl.program_id(1)
    scale = ATT_DH ** -0.5
    kk, vv = k_ref[...], v_ref[...]
    for gq in range(ATT_GROUP):
        q = q_ref[:, gq * ATT_DH:(gq + 1) * ATT_DH]
        s = _bdot_nt(q, kk) * scale
        sink = sink_ref[kvh * ATT_GROUP + gq]
        o_ref[:, gq * ATT_DH:(gq + 1) * ATT_DH] = _softmax_sink_pv([s], sink, [vv]).astype(BF16)


def _ctx_attn(proj, nseq, t, sink):
    gw = ATT_GROUP * ATT_DH
    return pl.pallas_call(
        _ctx_attn_kernel,
        grid=(nseq, ATT_KVH),
        in_specs=[pl.BlockSpec(memory_space=pltpu.SMEM),
                  pl.BlockSpec((t, gw), lambda b, kv: (b, OD_Q // gw + kv)),
                  pl.BlockSpec((t, ATT_DH), lambda b, kv: (b, OD_K // ATT_DH + kv)),
                  pl.BlockSpec((t, ATT_DH), lambda b, kv: (b, OD_V // ATT_DH + kv))],
        out_specs=pl.BlockSpec((t, gw), lambda b, kv: (b, kv)),
        out_shape=jax.ShapeDtypeStruct((nseq * t, ATT_HEADS * ATT_DH), BF16),
        compiler_params=_cparams(("arbitrary", "arbitrary")), name="ctx_attn",
    )(sink, proj, proj, proj)


def _lat_attn_kernel(sink_ref, q_ref, k_ref, v_ref, kc_ref, vc_ref, cos_ref, sin_ref, o_ref, kp_scr, vp_scr, *, t):
    kvh = pl.program_id(1)
    scale = ATT_DH ** -0.5
    band = QBLOCK + 2 * WINDOW
    cos, sin = cos_ref[...], sin_ref[...]

    def rope(x):
        return x * cos + pltpu.roll(x, ATT_DH // 2, 1) * sin

    zeros = jnp.zeros((WINDOW, ATT_DH), F32)
    kp_scr[0:WINDOW, :] = zeros
    kp_scr[WINDOW + t:, :] = zeros
    kp_scr[WINDOW:WINDOW + t, :] = rope(k_ref[...])
    vp_scr[0:WINDOW, :] = zeros
    vp_scr[WINDOW + t:, :] = zeros
    vp_scr[WINDOW:WINDOW + t, :] = v_ref[...]
    kc, vc = kc_ref[...], vc_ref[...]

    def body(i, carry):
        start = pl.multiple_of(i * QBLOCK, QBLOCK)
        ki = kp_scr[pl.ds(start, band), :]
        vi = vp_scr[pl.ds(start, band), :]
        qpos = start + _iota((QBLOCK, band), 0)
        kpos = start - WINDOW + _iota((QBLOCK, band), 1)
        valid = (jnp.abs(qpos - kpos) <= WINDOW) & (kpos >= 0) & (kpos < t)
        cs = cos_ref[pl.ds(start, QBLOCK), :]
        sn = sin_ref[pl.ds(start, QBLOCK), :]
        for gq in range(ATT_GROUP):
            q = q_ref[pl.ds(start, QBLOCK), gq * ATT_DH:(gq + 1) * ATT_DH]
            q = q * cs + pltpu.roll(q, ATT_DH // 2, 1) * sn
            s_band = jnp.where(valid, _bdot_nt(q, ki) * scale, -jnp.inf)
            s_ctx = _bdot_nt(q, kc) * scale
            sink = sink_ref[kvh * ATT_GROUP + gq]
            o_ref[pl.ds(start, QBLOCK), gq * ATT_DH:(gq + 1) * ATT_DH] = _softmax_sink_pv(
                [s_band, s_ctx], sink, [vi, vc]).astype(BF16)
        return carry

    lax.fori_loop(0, t // QBLOCK, body, 0)


def _lat_attn(proj, nseq, t, sink, cache_k, cache_v, cos2, sin2):
    gw = ATT_GROUP * ATT_DH
    past = cache_k.shape[0] // nseq
    return pl.pallas_call(
        functools.partial(_lat_attn_kernel, t=t),
        grid=(nseq, ATT_KVH),
        in_specs=[pl.BlockSpec(memory_space=pltpu.SMEM),
                  pl.BlockSpec((t, gw), lambda b, kv: (b, OD_Q // gw + kv)),
                  pl.BlockSpec((t, ATT_DH), lambda b, kv: (b, OD_K // ATT_DH + kv)),
                  pl.BlockSpec((t, ATT_DH), lambda b, kv: (b, OD_V // ATT_DH + kv)),
                  pl.BlockSpec((past, ATT_DH), lambda b, kv: (b, kv)),
                  pl.BlockSpec((past, ATT_DH), lambda b, kv: (b, kv)),
                  pl.BlockSpec((t, ATT_DH), lambda b, kv: (0, 0)),
                  pl.BlockSpec((t, ATT_DH), lambda b, kv: (0, 0))],
        out_specs=pl.BlockSpec((t, gw), lambda b, kv: (b, kv)),
        out_shape=jax.ShapeDtypeStruct((nseq * t, ATT_HEADS * ATT_DH), BF16),
        scratch_shapes=[pltpu.VMEM((t + 2 * WINDOW, ATT_DH), F32), pltpu.VMEM((t + 2 * WINDOW, ATT_DH), F32)],
        compiler_params=_cparams(("arbitrary", "arbitrary")), name="lat_attn",
    )(sink, proj, proj, proj, cache_k, cache_v, cos2, sin2)


def _rope_tables(t):
    rows = t // GRID_W
    row = jnp.repeat(jnp.arange(rows), GRID_W).astype(F32)
    col = jnp.tile(jnp.arange(GRID_W), rows).astype(F32)
    n_freq = ATT_DH // 4
    inv = ROPE_BASE ** (-jnp.arange(n_freq, dtype=F32) / n_freq)
    ang = jnp.concatenate([row[:, None] * inv, col[:, None] * inv], axis=-1)
    cos, sin = jnp.cos(ang), jnp.sin(ang)
    return jnp.concatenate([cos, cos], axis=-1), jnp.concatenate([-sin, sin], axis=-1)


def _rwkv_kernel(*refs, t, nsub, has_state):
    (r_ref, k_ref, v_ref, wd_ref, ad_ref, gd_ref, mur_ref, muk_ref, muv_ref, muw_ref, mua_ref, mug_ref,
     w0_ref, wup_ref, a0_ref, aup_ref, gup_ref, kk_ref, ka_ref, rk_ref, lng_ref, lnb_ref) = refs[:22]
    k = 22
    s0_ref = st_ref = None
    if has_state:
        s0_ref = refs[k]
        k += 1
    o_ref = refs[k]
    k += 1
    if not has_state:
        st_ref = refs[k]
        k += 1
    (pad_scr, r_scr, v_scr, aa_scr, wl_scr, kd_scr, bb_scr, mah_scr, mal_scr, c2_scr, bc_scr, dg_scr,
     yacc_scr, state_scr) = refs[k:]
    nc_seq = t // L
    nc = nsub * nc_seq

    def shift_mix(x_ref, mu_ref):
        parts = []
        for s in range(nsub):
            x = x_ref[s * t:(s + 1) * t, :]
            zeros = jnp.zeros((CONV_PAD, 128), F32)
            pad_scr[0:CONV_PAD, :] = zeros
            pad_scr[CONV_PAD + t:, :] = zeros
            pad_scr[CONV_PAD:CONV_PAD + t, :] = x
            nb = 0.5 * (pad_scr[CONV_PAD - 1:CONV_PAD - 1 + t, :] + pad_scr[CONV_PAD + 1:CONV_PAD + 1 + t, :])
            parts.append(x + mu_ref[...] * (nb - x))
        return parts[0] if nsub == 1 else jnp.concatenate(parts, axis=0)

    r = shift_mix(r_ref, mur_ref)
    kx = shift_mix(k_ref, muk_ref)
    v = shift_mix(v_ref, muv_ref)
    wd = shift_mix(wd_ref, muw_ref)
    ad = shift_mix(ad_ref, mua_ref)
    gd = shift_mix(gd_ref, mug_ref)

    lane_sq = _iota((128, 128), 1)
    row_sq = _iota((128, 128), 0)
    same_head = ((lane_sq >= RW_K) == (row_sq >= RW_K))
    bd_ones = same_head.astype(F32)

    def head_sum(x):
        return _dot(jnp.concatenate(_split2(x), axis=1), jnp.concatenate([bd_ones, bd_ones], axis=0).astype(BF16))

    gate = _bdot(jax.nn.sigmoid(gd), gup_ref[...])
    kkv = kx * kk_ref[...]
    kkn = kkv * lax.rsqrt(head_sum(kkv * kkv) + NORM_EPS)
    tw = jnp.tanh(wd)
    rows_lo = _iota((128, 128), 0) < RW_K
    kd_sum = jnp.zeros((nsub * t, 128), F32)
    for d in range(2):
        half = rows_lo if d == 0 else jnp.logical_not(rows_lo)
        w = w0_ref[d:d + 1, :] + _bdot(tw, jnp.where(half, wup_ref[...], 0.0))
        wl_scr[d] = -jnp.exp(-_softplus(-w) - 0.5)
        a = jax.nn.sigmoid(a0_ref[d:d + 1, :] + _bdot(ad, jnp.where(half, aup_ref[...], 0.0)))
        kd = kx * (1.0 + (a - 1.0) * ka_ref[...])
        kd_scr[d] = kd
        kd_sum = kd_sum + kd
        bb_scr[d] = kkn * a
    r_scr[...] = r
    v_scr[...] = v
    aa_scr[...] = -kkn

    dup = ((_iota((RW_K, 128), 1) & (RW_K - 1)) == _iota((RW_K, 128), 0)).astype(F32)
    row_head1 = _iota((2 * L, 128), 0) >= L
    lane_head1 = _iota((2 * L, 128), 1) >= RW_K
    own_lanes = row_head1 == lane_head1
    sq_r, sq_c = _iota((2 * L, 2 * L), 0), _iota((2 * L, 2 * L), 1)
    same_blk = (sq_r >= L) == (sq_c >= L)
    pos_r, pos_c = sq_r & (L - 1), sq_c & (L - 1)
    ones_l = jnp.ones((L, 128), F32)
    bd_masks = [(same_blk & (pos_c <= pos_r), same_blk & (pos_c < pos_r)),
                (same_blk & (pos_c >= pos_r), same_blk & (pos_c > pos_r))]
    mm, mm_nt, mm_tn = _bdot, _bdot_nt, _bdot_tn
    grp = min(RW_GROUP, nc)

    def two(x):
        return jnp.where(own_lanes, jnp.concatenate([x, x], axis=0), 0.0)

    def prep(gi, carry):
        cs = [gi * grp + j for j in range(grp)]
        cds = [(j, d) for j in range(grp) for d in range(2)]
        rows = [pl.ds(pl.multiple_of(c * L, L), L) for c in cs]
        srows = [pl.ds(pl.multiple_of(c * 2 * L, 2 * L), 2 * L) for c in cs]
        vv2 = [two(v_scr[r, :]) for r in rows]
        wl = {(j, d): wl_scr[d, rows[j], :] for j, d in cds}
        cin = {jd: _dot01_l(_tri_masks(jd[1] == 1)[0].astype(F32), wl[jd]) for jd in cds}
        decay = {jd: jnp.exp(_dot01_tn(wl[jd], ones_l)) for jd in cds}
        ops = {}
        for j, d in cds:
            e_in, e_out = jnp.exp(cin[j, d]), jnp.exp(-cin[j, d])
            ops[j, d] = (two(aa_scr[rows[j], :] * jnp.exp(cin[j, d] - wl[j, d])), two(r_scr[rows[j], :] * e_in),
                         two(bb_scr[d, rows[j], :] * e_out), two(kd_scr[d, rows[j], :] * e_out))
        prod = {jd: mm_nt(jnp.concatenate(ops[jd][0:2], axis=0), jnp.concatenate(ops[jd][2:4], axis=0)) for jd in cds}
        blk = {}
        for j, d in cds:
            incl_bd, strict_bd = bd_masks[d]
            p = prod[j, d]
            blk[j, d] = (jnp.where(strict_bd, p[0:2 * L, 0:2 * L], 0.0), jnp.where(strict_bd, p[0:2 * L, 2 * L:], 0.0),
                         jnp.where(incl_bd, p[2 * L:, 0:2 * L], 0.0), jnp.where(incl_bd, p[2 * L:, 2 * L:], 0.0))
        packs = [-jnp.concatenate([blk[j, d][0][0:L, :] + blk[j, d][0][L:, :] for d in range(2)], axis=1)
                 for j in range(grp)]
        inv = _inv_unit_packed_multi(packs)
        t_inv = {(j, d): _to_bd(inv[j][:, d * 2 * L:(d + 1) * 2 * L]) for j, d in cds}
        kv = {(j, d): mm(jnp.concatenate(blk[j, d][1::2], axis=0), vv2[j]) for j, d in cds}
        m1c1 = {jd: mm(t_inv[jd], jnp.concatenate([ops[jd][0], kv[jd][0:2 * L, :]], axis=1)) for jd in cds}
        rm = {jd: mm(blk[jd][2], m1c1[jd]) for jd in cds}
        bm = {jd: mm_tn(ops[jd][2], m1c1[jd]) for jd in cds}
        kv2 = {(j, d): mm_tn(ops[j, d][3], vv2[j]) for j, d in cds}
        for j, d in cds:
            m2ah = jnp.concatenate([ops[j, d][1] + rm[j, d][:, 0:128], decay[j, d] * bm[j, d][:, 0:128]], axis=0)
            hi, lo = _split2(m2ah)
            mrows = pl.ds(pl.multiple_of(cs[j] * 4 * L, 4 * L), 4 * L)
            mah_scr[d, mrows, :] = hi
            mal_scr[d, mrows, :] = lo
            c2_scr[d, srows[j], :] = rm[j, d][:, 128:] + kv[j, d][2 * L:, :]
            bc_scr[d, srows[j], :] = decay[j, d] * (bm[j, d][:, 128:] + kv2[j, d])
            dg_scr[d, srows[j], :] = decay[j, d]
        return carry

    lax.fori_loop(0, nc // grp, prep, 0)

    chains = [(s, d) for s in range(nsub) for d in range(2)]
    for s, d in chains:
        if has_state:
            stacked = s0_ref[s, d].reshape(2 * RW_K, RW_K)
            state_scr[2 * s + d] = jnp.where(same_head, _dot01_r(stacked, dup), 0.0).T
        else:
            state_scr[2 * s + d] = jnp.zeros((128, 128), F32)

    def scan(i, carry):
        for s, d in chains:
            c = s * nc_seq + ((nc_seq - 1 - i) if d else i)
            rows = pl.ds(pl.multiple_of(c * L, L), L)
            srows = pl.ds(pl.multiple_of(c * 2 * L, 2 * L), 2 * L)
            h_st = state_scr[2 * s + d]
            mrows = pl.ds(pl.multiple_of(c * 4 * L, 4 * L), 4 * L)
            lh, ll = mah_scr[d, mrows, :], mal_scr[d, mrows, :]
            hh, hl = _split2(h_st)
            res = _dot(lh, hh)
            corr = res[2 * L:, :] + _dot(lh[2 * L:, :], hl) + _dot(ll[2 * L:, :], hh)
            y2 = res[0:2 * L, :] + c2_scr[d, srows, :]
            yacc_scr[d, rows, :] = y2[0:L, :] + y2[L:, :]
            state_scr[2 * s + d] = dg_scr[d, srows, :] * h_st + corr + bc_scr[d, srows, :]
        return carry

    lax.fori_loop(0, nc_seq, scan, 0)
    if not has_state:
        dup_t = ((_iota((128, RW_K), 0) & (RW_K - 1)) == _iota((128, RW_K), 1)).astype(F32)
        for s, d in chains:
            st_ref[s, d] = _dot01_r(state_scr[2 * s + d].T, dup_t).reshape(2, RW_K, RW_K)

    y = yacc_scr[0] + yacc_scr[1]
    mean = head_sum(y) * (1.0 / RW_K)
    yc = y - mean
    var = head_sum(yc * yc) * (1.0 / RW_K)
    yn = yc * lax.rsqrt(var + RWKV_LN_EPS) * lng_ref[...] + lnb_ref[...]
    bonus = head_sum(r * (0.5 * kd_sum) * rk_ref[...]) * v
    o_ref[...] = ((yn + bonus) * gate).astype(BF16)


def _rwkv(proj, nseq, t, prm, state, nsub=1):
    has_state = state is not None
    mu, w0, wup, a0, aup, gup, k_k, k_a, r_k, ln_g, ln_b = prm
    seq = lambda base: (lambda b, hp: (b, base // 128 + hp))
    fix = lambda base: (lambda b, hp: (b, base // 128))
    mu_at = lambda base: (lambda b, hp: (0, (base - OD_R) // 128 + hp))
    mu_fix = lambda base: (lambda b, hp: (0, (base - OD_R) // 128))
    col = lambda b, hp: (0, hp)
    rb = nsub * t
    in_specs = [
        pl.BlockSpec((rb, 128), seq(OD_R)), pl.BlockSpec((rb, 128), seq(OD_RK)), pl.BlockSpec((rb, 128), seq(OD_RV)),
        pl.BlockSpec((rb, 128), fix(OD_WD)), pl.BlockSpec((rb, 128), fix(OD_AD)), pl.BlockSpec((rb, 128), fix(OD_GD)),
        pl.BlockSpec((1, 128), mu_at(OD_R)), pl.BlockSpec((1, 128), mu_at(OD_RK)), pl.BlockSpec((1, 128), mu_at(OD_RV)),
        pl.BlockSpec((1, 128), mu_fix(OD_WD)), pl.BlockSpec((1, 128), mu_fix(OD_AD)), pl.BlockSpec((1, 128), mu_fix(OD_GD)),
        pl.BlockSpec((2, 128), col), pl.BlockSpec((128, 128), col),
        pl.BlockSpec((2, 128), col), pl.BlockSpec((128, 128), col),
        pl.BlockSpec((128, 128), col),
        pl.BlockSpec((1, 128), col), pl.BlockSpec((1, 128), col), pl.BlockSpec((1, 128), col),
        pl.BlockSpec((1, 128), col), pl.BlockSpec((1, 128), col),
    ]
    args = [proj] * 6 + [mu] * 6 + [w0, wup, a0, aup, gup, k_k, k_a, r_k, ln_g, ln_b]
    st_block = (nsub, 2, 2, RW_K, RW_K)
    st_map = lambda b, hp: (b, 0, hp, 0, 0)
    out_specs = [pl.BlockSpec((rb, 128), lambda b, hp: (b, hp))]
    out_shape = [jax.ShapeDtypeStruct((nseq * t, RW_HEADS * RW_K), BF16)]
    if has_state:
        in_specs.append(pl.BlockSpec(st_block, st_map))
        args.append(state)
    else:
        out_specs.append(pl.BlockSpec(st_block, st_map))
        out_shape.append(jax.ShapeDtypeStruct((nseq, 2, RW_HEADS, RW_K, RW_K), F32))
    scratch = [pltpu.VMEM((t + 2 * CONV_PAD, 128), F32), pltpu.VMEM((rb, 128), F32), pltpu.VMEM((rb, 128), F32),
               pltpu.VMEM((rb, 128), F32), pltpu.VMEM((2, rb, 128), F32), pltpu.VMEM((2, rb, 128), F32),
               pltpu.VMEM((2, rb, 128), F32)] + [pltpu.VMEM((2, 4 * rb, 128), BF16)] * 2 + [
               pltpu.VMEM((2, 2 * rb, 128), F32)] * 3 + [
               pltpu.VMEM((2, rb, 128), F32), pltpu.VMEM((2 * nsub, 128, 128), F32)]
    return pl.pallas_call(
        functools.partial(_rwkv_kernel, t=t, nsub=nsub, has_state=has_state),
        grid=(nseq // nsub, RW_HEADS // 2), in_specs=in_specs, out_specs=out_specs, out_shape=out_shape,
        scratch_shapes=scratch, compiler_params=_cparams(("arbitrary", "arbitrary")), name="rwkv",
    )(*args)


def _even_w_in(w):
    za, xs, bm, cm = w[:, 0:1024], w[:, 1024:2048], w[:, 2048:2304], w[:, 2304:2560]
    dt, q, kk, v = w[:, 2560:2592], w[:, 2592:3616], w[:, 3616:4640], w[:, 4640:5664]
    zb, al, be = w[:, 5664:6688], w[:, 6688:6704], w[:, 6704:6720]
    pad = jnp.zeros((w.shape[0], EV_COLS - EV_SM - 64), w.dtype)
    return jnp.concatenate([za, xs, zb, q, kk, v, bm, cm, dt, al, be, pad], axis=1)


def _pad_row(parts):
    flat = jnp.concatenate([p.reshape(-1) for p in parts])
    return jnp.pad(flat, (0, 128 - flat.shape[0])).reshape(1, 128)


def kernel(x_prompt, x_sample, state_ssd, state_gdn, cache_k, cache_v, state_rwkv, c, c_ctx, ada_w, ada_b, norm1_g, norm2_g, ffn_w1, ffn_w2, final_g, ev_w_in, ev_w_out, ssd_conv_w, ssd_conv_b, ssd_a_log, ssd_dt_bias, ssd_d, ssd_norm_g, gdn_conv_w, gdn_a_log, gdn_dt_bias, gdn_norm_g, od_w_in, od_w_out, attn_sink, rwkv_mu, rwkv_w0, rwkv_w_up, rwkv_a0, rwkv_a_up, rwkv_g_up, rwkv_k_k, rwkv_k_a, rwkv_r_k, rwkv_ln_g, rwkv_ln_b):
    bp, tp, _ = x_prompt.shape
    bs, ts, _ = x_sample.shape
    streams = [(x_prompt.reshape(bp * tp, D), bp, tp, None), (x_sample.reshape(bs * ts, D), bs, ts, ts)]

    cvec = jnp.concatenate([c_ctx[None, :], c, jnp.zeros((16 - 1 - bs, D), F32)], axis=0)
    mod = _adaln(cvec, ada_w, ada_b)

    w_in0 = _even_w_in(ev_w_in[0].astype(BF16))
    w_out0 = ev_w_out[0].astype(BF16)
    bias_row = _pad_row([ssd_dt_bias[0], gdn_dt_bias[0]])
    alog_row = _pad_row([ssd_a_log[0], gdn_a_log[0]])
    dskip_row = jnp.repeat(ssd_d[0], SSD_P).reshape(1, SSD_HEADS * SSD_P)
    w1_0, w2_0 = ffn_w1[0].astype(BF16), ffn_w2[0].astype(BF16)
    w_in1 = od_w_in[0].astype(BF16)
    w_out1 = od_w_out[0].astype(BF16)
    w1_1, w2_1 = ffn_w1[1].astype(BF16), ffn_w2[1].astype(BF16)
    rw_prm = (rwkv_mu[0].reshape(1, -1), rwkv_w0[0], rwkv_w_up[0].reshape(2 * 64, -1), rwkv_a0[0],
              rwkv_a_up[0].reshape(2 * 64, -1), rwkv_g_up[0], rwkv_k_k[0].reshape(1, -1), rwkv_k_a[0].reshape(1, -1),
              rwkv_r_k[0].reshape(1, -1), rwkv_ln_g[0].reshape(1, -1), rwkv_ln_b[0].reshape(1, -1))
    cos2, sin2 = _rope_tables(ts)
    past = cache_k.shape[2]
    ck = cache_k[:, 0].reshape(bs * past, ATT_KVH * ATT_DH)
    cv = cache_v[:, 0].reshape(bs * past, ATT_KVH * ATT_DH)

    outs = []
    for x, nseq, t, sample_t in streams:
        latent = sample_t is not None
        m0 = mod[0]
        proj = _inproj(x, norm1_g[0:1], m0, w_in0, sample_t)
        s_ssd = state_ssd[:, 0] if latent else None
        s_gdn = state_gdn[:, 0] if latent else None
        r_ssd = _ssd(proj, nseq, t, ssd_conv_w[0], ssd_conv_b[0].reshape(1, -1), bias_row, alog_row, dskip_row,
                     ssd_norm_g[0].reshape(1, -1), s_ssd)
        r_gdn = _gdn(proj, nseq, t, gdn_conv_w[0], bias_row, alog_row, gdn_norm_g[0].reshape(1, -1), s_gdn,
                     nsub=1 if latent else GDN_PROMPT_NSUB)
        x = _mix_ffn(r_ssd[0], r_gdn[0], x, norm2_g[0:1], m0, w_out0, w1_0, w2_0, final_g.reshape(1, D), sample_t,
                     final=False)
        m1 = mod[1]
        proj = _inproj(x, norm1_g[1:2], m1, w_in1, sample_t)
        if latent:
            att = _lat_attn(proj, nseq, t, attn_sink[0], ck, cv, cos2, sin2)
            r_rw = _rwkv(proj, nseq, t, rw_prm, state_rwkv[:, 0])
        else:
            att = _ctx_attn(proj, nseq, t, attn_sink[0])
            r_rw = _rwkv(proj, nseq, t, rw_prm, None, nsub=RW_PROMPT_NSUB)
        x = _mix_ffn(att, r_rw[0], x, norm2_g[1:2], m1, w_out1, w1_1, w2_1, final_g.reshape(1, D), sample_t,
                     final=True)
        outs.append((x.reshape(nseq, t, D), r_ssd, r_gdn, proj, r_rw))

    (y_prompt, p_ssd, p_gdn, p_proj, p_rw), (y_sample, _, _, _, _) = outs
    new_k = p_proj[:, OD_K:OD_V].reshape(bp, 1, tp, ATT_KVH, ATT_DH)
    new_v = p_proj[:, OD_V:OD_R].reshape(bp, 1, tp, ATT_KVH, ATT_DH)
    return (y_prompt, y_sample, p_ssd[1][:, None], p_gdn[1][:, None], new_k, new_v, p_rw[1][:, None])
```

```python
import functools

import jax
import jax.numpy as jnp
from jax import lax
from jax.experimental import pallas as pl
from jax.experimental.pallas import tpu as pltpu

F32 = jnp.float32
BF16 = jnp.bfloat16

D = 1024
NORM_EPS = 1e-6
L = 64
GDN_GROUP, RW_GROUP = 8, 8
GDN_PROMPT_NSUB, RW_PROMPT_NSUB = 4, 2
CONV_W = 5
CONV_PAD = 8

SSD_HEADS, SSD_P, SSD_N, SSD_GROUPS = 16, 64, 128, 2
SSD_GW = SSD_HEADS // SSD_GROUPS * SSD_P
GDN_HEADS, GDN_K = 8, 128
ATT_HEADS, ATT_KVH, ATT_DH, ATT_GROUP = 8, 2, 128, 4
WINDOW, QBLOCK, GRID_W, ROPE_BASE = 128, 128, 64, 10000.0
RW_HEADS, RW_K = 16, 64
RWKV_LN_EPS = 64e-5
FFN_H = 2816
FFN_TN = 256

EV_ZA, EV_XS, EV_ZB, EV_Q, EV_K, EV_V, EV_B, EV_C, EV_SM, EV_COLS = 0, 1024, 2048, 3072, 4096, 5120, 6144, 6400, 6656, 6784
SM_DT, SM_ALPHA, SM_BETA = 0, 32, 48
OD_Q, OD_K, OD_V, OD_R, OD_RK, OD_RV, OD_WD, OD_AD, OD_GD, OD_COLS = 0, 1024, 1280, 1536, 2560, 3584, 4608, 4736, 4864, 4992

VMEM_LIMIT = 56 * 1024 * 1024


def _cparams(sem):
    return pltpu.CompilerParams(dimension_semantics=sem, vmem_limit_bytes=VMEM_LIMIT)


def _dot(a, b, prec=None):
    return jnp.dot(a, b, preferred_element_type=F32, precision=prec)


def _dot_nt(a, b, prec=None):
    return lax.dot_general(a, b, (((1,), (1,)), ((), ())), preferred_element_type=F32, precision=prec)


def _dot_tn(a, b, prec=None):
    return lax.dot_general(a, b, (((0,), (0,)), ((), ())), preferred_element_type=F32, precision=prec)


def _bdot(a, b):
    return _dot(a.astype(BF16), b.astype(BF16))


def _bdot_nt(a, b):
    return _dot_nt(a.astype(BF16), b.astype(BF16))


def _bdot_tn(a, b):
    return _dot_tn(a.astype(BF16), b.astype(BF16))


def _silu(x):
    return x * jax.nn.sigmoid(x)


def _softplus(x):
    return jnp.maximum(x, 0.0) + jnp.log(1.0 + jnp.exp(-jnp.abs(x)))


def _iota(shape, dim):
    return lax.broadcasted_iota(jnp.int32, shape, dim)


def _tri_masks(rev):
    r, c = _iota((L, L), 0), _iota((L, L), 1)
    if rev:
        return c >= r, c > r
    return c <= r, c < r


def _split3(x):
    hi = x.astype(BF16)
    r1 = x - hi.astype(F32)
    mid = r1.astype(BF16)
    lo = (r1 - mid.astype(F32)).astype(BF16)
    return hi, mid, lo


def _dot01_l(m01, x):
    mb = m01.astype(BF16)
    return _dot(jnp.concatenate([mb, mb, mb], axis=1), jnp.concatenate(_split3(x), axis=0))


def _dot01_r(x, m01):
    mb = m01.astype(BF16)
    return _dot(jnp.concatenate(_split3(x), axis=1), jnp.concatenate([mb, mb, mb], axis=0))


def _dot01_tn(x, m01):
    mb = m01.astype(BF16)
    return _dot_tn(jnp.concatenate(_split3(x), axis=0), jnp.concatenate([mb, mb, mb], axis=0))


def _to_bd(m):
    n = m.shape[1] // L
    blk = jnp.right_shift(_iota(m.shape, 1), L.bit_length() - 1)
    return jnp.concatenate([jnp.where(blk == j, m, 0.0) for j in range(n)], axis=0)


def _split2(x):
    hi = x.astype(BF16)
    return hi, (x - hi.astype(F32)).astype(BF16)


def _inv_unit_packed_multi(mats):
    eye = ((_iota(mats[0].shape, 1) & (L - 1)) == _iota(mats[0].shape, 0)).astype(F32)

    def mm3(lh, ll, ph, pl_):
        bh, bl = _to_bd(ph), _to_bd(pl_)
        return _dot(lh, bh) + _dot(lh, bl) + _dot(ll, bh)

    xs = [eye - a for a in mats]
    ps = [mm3(h, lo, h, lo) for h, lo in map(_split2, mats)]
    n = 2
    while True:
        sp = [_split2(p) for p in ps]
        sx = [_split2(x) for x in xs]
        n *= 2
        if n >= L:
            return [x + mm3(xh, xl, ph, pl_) for x, (xh, xl), (ph, pl_) in zip(xs, sx, sp)]
        res = [mm3(jnp.concatenate([xh, ph], axis=0), jnp.concatenate([xl, pl_], axis=0), ph, pl_)
               for (xh, xl), (ph, pl_) in zip(sx, sp)]
        xs = [x + r[0:L, :] for x, r in zip(xs, res)]
        ps = [r[L:, :] for r in res]


def _conv_silu(x, w_ref, b, pad_ref, t):
    c = x.shape[1]
    zeros = jnp.zeros((CONV_PAD, c), F32)
    pad_ref[0:CONV_PAD, 0:c] = zeros
    pad_ref[CONV_PAD + t:2 * CONV_PAD + t, 0:c] = zeros
    pad_ref[CONV_PAD:CONV_PAD + t, 0:c] = x
    acc = None
    for j in range(CONV_W):
        off = CONV_PAD - CONV_W // 2 + j
        term = pad_ref[off:off + t, 0:c] * w_ref[j:j + 1, :]
        acc = term if acc is None else acc + term
    if b is not None:
        acc = acc + b
    return _silu(acc)


def _adaln_kernel(c_ref, w_ref, b_ref, o_ref):
    c = c_ref[...]
    o_ref[0] = _bdot(_silu(c), w_ref[0]) + b_ref[0]


def _adaln(cvec, ada_w, ada_b):
    depth, _, n = ada_w.shape
    tn = 1536
    out = pl.pallas_call(
        _adaln_kernel,
        grid=(depth, n // tn),
        in_specs=[pl.BlockSpec((16, D), lambda l, j: (0, 0)),
                  pl.BlockSpec((1, D, tn), lambda l, j: (l, 0, j)),
                  pl.BlockSpec((1, 1, tn), lambda l, j: (l, 0, j))],
        out_specs=pl.BlockSpec((1, 16, tn), lambda l, j: (l, 0, j)),
        out_shape=jax.ShapeDtypeStruct((depth, 16, n), F32),
        compiler_params=_cparams(("arbitrary", "arbitrary")),
        name="adaln",
    )(cvec, ada_w, ada_b.reshape(depth, 1, n))
    return out.reshape(depth, 16, 6, D)


def _mod_row_map(tm, t):
    if t is None:
        return lambda i: (0, 0, 0)
    per = t // tm
    return lambda i: (1 + i // per, 0, 0)


def _norm_mod(x, g, m, shift_idx, scale_idx):
    var = jnp.mean(x * x, axis=-1, keepdims=True)
    y = x * lax.rsqrt(var + NORM_EPS) * g
    return y * (1.0 + m[scale_idx:scale_idx + 1]) + m[shift_idx:shift_idx + 1]


def _inproj_kernel(x_ref, g_ref, mod_ref, w_ref, o_ref, *, tn):
    h = _norm_mod(x_ref[...], g_ref[...], mod_ref[0], 0, 1).astype(BF16)
    n = w_ref.shape[1]
    for j in range(0, n, tn):
        w = min(tn, n - j)
        o_ref[:, j:j + w] = _dot(h, w_ref[:, j:j + w])


def _inproj(x, g, mod, w, sample_t, tm=256, tn=512):
    rows, n = x.shape[0], w.shape[1]
    return pl.pallas_call(
        functools.partial(_inproj_kernel, tn=tn),
        grid=(rows // tm,),
        in_specs=[pl.BlockSpec((tm, D), lambda i: (i, 0)),
                  pl.BlockSpec((1, D), lambda i: (0, 0)),
                  pl.BlockSpec((1, 6, D), _mod_row_map(tm, sample_t)),
                  pl.BlockSpec((D, n), lambda i: (0, 0))],
        out_specs=pl.BlockSpec((tm, n), lambda i: (i, 0)),
        out_shape=jax.ShapeDtypeStruct((rows, n), F32),
        compiler_params=_cparams(("arbitrary",)),
        name="inproj",
    )(x, g, mod, w)


def _mix_ffn_kernel(a1_ref, a2_ref, x_ref, g_ref, mod_ref, wo_ref, w1_ref, w2_ref, fg_ref, o_ref, *, final):
    m = mod_ref[0]
    k1 = a1_ref.shape[1]
    mix = _dot(a1_ref[...], wo_ref[0:k1, :]) + _dot(a2_ref[...], wo_ref[k1:, :])
    x = x_ref[...] + m[2:3] * mix
    h = _norm_mod(x, g_ref[...], m, 3, 4).astype(BF16)
    acc = jnp.zeros(x.shape, F32)
    for j in range(0, FFN_H, FFN_TN):
        gate = _dot(h, w1_ref[:, j:j + FFN_TN])
        up = _dot(h, w1_ref[:, FFN_H + j:FFN_H + j + FFN_TN])
        acc = acc + _dot((_silu(gate) * up).astype(BF16), w2_ref[j:j + FFN_TN, :])
    y = x + m[5:6] * acc
    if final:
        var = jnp.mean(y * y, axis=-1, keepdims=True)
        y = y * lax.rsqrt(var + NORM_EPS) * fg_ref[...]
    o_ref[...] = y


def _mix_ffn(a1, a2, x, g, mod, w_out, w1, w2, final_g, sample_t, final, tm=256):
    rows = x.shape[0]
    once = dict(pipeline_mode=pl.Buffered(1))
    return pl.pallas_call(
        functools.partial(_mix_ffn_kernel, final=final),
        grid=(rows // tm,),
        in_specs=[pl.BlockSpec((tm, a1.shape[1]), lambda i: (i, 0)),
                  pl.BlockSpec((tm, a2.shape[1]), lambda i: (i, 0)),
                  pl.BlockSpec((tm, D), lambda i: (i, 0)),
                  pl.BlockSpec((1, D), lambda i: (0, 0)),
                  pl.BlockSpec((1, 6, D), _mod_row_map(tm, sample_t)),
                  pl.BlockSpec(w_out.shape, lambda i: (0, 0), **once),
                  pl.BlockSpec(w1.shape, lambda i: (0, 0), **once),
                  pl.BlockSpec(w2.shape, lambda i: (0, 0), **once),
                  pl.BlockSpec((1, D), lambda i: (0, 0))],
        out_specs=pl.BlockSpec((tm, D), lambda i: (i, 0)),
        out_shape=jax.ShapeDtypeStruct((rows, D), F32),
        compiler_params=_cparams(("arbitrary",)),
        name="mix_ffn",
    )(a1, a2, x, g, mod, w_out, w1, w2, final_g)


def _ssd_kernel(*refs, t, has_state):
    (za_ref, xs_ref, bm_ref, cm_ref, sm_ref, cwx_ref, cwb_ref, cwc_ref, cbx_ref, cbb_ref, cbc_ref,
     bias_ref, alog_ref, dsk_ref, ng_ref) = refs[:15]
    k = 15
    s0_ref = st_ref = None
    if has_state:
        s0_ref = refs[k]
        k += 1
    y_ref = refs[k]
    k += 1
    if not has_state:
        st_ref = refs[k]
        k += 1
    pad_scr, xs_scr, bb_scr, cc_scr, dt_scr, da_scr, dasm_scr, yacc_scr, state_scr = refs[k:]

    g = pl.program_id(1)
    nc = t // L
    npair = SSD_GW // 128

    xs_scr[...] = _conv_silu(xs_ref[...], cwx_ref, cbx_ref[...], pad_scr, t)
    bb_scr[...] = _conv_silu(bm_ref[...], cwb_ref, cbb_ref[...], pad_scr, t)
    cc_scr[...] = _conv_silu(cm_ref[...], cwc_ref, cbc_ref[...], pad_scr, t)

    dtv = _softplus(sm_ref[...] + bias_ref[...])
    dav = dtv * (-jnp.exp(alog_ref[...]))
    heads_per_group = SSD_HEADS // SSD_GROUPS
    for d in range(2):
        first = SM_DT + d * SSD_HEADS + g * heads_per_group
        src = _iota((128, SSD_GW), 0)
        e = (src == first + jnp.right_shift(_iota((128, SSD_GW), 1), 6)).astype(F32)
        dt_scr[d] = _dot01_r(dtv, e)
        da_scr[d] = _dot01_r(dav, e)
        sel = (_iota((128, 128), 0) == first + _iota((128, 128), 1)) & (_iota((128, 128), 1) < heads_per_group)
        dasm_scr[d] = _dot01_r(dav, sel.astype(F32))

    lane128 = _iota((L, 128), 1)
    for d in range(2):
        if has_state:
            state_scr[d] = s0_ref[0, d].reshape(SSD_GW, SSD_N).T
        else:
            state_scr[d] = jnp.zeros((SSD_N, SSD_GW), F32)

    def body(ci, carry):
        for d in range(2):
            rev = d == 1
            incl = _tri_masks(rev)[0]
            tri = incl.astype(F32)
            tri_t = _tri_masks(not rev)[0].astype(F32)
            last = 0 if rev else L - 1
            c = (nc - 1 - ci) if rev else ci
            r0 = pl.multiple_of(c * L, L)
            rows = pl.ds(r0, L)
            x = xs_scr[rows, :]
            bc = bb_scr[rows, :]
            cc = cc_scr[rows, :]
            acum = _dot01_l(tri, da_scr[d, rows, :])
            acum_t = _dot01_tn(dasm_scr[d, rows, :], tri_t)
            xdt = x * dt_scr[d, rows, :]
            cb = _bdot_nt(cc, bc)
            total = acum[last:last + 1, :]
            s_in = state_scr[d]
            y_inter = _bdot(cc, s_in) * jnp.exp(acum)
            pieces = []
            for j in range(npair):
                xp = xdt[:, j * 128:(j + 1) * 128]
                yp = None
                for hh in range(2):
                    h = 2 * j + hh
                    col = acum[:, h * SSD_P:h * SSD_P + 1]
                    row = acum_t[h:h + 1, :]
                    dec = jnp.where(incl, jnp.exp(jnp.minimum(col - row, 0.0)), 0.0)
                    mine = (lane128 >= SSD_P) if hh else (lane128 < SSD_P)
                    term = _bdot(cb * dec, jnp.where(mine, xp, 0.0))
                    yp = term if yp is None else yp + term
                pieces.append(yp)
            yacc_scr[d, rows, :] = jnp.concatenate(pieces, axis=1) + y_inter
            wx = jnp.exp(total - acum) * xdt
            state_scr[d] = s_in * jnp.exp(total) + _bdot_tn(bc, wx)
        return carry

    lax.fori_loop(0, nc, body, 0)
    if not has_state:
        for d in range(2):
            st_ref[0, d] = state_scr[d].T.reshape(heads_per_group, SSD_P, SSD_N)

    y = yacc_scr[0] + yacc_scr[1] + dsk_ref[...] * xs_scr[...]
    y = y * _silu(za_ref[...])
    var = jnp.mean(y * y, axis=-1, keepdims=True)
    y_ref[...] = (y * lax.rsqrt(var + NORM_EPS) * ng_ref[...]).astype(BF16)


def _ssd(proj, nseq, t, conv_w, conv_b, bias_row, alog_row, dskip_row, norm_g, state):
    has_state = state is not None
    seq = lambda blk: (lambda b, g: (b, blk(g)))
    const = lambda blk: (lambda b, g: (0, blk(g)))
    in_specs = [
        pl.BlockSpec((t, SSD_GW), seq(lambda g: EV_ZA // SSD_GW + g)),
        pl.BlockSpec((t, SSD_GW), seq(lambda g: EV_XS // SSD_GW + g)),
        pl.BlockSpec((t, 128), seq(lambda g: EV_B // 128 + g)),
        pl.BlockSpec((t, 128), seq(lambda g: EV_C // 128 + g)),
        pl.BlockSpec((t, 128), seq(lambda g: EV_SM // 128)),
        pl.BlockSpec((CONV_W, SSD_GW), const(lambda g: g)),
        pl.BlockSpec((CONV_W, 128), const(lambda g: 1024 // 128 + g)),
        pl.BlockSpec((CONV_W, 128), const(lambda g: 1280 // 128 + g)),
        pl.BlockSpec((1, SSD_GW), const(lambda g: g)),
        pl.BlockSpec((1, 128), const(lambda g: 1024 // 128 + g)),
        pl.BlockSpec((1, 128), const(lambda g: 1280 // 128 + g)),
        pl.BlockSpec((1, 128), const(lambda g: 0)),
        pl.BlockSpec((1, 128), const(lambda g: 0)),
        pl.BlockSpec((1, SSD_GW), const(lambda g: g)),
        pl.BlockSpec((1, SSD_GW), const(lambda g: g)),
    ]
    args = [proj, proj, proj, proj, proj, conv_w, conv_w, conv_w, conv_b, conv_b, conv_b,
            bias_row, alog_row, dskip_row, norm_g]
    st_block = (1, 2, SSD_HEADS // SSD_GROUPS, SSD_P, SSD_N)
    st_map = lambda b, g: (b, 0, g, 0, 0)
    out_specs = [pl.BlockSpec((t, SSD_GW), lambda b, g: (b, g))]
    out_shape = [jax.ShapeDtypeStruct((nseq * t, SSD_HEADS * SSD_P), BF16)]
    if has_state:
        in_specs.append(pl.BlockSpec(st_block, st_map))
        args.append(state)
    else:
        out_specs.append(pl.BlockSpec(st_block, st_map))
        out_shape.append(jax.ShapeDtypeStruct((nseq, 2, SSD_HEADS, SSD_P, SSD_N), F32))
    scratch = [pltpu.VMEM((t + 2 * CONV_PAD, SSD_GW), F32), pltpu.VMEM((t, SSD_GW), F32),
               pltpu.VMEM((t, 128), F32), pltpu.VMEM((t, 128), F32),
               pltpu.VMEM((2, t, SSD_GW), F32), pltpu.VMEM((2, t, SSD_GW), F32), pltpu.VMEM((2, t, 128), F32),
               pltpu.VMEM((2, t, SSD_GW), F32), pltpu.VMEM((2, SSD_N, SSD_GW), F32)]
    return pl.pallas_call(
        functools.partial(_ssd_kernel, t=t, has_state=has_state),
        grid=(nseq, SSD_GROUPS), in_specs=in_specs, out_specs=out_specs, out_shape=out_shape,
        scratch_shapes=scratch, compiler_params=_cparams(("arbitrary", "arbitrary")), name="ssd",
    )(*args)


def _gdn_kernel(*refs, t, nsub, has_state):
    (q_ref, k_ref, v_ref, zb_ref, sm_ref, cwq_ref, cwk_ref, cwv_ref, bias_ref, alog_ref, ng_ref) = refs[:11]
    k = 11
    s0_ref = st_ref = None
    if has_state:
        s0_ref = refs[k]
        k += 1
    o_ref = refs[k]
    k += 1
    if not has_state:
        st_ref = refs[k]
        k += 1
    (pad_scr, q_scr, k_scr, v_scr, g_scr, b_scr, ma_scr, c2_scr, bc_scr, dg_scr,
     oacc_scr, state_scr) = refs[k:]

    h = pl.program_id(1)
    nc_seq = t // L
    nc = nsub * nc_seq
    rows_all = nsub * t

    def l2n(x):
        return x * lax.rsqrt(jnp.sum(x * x, axis=-1, keepdims=True) + NORM_EPS)

    for s in range(nsub):
        rs = slice(s * t, (s + 1) * t)
        q_scr[rs, :] = l2n(_conv_silu(q_ref[rs, :], cwq_ref, None, pad_scr, t)) * (GDN_K ** -0.5)
        k_scr[rs, :] = l2n(_conv_silu(k_ref[rs, :], cwk_ref, None, pad_scr, t))
        v_scr[rs, :] = _conv_silu(v_ref[rs, :], cwv_ref, None, pad_scr, t)

    sm = sm_ref[...]
    gl = -jnp.exp(alog_ref[...]) * _softplus(sm + bias_ref[...])
    bt = jax.nn.sigmoid(sm)
    lane = _iota((rows_all, 128), 1)
    for d in range(2):
        gcol = jnp.sum(jnp.where(lane == SM_ALPHA + d * GDN_HEADS + h, gl, 0.0), axis=1, keepdims=True)
        bcol = jnp.sum(jnp.where(lane == SM_BETA + d * GDN_HEADS + h, bt, 0.0), axis=1, keepdims=True)
        g_scr[d] = jnp.broadcast_to(gcol, (rows_all, 128))
        b_scr[d] = jnp.broadcast_to(bcol, (rows_all, 128))

    masks = [_tri_masks(False), _tri_masks(True)]
    grp = min(GDN_GROUP, nc)

    def prep(gi, carry):
        cs = [gi * grp + j for j in range(grp)]
        rows = [pl.ds(pl.multiple_of(c * L, L), L) for c in cs]
        qkv = [(q_scr[r, :], k_scr[r, :], v_scr[r, :]) for r in rows]
        qkk = [_bdot_nt(jnp.concatenate([qc, kc], axis=0), kc) for qc, kc, _ in qkv]
        gbs = [[(g_scr[d, r, :], b_scr[d, r, :]) for d in range(2)] for r in rows]
        gam = [[_dot01_l(masks[d][0].astype(F32), gbs[j][d][0]) for d in range(2)] for j in range(grp)]
        gam_row = [[_dot01_tn(gbs[j][d][0], masks[1 - d][0].astype(F32))[0:1, :] for d in range(2)]
                   for j in range(grp)]
        per = []
        for j in range(grp):
            qc, kc, vc = qkv[j]
            row = []
            for d in range(2):
                incl, strict = masks[d]
                bb = gbs[j][d][1]
                g = gam[j][d]
                dec = jnp.where(incl, jnp.exp(jnp.minimum(g[:, 0:1] - gam_row[j][d], 0.0)), 0.0)
                a_mat = jnp.where(strict, bb[:, 0:1] * qkk[j][L:, :] * dec, 0.0)
                eg = jnp.exp(g)
                last = 0 if d else L - 1
                g_last = g[last:last + 1, :]
                rhs = jnp.concatenate([bb * vc, bb * kc * eg], axis=1)
                row.append((a_mat, rhs, qkk[j][0:L, :] * dec, qc * eg, kc * jnp.exp(g_last - g), g_last))
            per.append(row)
        packs = [jnp.concatenate([per[j][0][0], per[j][1][0], per[j + 1][0][0], per[j + 1][1][0]], axis=1)
                 for j in range(0, grp, 2)]
        inv = _inv_unit_packed_multi(packs)
        t_inv = [_to_bd(inv[j // 2][:, (j % 2) * 2 * L:(j % 2 + 1) * 2 * L]) for j in range(grp)]
        uw = [_bdot(t_inv[j], jnp.concatenate([per[j][0][1], per[j][1][1]], axis=0))
              for j in range(grp)]
        qk_uw = [_bdot(_to_bd(jnp.concatenate([per[j][0][2], per[j][1][2]], axis=1)), uw[j])
                 for j in range(grp)]
        ke_uw = [[_bdot_tn(per[j][d][4], uw[j][d * L:(d + 1) * L, :]) for d in range(2)]
                 for j in range(grp)]
        for j, c in enumerate(cs):
            srows = pl.ds(pl.multiple_of(c * GDN_K, GDN_K), GDN_K)
            drows = pl.ds(pl.multiple_of(c * 8, 8), 8)
            for d in range(2):
                half = slice(d * L, (d + 1) * L)
                mrows = pl.ds(pl.multiple_of(c * (L + GDN_K), L + GDN_K), L + GDN_K)
                ma_scr[d, mrows, :] = jnp.concatenate(
                    [per[j][d][3] - qk_uw[j][half, GDN_K:], -ke_uw[j][d][:, GDN_K:]], axis=0).astype(BF16)
                c2_scr[d, rows[j], :] = qk_uw[j][half, 0:GDN_K]
                bc_scr[d, srows, :] = ke_uw[j][d][:, 0:GDN_K]
                dg_scr[d, drows, :] = jnp.broadcast_to(jnp.exp(per[j][d][5]), (8, 128))
        return carry

    lax.fori_loop(0, nc // grp, prep, 0)

    chains = [(s, d) for s in range(nsub) for d in range(2)]
    for s, d in chains:
        state_scr[2 * s + d] = s0_ref[s, d, 0] if has_state else jnp.zeros((GDN_K, GDN_K), F32)

    def scan(i, carry):
        for s, d in chains:
            c = s * nc_seq + ((nc_seq - 1 - i) if d else i)
            rows = pl.ds(pl.multiple_of(c * L, L), L)
            srows = pl.ds(pl.multiple_of(c * GDN_K, GDN_K), GDN_K)
            drows = pl.ds(pl.multiple_of(c * 8, 8), 8)
            st = state_scr[2 * s + d]
            mrows = pl.ds(pl.multiple_of(c * (L + GDN_K), L + GDN_K), L + GDN_K)
            res = _dot(ma_scr[d, mrows, :], st.astype(BF16))
            oacc_scr[d, rows, :] = res[0:L, :] + c2_scr[d, rows, :]
            state_scr[2 * s + d] = dg_scr[d, drows, :][0:1, :] * st + res[L:, :] + bc_scr[d, srows, :]
        return carry

    lax.fori_loop(0, nc_seq, scan, 0)
    if not has_state:
        for s, d in chains:
            st_ref[s, d, 0] = state_scr[2 * s + d]

    o = oacc_scr[0] + oacc_scr[1]
    var = jnp.mean(o * o, axis=-1, keepdims=True)
    o_ref[...] = (o * lax.rsqrt(var + NORM_EPS) * ng_ref[...] * _silu(zb_ref[...])).astype(BF16)


def _gdn(proj, nseq, t, conv_w, bias_row, alog_row, norm_g, state, nsub=1):
    has_state = state is not None
    seq = lambda base: (lambda b, h: (b, base // 128 + h))
    const = lambda base: (lambda b, h: (0, base // 128 + h))
    rb = nsub * t
    in_specs = [
        pl.BlockSpec((rb, 128), seq(EV_Q)), pl.BlockSpec((rb, 128), seq(EV_K)), pl.BlockSpec((rb, 128), seq(EV_V)),
        pl.BlockSpec((rb, 128), seq(EV_ZB)), pl.BlockSpec((rb, 128), lambda b, h: (b, EV_SM // 128)),
        pl.BlockSpec((CONV_W, 128), const(0)), pl.BlockSpec((CONV_W, 128), const(1024)),
        pl.BlockSpec((CONV_W, 128), const(2048)),
        pl.BlockSpec((1, 128), lambda b, h: (0, 0)), pl.BlockSpec((1, 128), lambda b, h: (0, 0)),
        pl.BlockSpec((1, 128), lambda b, h: (0, 0)),
    ]
    args = [proj, proj, proj, proj, proj, conv_w, conv_w, conv_w, bias_row, alog_row, norm_g]
    st_block = (nsub, 2, 1, GDN_K, GDN_K)
    st_map = lambda b, h: (b, 0, h, 0, 0)
    out_specs = [pl.BlockSpec((rb, 128), lambda b, h: (b, h))]
    out_shape = [jax.ShapeDtypeStruct((nseq * t, GDN_HEADS * GDN_K), BF16)]
    if has_state:
        in_specs.append(pl.BlockSpec(st_block, st_map))
        args.append(state)
    else:
        out_specs.append(pl.BlockSpec(st_block, st_map))
        out_shape.append(jax.ShapeDtypeStruct((nseq, 2, GDN_HEADS, GDN_K, GDN_K), F32))
    nc = rb // L
    scratch = [pltpu.VMEM((t + 2 * CONV_PAD, 128), F32), pltpu.VMEM((rb, 128), F32), pltpu.VMEM((rb, 128), F32),
               pltpu.VMEM((rb, 128), F32), pltpu.VMEM((2, rb, 128), F32), pltpu.VMEM((2, rb, 128), F32),
               pltpu.VMEM((2, nc * (L + GDN_K), 128), BF16), pltpu.VMEM((2, rb, 128), F32),
               pltpu.VMEM((2, nc * GDN_K, GDN_K), F32),
               pltpu.VMEM((2, nc * 8, 128), F32),
               pltpu.VMEM((2, rb, 128), F32), pltpu.VMEM((2 * nsub, GDN_K, GDN_K), F32)]
    return pl.pallas_call(
        functools.partial(_gdn_kernel, t=t, nsub=nsub, has_state=has_state),
        grid=(nseq // nsub, GDN_HEADS), in_specs=in_specs, out_specs=out_specs, out_shape=out_shape,
        scratch_shapes=scratch, compiler_params=_cparams(("arbitrary", "arbitrary")), name="gdn",
    )(*args)


def _softmax_sink_pv(scores, sink, values):
    m = sink
    for s in scores:
        m = jnp.maximum(m, jnp.max(s, axis=-1, keepdims=True))
    den = jnp.exp(sink - m)
    ps = []
    for s in scores:
        p = jnp.exp(s - m)
        den = den + jnp.sum(p, axis=-1, keepdims=True)
        ps.append(p)
    out = None
    for p, v in zip(ps, values):
        term = _bdot(p / den, v)
        out = term if out is None else out + term
    return out


def _ctx_attn_kernel(sink_ref, q_ref, k_ref, v_ref, o_ref):
    kvh = pl.program_id(1)
    scale = ATT_DH ** -0.5
    kk, vv = k_ref[...], v_ref[...]
    for gq in range(ATT_GROUP):
        q = q_ref[:, gq * ATT_DH:(gq + 1) * ATT_DH]
        s = _bdot_nt(q, kk) * scale
        sink = sink_ref[kvh * ATT_GROUP + gq]
        o_ref[:, gq * ATT_DH:(gq + 1) * ATT_DH] = _softmax_sink_pv([s], sink, [vv]).astype(BF16)


def _ctx_attn(proj, nseq, t, sink):
    gw = ATT_GROUP * ATT_DH
    return pl.pallas_call(
        _ctx_attn_kernel,
        grid=(nseq, ATT_KVH),
        in_specs=[pl.BlockSpec(memory_space=pltpu.SMEM),
                  pl.BlockSpec((t, gw), lambda b, kv: (b, OD_Q // gw + kv)),
                  pl.BlockSpec((t, ATT_DH), lambda b, kv: (b, OD_K // ATT_DH + kv)),
                  pl.BlockSpec((t, ATT_DH), lambda b, kv: (b, OD_V // ATT_DH + kv))],
        out_specs=pl.BlockSpec((t, gw), lambda b, kv: (b, kv)),
        out_shape=jax.ShapeDtypeStruct((nseq * t, ATT_HEADS * ATT_DH), BF16),
        compiler_params=_cparams(("arbitrary", "arbitrary")), name="ctx_attn",
    )(sink, proj, proj, proj)


def _lat_attn_kernel(sink_ref, q_ref, k_ref, v_ref, kc_ref, vc_ref, cos_ref, sin_ref, o_ref, kp_scr, vp_scr, *, t):
    kvh = pl.program_id(1)
    scale = ATT_DH ** -0.5
    band = QBLOCK + 2 * WINDOW
    cos, sin = cos_ref[...], sin_ref[...]

    def rope(x):
        return x * cos + pltpu.roll(x, ATT_DH // 2, 1) * sin

    zeros = jnp.zeros((WINDOW, ATT_DH), F32)
    kp_scr[0:WINDOW, :] = zeros
    kp_scr[WINDOW + t:, :] = zeros
    kp_scr[WINDOW:WINDOW + t, :] = rope(k_ref[...])
    vp_scr[0:WINDOW, :] = zeros
    vp_scr[WINDOW + t:, :] = zeros
    vp_scr[WINDOW:WINDOW + t, :] = v_ref[...]
    kc, vc = kc_ref[...], vc_ref[...]

    def body(i, carry):
        start = pl.multiple_of(i * QBLOCK, QBLOCK)
        ki = kp_scr[pl.ds(start, band), :]
        vi = vp_scr[pl.ds(start, band), :]
        qpos = start + _iota((QBLOCK, band), 0)
        kpos = start - WINDOW + _iota((QBLOCK, band), 1)
        valid = (jnp.abs(qpos - kpos) <= WINDOW) & (kpos >= 0) & (kpos < t)
        cs = cos_ref[pl.ds(start, QBLOCK), :]
        sn = sin_ref[pl.ds(start, QBLOCK), :]
        for gq in range(ATT_GROUP):
            q = q_ref[pl.ds(start, QBLOCK), gq * ATT_DH:(gq + 1) * ATT_DH]
            q = q * cs + pltpu.roll(q, ATT_DH // 2, 1) * sn
            s_band = jnp.where(valid, _bdot_nt(q, ki) * scale, -jnp.inf)
            s_ctx = _bdot_nt(q, kc) * scale
            sink = sink_ref[kvh * ATT_GROUP + gq]
            o_ref[pl.ds(start, QBLOCK), gq * ATT_DH:(gq + 1) * ATT_DH] = _softmax_sink_pv(
                [s_band, s_ctx], sink, [vi, vc]).astype(BF16)
        return carry

    lax.fori_loop(0, t // QBLOCK, body, 0)


def _lat_attn(proj, nseq, t, sink, cache_k, cache_v, cos2, sin2):
    gw = ATT_GROUP * ATT_DH
    past = cache_k.shape[0] // nseq
    return pl.pallas_call(
        functools.partial(_lat_attn_kernel, t=t),
        grid=(nseq, ATT_KVH),
        in_specs=[pl.BlockSpec(memory_space=pltpu.SMEM),
                  pl.BlockSpec((t, gw), lambda b, kv: (b, OD_Q // gw + kv)),
                  pl.BlockSpec((t, ATT_DH), lambda b, kv: (b, OD_K // ATT_DH + kv)),
                  pl.BlockSpec((t, ATT_DH), lambda b, kv: (b, OD_V // ATT_DH + kv)),
                  pl.BlockSpec((past, ATT_DH), lambda b, kv: (b, kv)),
                  pl.BlockSpec((past, ATT_DH), lambda b, kv: (b, kv)),
                  pl.BlockSpec((t, ATT_DH), lambda b, kv: (0, 0)),
                  pl.BlockSpec((t, ATT_DH), lambda b, kv: (0, 0))],
        out_specs=pl.BlockSpec((t, gw), lambda b, kv: (b, kv)),
        out_shape=jax.ShapeDtypeStruct((nseq * t, ATT_HEADS * ATT_DH), BF16),
        scratch_shapes=[pltpu.VMEM((t + 2 * WINDOW, ATT_DH), F32), pltpu.VMEM((t + 2 * WINDOW, ATT_DH), F32)],
        compiler_params=_cparams(("arbitrary", "arbitrary")), name="lat_attn",
    )(sink, proj, proj, proj, cache_k, cache_v, cos2, sin2)


def _rope_tables(t):
    rows = t // GRID_W
    row = jnp.repeat(jnp.arange(rows), GRID_W).astype(F32)
    col = jnp.tile(jnp.arange(GRID_W), rows).astype(F32)
    n_freq = ATT_DH // 4
    inv = ROPE_BASE ** (-jnp.arange(n_freq, dtype=F32) / n_freq)
    ang = jnp.concatenate([row[:, None] * inv, col[:, None] * inv], axis=-1)
    cos, sin = jnp.cos(ang), jnp.sin(ang)
    return jnp.concatenate([cos, cos], axis=-1), jnp.concatenate([-sin, sin], axis=-1)


def _rwkv_kernel(*refs, t, nsub, has_state):
    (r_ref, k_ref, v_ref, wd_ref, ad_ref, gd_ref, mur_ref, muk_ref, muv_ref, muw_ref, mua_ref, mug_ref,
     w0_ref, wup_ref, a0_ref, aup_ref, gup_ref, kk_ref, ka_ref, rk_ref, lng_ref, lnb_ref) = refs[:22]
    k = 22
    s0_ref = st_ref = None
    if has_state:
        s0_ref = refs[k]
        k += 1
    o_ref = refs[k]
    k += 1
    if not has_state:
        st_ref = refs[k]
        k += 1
    (pad_scr, r_scr, v_scr, aa_scr, wl_scr, kd_scr, bb_scr, mah_scr, mal_scr, c2_scr, bc_scr, dg_scr,
     yacc_scr, state_scr) = refs[k:]
    nc_seq = t // L
    nc = nsub * nc_seq

    def shift_mix(x_ref, mu_ref):
        parts = []
        for s in range(nsub):
            x = x_ref[s * t:(s + 1) * t, :]
            zeros = jnp.zeros((CONV_PAD, 128), F32)
            pad_scr[0:CONV_PAD, :] = zeros
            pad_scr[CONV_PAD + t:, :] = zeros
            pad_scr[CONV_PAD:CONV_PAD + t, :] = x
            nb = 0.5 * (pad_scr[CONV_PAD - 1:CONV_PAD - 1 + t, :] + pad_scr[CONV_PAD + 1:CONV_PAD + 1 + t, :])
            parts.append(x + mu_ref[...] * (nb - x))
        return parts[0] if nsub == 1 else jnp.concatenate(parts, axis=0)

    r = shift_mix(r_ref, mur_ref)
    kx = shift_mix(k_ref, muk_ref)
    v = shift_mix(v_ref, muv_ref)
    wd = shift_mix(wd_ref, muw_ref)
    ad = shift_mix(ad_ref, mua_ref)
    gd = shift_mix(gd_ref, mug_ref)

    lane_sq = _iota((128, 128), 1)
    row_sq = _iota((128, 128), 0)
    same_head = ((lane_sq >= RW_K) == (row_sq >= RW_K))
    bd_ones = same_head.astype(F32)

    def head_sum(x):
        return _dot(jnp.concatenate(_split2(x), axis=1), jnp.concatenate([bd_ones, bd_ones], axis=0).astype(BF16))

    gate = _bdot(jax.nn.sigmoid(gd), gup_ref[...])
    kkv = kx * kk_ref[...]
    kkn = kkv * lax.rsqrt(head_sum(kkv * kkv) + NORM_EPS)
    tw = jnp.tanh(wd)
    rows_lo = _iota((128, 128), 0) < RW_K
    kd_sum = jnp.zeros((nsub * t, 128), F32)
    for d in range(2):
        half = rows_lo if d == 0 else jnp.logical_not(rows_lo)
        w = w0_ref[d:d + 1, :] + _bdot(tw, jnp.where(half, wup_ref[...], 0.0))
        wl_scr[d] = -jnp.exp(-_softplus(-w) - 0.5)
        a = jax.nn.sigmoid(a0_ref[d:d + 1, :] + _bdot(ad, jnp.where(half, aup_ref[...], 0.0)))
        kd = kx * (1.0 + (a - 1.0) * ka_ref[...])
        kd_scr[d] = kd
        kd_sum = kd_sum + kd
        bb_scr[d] = kkn * a
    r_scr[...] = r
    v_scr[...] = v
    aa_scr[...] = -kkn

    dup = ((_iota((RW_K, 128), 1) & (RW_K - 1)) == _iota((RW_K, 128), 0)).astype(F32)
    row_head1 = _iota((2 * L, 128), 0) >= L
    lane_head1 = _iota((2 * L, 128), 1) >= RW_K
    own_lanes = row_head1 == lane_head1
    sq_r, sq_c = _iota((2 * L, 2 * L), 0), _iota((2 * L, 2 * L), 1)
    same_blk = (sq_r >= L) == (sq_c >= L)
    pos_r, pos_c = sq_r & (L - 1), sq_c & (L - 1)
    ones_l = jnp.ones((L, 128), F32)
    bd_masks = [(same_blk & (pos_c <= pos_r), same_blk & (pos_c < pos_r)),
                (same_blk & (pos_c >= pos_r), same_blk & (pos_c > pos_r))]
    mm, mm_nt, mm_tn = _bdot, _bdot_nt, _bdot_tn
    grp = min(RW_GROUP, nc)

    def two(x):
        return jnp.where(own_lanes, jnp.concatenate([x, x], axis=0), 0.0)

    def prep(gi, carry):
        cs = [gi * grp + j for j in range(grp)]
        cds = [(j, d) for j in range(grp) for d in range(2)]
        rows = [pl.ds(pl.multiple_of(c * L, L), L) for c in cs]
        srows = [pl.ds(pl.multiple_of(c * 2 * L, 2 * L), 2 * L) for c in cs]
        vv2 = [two(v_scr[r, :]) for r in rows]
        wl = {(j, d): wl_scr[d, rows[j], :] for j, d in cds}
        cin = {jd: _dot01_l(_tri_masks(jd[1] == 1)[0].astype(F32), wl[jd]) for jd in cds}
        decay = {jd: jnp.exp(_dot01_tn(wl[jd], ones_l)) for jd in cds}
        ops = {}
        for j, d in cds:
            e_in, e_out = jnp.exp(cin[j, d]), jnp.exp(-cin[j, d])
            ops[j, d] = (two(aa_scr[rows[j], :] * jnp.exp(cin[j, d] - wl[j, d])), two(r_scr[rows[j], :] * e_in),
                         two(bb_scr[d, rows[j], :] * e_out), two(kd_scr[d, rows[j], :] * e_out))
        prod = {jd: mm_nt(jnp.concatenate(ops[jd][0:2], axis=0), jnp.concatenate(ops[jd][2:4], axis=0)) for jd in cds}
        blk = {}
        for j, d in cds:
            incl_bd, strict_bd = bd_masks[d]
            p = prod[j, d]
            blk[j, d] = (jnp.where(strict_bd, p[0:2 * L, 0:2 * L], 0.0), jnp.where(strict_bd, p[0:2 * L, 2 * L:], 0.0),
                         jnp.where(incl_bd, p[2 * L:, 0:2 * L], 0.0), jnp.where(incl_bd, p[2 * L:, 2 * L:], 0.0))
        packs = [-jnp.concatenate([blk[j, d][0][0:L, :] + blk[j, d][0][L:, :] for d in range(2)], axis=1)
                 for j in range(grp)]
        inv = _inv_unit_packed_multi(packs)
        t_inv = {(j, d): _to_bd(inv[j][:, d * 2 * L:(d + 1) * 2 * L]) for j, d in cds}
        kv = {(j, d): mm(jnp.concatenate(blk[j, d][1::2], axis=0), vv2[j]) for j, d in cds}
        m1c1 = {jd: mm(t_inv[jd], jnp.concatenate([ops[jd][0], kv[jd][0:2 * L, :]], axis=1)) for jd in cds}
        rm = {jd: mm(blk[jd][2], m1c1[jd]) for jd in cds}
        bm = {jd: mm_tn(ops[jd][2], m1c1[jd]) for jd in cds}
        kv2 = {(j, d): mm_tn(ops[j, d][3], vv2[j]) for j, d in cds}
        for j, d in cds:
            m2ah = jnp.concatenate([ops[j, d][1] + rm[j, d][:, 0:128], decay[j, d] * bm[j, d][:, 0:128]], axis=0)
            hi, lo = _split2(m2ah)
            mrows = pl.ds(pl.multiple_of(cs[j] * 4 * L, 4 * L), 4 * L)
            mah_scr[d, mrows, :] = hi
            mal_scr[d, mrows, :] = lo
            c2_scr[d, srows[j], :] = rm[j, d][:, 128:] + kv[j, d][2 * L:, :]
            bc_scr[d, srows[j], :] = decay[j, d] * (bm[j, d][:, 128:] + kv2[j, d])
            dg_scr[d, srows[j], :] = decay[j, d]
        return carry

    lax.fori_loop(0, nc // grp, prep, 0)

    chains = [(s, d) for s in range(nsub) for d in range(2)]
    for s, d in chains:
        if has_state:
            stacked = s0_ref[s, d].reshape(2 * RW_K, RW_K)
            state_scr[2 * s + d] = jnp.where(same_head, _dot01_r(stacked, dup), 0.0).T
        else:
            state_scr[2 * s + d] = jnp.zeros((128, 128), F32)

    def scan(i, carry):
        for s, d in chains:
            c = s * nc_seq + ((nc_seq - 1 - i) if d else i)
            rows = pl.ds(pl.multiple_of(c * L, L), L)
            srows = pl.ds(pl.multiple_of(c * 2 * L, 2 * L), 2 * L)
            h_st = state_scr[2 * s + d]
            mrows = pl.ds(pl.multiple_of(c * 4 * L, 4 * L), 4 * L)
            lh, ll = mah_scr[d, mrows, :], mal_scr[d, mrows, :]
            hh, hl = _split2(h_st)
            res = _dot(lh, hh)
            corr = res[2 * L:, :] + _dot(lh[2 * L:, :], hl) + _dot(ll[2 * L:, :], hh)
            y2 = res[0:2 * L, :] + c2_scr[d, srows, :]
            yacc_scr[d, rows, :] = y2[0:L, :] + y2[L:, :]
            state_scr[2 * s + d] = dg_scr[d, srows, :] * h_st + corr + bc_scr[d, srows, :]
        return carry

    lax.fori_loop(0, nc_seq, scan, 0)
    if not has_state:
        dup_t = ((_iota((128, RW_K), 0) & (RW_K - 1)) == _iota((128, RW_K), 1)).astype(F32)
        for s, d in chains:
            st_ref[s, d] = _dot01_r(state_scr[2 * s + d].T, dup_t).reshape(2, RW_K, RW_K)

    y = yacc_scr[0] + yacc_scr[1]
    mean = head_sum(y) * (1.0 / RW_K)
    yc = y - mean
    var = head_sum(yc * yc) * (1.0 / RW_K)
    yn = yc * lax.rsqrt(var + RWKV_LN_EPS) * lng_ref[...] + lnb_ref[...]
    bonus = head_sum(r * (0.5 * kd_sum) * rk_ref[...]) * v
    o_ref[...] = ((yn + bonus) * gate).astype(BF16)


def _rwkv(proj, nseq, t, prm, state, nsub=1):
    has_state = state is not None
    mu, w0, wup, a0, aup, gup, k_k, k_a, r_k, ln_g, ln_b = prm
    seq = lambda base: (lambda b, hp: (b, base // 128 + hp))
    fix = lambda base: (lambda b, hp: (b, base // 128))
    mu_at = lambda base: (lambda b, hp: (0, (base - OD_R) // 128 + hp))
    mu_fix = lambda base: (lambda b, hp: (0, (base - OD_R) // 128))
    col = lambda b, hp: (0, hp)
    rb = nsub * t
    in_specs = [
        pl.BlockSpec((rb, 128), seq(OD_R)), pl.BlockSpec((rb, 128), seq(OD_RK)), pl.BlockSpec((rb, 128), seq(OD_RV)),
        pl.BlockSpec((rb, 128), fix(OD_WD)), pl.BlockSpec((rb, 128), fix(OD_AD)), pl.BlockSpec((rb, 128), fix(OD_GD)),
        pl.BlockSpec((1, 128), mu_at(OD_R)), pl.BlockSpec((1, 128), mu_at(OD_RK)), pl.BlockSpec((1, 128), mu_at(OD_RV)),
        pl.BlockSpec((1, 128), mu_fix(OD_WD)), pl.BlockSpec((1, 128), mu_fix(OD_AD)), pl.BlockSpec((1, 128), mu_fix(OD_GD)),
        pl.BlockSpec((2, 128), col), pl.BlockSpec((128, 128), col),
        pl.BlockSpec((2, 128), col), pl.BlockSpec((128, 128), col),
        pl.BlockSpec((128, 128), col),
        pl.BlockSpec((1, 128), col), pl.BlockSpec((1, 128), col), pl.BlockSpec((1, 128), col),
        pl.BlockSpec((1, 128), col), pl.BlockSpec((1, 128), col),
    ]
    args = [proj] * 6 + [mu] * 6 + [w0, wup, a0, aup, gup, k_k, k_a, r_k, ln_g, ln_b]
    st_block = (nsub, 2, 2, RW_K, RW_K)
    st_map = lambda b, hp: (b, 0, hp, 0, 0)
    out_specs = [pl.BlockSpec((rb, 128), lambda b, hp: (b, hp))]
    out_shape = [jax.ShapeDtypeStruct((nseq * t, RW_HEADS * RW_K), BF16)]
    if has_state:
        in_specs.append(pl.BlockSpec(st_block, st_map))
        args.append(state)
    else:
        out_specs.append(pl.BlockSpec(st_block, st_map))
        out_shape.append(jax.ShapeDtypeStruct((nseq, 2, RW_HEADS, RW_K, RW_K), F32))
    scratch = [pltpu.VMEM((t + 2 * CONV_PAD, 128), F32), pltpu.VMEM((rb, 128), F32), pltpu.VMEM((rb, 128), F32),
               pltpu.VMEM((rb, 128), F32), pltpu.VMEM((2, rb, 128), F32), pltpu.VMEM((2, rb, 128), F32),
               pltpu.VMEM((2, rb, 128), F32)] + [pltpu.VMEM((2, 4 * rb, 128), BF16)] * 2 + [
               pltpu.VMEM((2, 2 * rb, 128), F32)] * 3 + [
               pltpu.VMEM((2, rb, 128), F32), pltpu.VMEM((2 * nsub, 128, 128), F32)]
    return pl.pallas_call(
        functools.partial(_rwkv_kernel, t=t, nsub=nsub, has_state=has_state),
        grid=(nseq // nsub, RW_HEADS // 2), in_specs=in_specs, out_specs=out_specs, out_shape=out_shape,
        scratch_shapes=scratch, compiler_params=_cparams(("arbitrary", "arbitrary")), name="rwkv",
    )(*args)


def _even_w_in(w):
    za, xs, bm, cm = w[:, 0:1024], w[:, 1024:2048], w[:, 2048:2304], w[:, 2304:2560]
    dt, q, kk, v = w[:, 2560:2592], w[:, 2592:3616], w[:, 3616:4640], w[:, 4640:5664]
    zb, al, be = w[:, 5664:6688], w[:, 6688:6704], w[:, 6704:6720]
    pad = jnp.zeros((w.shape[0], EV_COLS - EV_SM - 64), w.dtype)
    return jnp.concatenate([za, xs, zb, q, kk, v, bm, cm, dt, al, be, pad], axis=1)


def _pad_row(parts):
    flat = jnp.concatenate([p.reshape(-1) for p in parts])
    return jnp.pad(flat, (0, 128 - flat.shape[0])).reshape(1, 128)


def kernel(x_prompt, x_sample, state_ssd, state_gdn, cache_k, cache_v, state_rwkv, c, c_ctx, ada_w, ada_b, norm1_g, norm2_g, ffn_w1, ffn_w2, final_g, ev_w_in, ev_w_out, ssd_conv_w, ssd_conv_b, ssd_a_log, ssd_dt_bias, ssd_d, ssd_norm_g, gdn_conv_w, gdn_a_log, gdn_dt_bias, gdn_norm_g, od_w_in, od_w_out, attn_sink, rwkv_mu, rwkv_w0, rwkv_w_up, rwkv_a0, rwkv_a_up, rwkv_g_up, rwkv_k_k, rwkv_k_a, rwkv_r_k, rwkv_ln_g, rwkv_ln_b):
    bp, tp, _ = x_prompt.shape
    bs, ts, _ = x_sample.shape
    streams = [(x_prompt.reshape(bp * tp, D), bp, tp, None), (x_sample.reshape(bs * ts, D), bs, ts, ts)]

    cvec = jnp.concatenate([c_ctx[None, :], c, jnp.zeros((16 - 1 - bs, D), F32)], axis=0)
    mod = _adaln(cvec, ada_w, ada_b)

    w_in0 = _even_w_in(ev_w_in[0].astype(BF16))
    w_out0 = ev_w_out[0].astype(BF16)
    bias_row = _pad_row([ssd_dt_bias[0], gdn_dt_bias[0]])
    alog_row = _pad_row([ssd_a_log[0], gdn_a_log[0]])
    dskip_row = jnp.repeat(ssd_d[0], SSD_P).reshape(1, SSD_HEADS * SSD_P)
    w1_0, w2_0 = ffn_w1[0].astype(BF16), ffn_w2[0].astype(BF16)
    w_in1 = od_w_in[0].astype(BF16)
    w_out1 = od_w_out[0].astype(BF16)
    w1_1, w2_1 = ffn_w1[1].astype(BF16), ffn_w2[1].astype(BF16)
    rw_prm = (rwkv_mu[0].reshape(1, -1), rwkv_w0[0], rwkv_w_up[0].reshape(2 * 64, -1), rwkv_a0[0],
              rwkv_a_up[0].reshape(2 * 64, -1), rwkv_g_up[0], rwkv_k_k[0].reshape(1, -1), rwkv_k_a[0].reshape(1, -1),
              rwkv_r_k[0].reshape(1, -1), rwkv_ln_g[0].reshape(1, -1), rwkv_ln_b[0].reshape(1, -1))
    cos2, sin2 = _rope_tables(ts)
    past = cache_k.shape[2]
    ck = cache_k[:, 0].reshape(bs * past, ATT_KVH * ATT_DH)
    cv = cache_v[:, 0].reshape(bs * past, ATT_KVH * ATT_DH)

    outs = []
    for x, nseq, t, sample_t in streams:
        latent = sample_t is not None
        m0 = mod[0]
        proj = _inproj(x, norm1_g[0:1], m0, w_in0, sample_t)
        s_ssd = state_ssd[:, 0] if latent else None
        s_gdn = state_gdn[:, 0] if latent else None
        r_ssd = _ssd(proj, nseq, t, ssd_conv_w[0], ssd_conv_b[0].reshape(1, -1), bias_row, alog_row, dskip_row,
                     ssd_norm_g[0].reshape(1, -1), s_ssd)
        r_gdn = _gdn(proj, nseq, t, gdn_conv_w[0], bias_row, alog_row, gdn_norm_g[0].reshape(1, -1), s_gdn,
                     nsub=1 if latent else GDN_PROMPT_NSUB)
        x = _mix_ffn(r_ssd[0], r_gdn[0], x, norm2_g[0:1], m0, w_out0, w1_0, w2_0, final_g.reshape(1, D), sample_t,
                     final=False)
        m1 = mod[1]
        proj = _inproj(x, norm1_g[1:2], m1, w_in1, sample_t)
        if latent:
            att = _lat_attn(proj, nseq, t, attn_sink[0], ck, cv, cos2, sin2)
            r_rw = _rwkv(proj, nseq, t, rw_prm, state_rwkv[:, 0])
        else:
            att = _ctx_attn(proj, nseq, t, attn_sink[0])
            r_rw = _rwkv(proj, nseq, t, rw_prm, None, nsub=RW_PROMPT_NSUB)
        x = _mix_ffn(att, r_rw[0], x, norm2_g[1:2], m1, w_out1, w1_1, w2_1, final_g.reshape(1, D), sample_t,
                     final=True)
        outs.append((x.reshape(nseq, t, D), r_ssd, r_gdn, proj, r_rw))

    (y_prompt, p_ssd, p_gdn, p_proj, p_rw), (y_sample, _, _, _, _) = outs
    new_k = p_proj[:, OD_K:OD_V].reshape(bp, 1, tp, ATT_KVH, ATT_DH)
    new_v = p_proj[:, OD_V:OD_R].reshape(bp, 1, tp, ATT_KVH, ATT_DH)
    return (y_prompt, y_sample, p_ssd[1][:, None], p_gdn[1][:, None], new_k, new_v, p_rw[1][:, None])
```

```python
import functools

import jax
import jax.numpy as jnp
from jax import lax
from jax.experimental import pallas as pl
from jax.experimental.pallas import tpu as pltpu

F32 = jnp.float32
BF16 = jnp.bfloat16

D = 1024
NORM_EPS = 1e-6
L = 64
GDN_GROUP, RW_GROUP = 8, 8
GDN_PROMPT_NSUB, RW_PROMPT_NSUB = 4, 2
CONV_W = 5
CONV_PAD = 8

SSD_HEADS, SSD_P, SSD_N, SSD_GROUPS = 16, 64, 128, 2
SSD_GW = SSD_HEADS // SSD_GROUPS * SSD_P
GDN_HEADS, GDN_K = 8, 128
ATT_HEADS, ATT_KVH, ATT_DH, ATT_GROUP = 8, 2, 128, 4
WINDOW, QBLOCK, GRID_W, ROPE_BASE = 128, 128, 64, 10000.0
RW_HEADS, RW_K = 16, 64
RWKV_LN_EPS = 64e-5
FFN_H = 2816
FFN_TN = 1408

EV_ZA, EV_XS, EV_ZB, EV_Q, EV_K, EV_V, EV_B, EV_C, EV_SM, EV_COLS = 0, 1024, 2048, 3072, 4096, 5120, 6144, 6400, 6656, 6784
SM_DT, SM_ALPHA, SM_BETA = 0, 32, 48
OD_Q, OD_K, OD_V, OD_R, OD_RK, OD_RV, OD_WD, OD_AD, OD_GD, OD_COLS = 0, 1024, 1280, 1536, 2560, 3584, 4608, 4736, 4864, 4992

VMEM_LIMIT = 56 * 1024 * 1024


def _cparams(sem):
    return pltpu.CompilerParams(dimension_semantics=sem, vmem_limit_bytes=VMEM_LIMIT)


def _dot(a, b, prec=None):
    return jnp.dot(a, b, preferred_element_type=F32, precision=prec)


def _dot_nt(a, b, prec=None):
    return lax.dot_general(a, b, (((1,), (1,)), ((), ())), preferred_element_type=F32, precision=prec)


def _dot_tn(a, b, prec=None):
    return lax.dot_general(a, b, (((0,), (0,)), ((), ())), preferred_element_type=F32, precision=prec)


def _bdot(a, b):
    return _dot(a.astype(BF16), b.astype(BF16))


def _bdot_nt(a, b):
    return _dot_nt(a.astype(BF16), b.astype(BF16))


def _bdot_tn(a, b):
    return _dot_tn(a.astype(BF16), b.astype(BF16))


def _silu(x):
    return x * jax.nn.sigmoid(x)


def _softplus(x):
    return jnp.maximum(x, 0.0) + jnp.log(1.0 + jnp.exp(-jnp.abs(x)))


def _iota(shape, dim):
    return lax.broadcasted_iota(jnp.int32, shape, dim)


def _tri_masks(rev):
    r, c = _iota((L, L), 0), _iota((L, L), 1)
    if rev:
        return c >= r, c > r
    return c <= r, c < r


def _split3(x):
    hi = x.astype(BF16)
    r1 = x - hi.astype(F32)
    mid = r1.astype(BF16)
    lo = (r1 - mid.astype(F32)).astype(BF16)
    return hi, mid, lo


def _dot01_l(m01, x):
    mb = m01.astype(BF16)
    return _dot(jnp.concatenate([mb, mb, mb], axis=1), jnp.concatenate(_split3(x), axis=0))


def _dot01_r(x, m01):
    mb = m01.astype(BF16)
    return _dot(jnp.concatenate(_split3(x), axis=1), jnp.concatenate([mb, mb, mb], axis=0))


def _dot01_tn(x, m01):
    mb = m01.astype(BF16)
    return _dot_tn(jnp.concatenate(_split3(x), axis=0), jnp.concatenate([mb, mb, mb], axis=0))


def _to_bd(m):
    n = m.shape[1] // L
    blk = jnp.right_shift(_iota(m.shape, 1), L.bit_length() - 1)
    return jnp.concatenate([jnp.where(blk == j, m, 0.0) for j in range(n)], axis=0)


def _split2(x):
    hi = x.astype(BF16)
    return hi, (x - hi.astype(F32)).astype(BF16)


def _inv_unit_packed_multi(mats):
    eye = ((_iota(mats[0].shape, 1) & (L - 1)) == _iota(mats[0].shape, 0)).astype(F32)

    def mm3(lh, ll, ph, pl_):
        bh, bl = _to_bd(ph), _to_bd(pl_)
        return _dot(lh, bh) + _dot(lh, bl) + _dot(ll, bh)

    xs = [eye - a for a in mats]
    ps = [mm3(h, lo, h, lo) for h, lo in map(_split2, mats)]
    n = 2
    while True:
        sp = [_split2(p) for p in ps]
        sx = [_split2(x) for x in xs]
        n *= 2
        if n >= L:
            return [x + mm3(xh, xl, ph, pl_) for x, (xh, xl), (ph, pl_) in zip(xs, sx, sp)]
        res = [mm3(jnp.concatenate([xh, ph], axis=0), jnp.concatenate([xl, pl_], axis=0), ph, pl_)
               for (xh, xl), (ph, pl_) in zip(sx, sp)]
        xs = [x + r[0:L, :] for x, r in zip(xs, res)]
        ps = [r[L:, :] for r in res]


def _conv_silu(x, w_ref, b, pad_ref, t):
    c = x.shape[1]
    zeros = jnp.zeros((CONV_PAD, c), F32)
    pad_ref[0:CONV_PAD, 0:c] = zeros
    pad_ref[CONV_PAD + t:2 * CONV_PAD + t, 0:c] = zeros
    pad_ref[CONV_PAD:CONV_PAD + t, 0:c] = x
    acc = None
    for j in range(CONV_W):
        off = CONV_PAD - CONV_W // 2 + j
        term = pad_ref[off:off + t, 0:c] * w_ref[j:j + 1, :]
        acc = term if acc is None else acc + term
    if b is not None:
        acc = acc + b
    return _silu(acc)


def _adaln_kernel(c_ref, w_ref, b_ref, o_ref):
    c = c_ref[...]
    o_ref[0] = _bdot(_silu(c), w_ref[0]) + b_ref[0]


def _adaln(cvec, ada_w, ada_b):
    depth, _, n = ada_w.shape
    tn = 1536
    out = pl.pallas_call(
        _adaln_kernel,
        grid=(depth, n // tn),
        in_specs=[pl.BlockSpec((16, D), lambda l, j: (0, 0)),
                  pl.BlockSpec((1, D, tn), lambda l, j: (l, 0, j)),
                  pl.BlockSpec((1, 1, tn), lambda l, j: (l, 0, j))],
        out_specs=pl.BlockSpec((1, 16, tn), lambda l, j: (l, 0, j)),
        out_shape=jax.ShapeDtypeStruct((depth, 16, n), F32),
        compiler_params=_cparams(("arbitrary", "arbitrary")),
        name="adaln",
    )(cvec, ada_w, ada_b.reshape(depth, 1, n))
    return out.reshape(depth, 16, 6, D)


def _mod_row_map(tm, t):
    if t is None:
        return lambda i: (0, 0, 0)
    per = t // tm
    return lambda i: (1 + i // per, 0, 0)


def _norm_mod(x, g, m, shift_idx, scale_idx):
    var = jnp.mean(x * x, axis=-1, keepdims=True)
    y = x * lax.rsqrt(var + NORM_EPS) * g
    return y * (1.0 + m[scale_idx:scale_idx + 1]) + m[shift_idx:shift_idx + 1]


def _inproj_kernel(x_ref, g_ref, mod_ref, w_ref, o_ref, *, tn):
    h = _norm_mod(x_ref[...], g_ref[...], mod_ref[0], 0, 1).astype(BF16)
    n = w_ref.shape[1]
    for j in range(0, n, tn):
        w = min(tn, n - j)
        o_ref[:, j:j + w] = _dot(h, w_ref[:, j:j + w])


def _inproj(x, g, mod, w, sample_t, tm=256, tn=512):
    rows, n = x.shape[0], w.shape[1]
    return pl.pallas_call(
        functools.partial(_inproj_kernel, tn=tn),
        grid=(rows // tm,),
        in_specs=[pl.BlockSpec((tm, D), lambda i: (i, 0)),
                  pl.BlockSpec((1, D), lambda i: (0, 0)),
                  pl.BlockSpec((1, 6, D), _mod_row_map(tm, sample_t)),
                  pl.BlockSpec((D, n), lambda i: (0, 0))],
        out_specs=pl.BlockSpec((tm, n), lambda i: (i, 0)),
        out_shape=jax.ShapeDtypeStruct((rows, n), F32),
        compiler_params=_cparams(("arbitrary",)),
        name="inproj",
    )(x, g, mod, w)


def _mix_ffn_kernel(a1_ref, a2_ref, x_ref, g_ref, mod_ref, wo_ref, w1_ref, w2_ref, fg_ref, o_ref, *, final):
    m = mod_ref[0]
    k1 = a1_ref.shape[1]
    mix = _dot(a1_ref[...], wo_ref[0:k1, :]) + _dot(a2_ref[...], wo_ref[k1:, :])
    x = x_ref[...] + m[2:3] * mix
    h = _norm_mod(x, g_ref[...], m, 3, 4).astype(BF16)
    acc = jnp.zeros(x.shape, F32)
    for j in range(0, FFN_H, FFN_TN):
        gate = _dot(h, w1_ref[:, j:j + FFN_TN])
        up = _dot(h, w1_ref[:, FFN_H + j:FFN_H + j + FFN_TN])
        acc = acc + _dot((_silu(gate) * up).astype(BF16), w2_ref[j:j + FFN_TN, :])
    y = x + m[5:6] * acc
    if final:
        var = jnp.mean(y * y, axis=-1, keepdims=True)
        y = y * lax.rsqrt(var + NORM_EPS) * fg_ref[...]
    o_ref[...] = y


def _mix_ffn(a1, a2, x, g, mod, w_out, w1, w2, final_g, sample_t, final, tm=256):
    rows = x.shape[0]
    once = dict(pipeline_mode=pl.Buffered(1))
    return pl.pallas_call(
        functools.partial(_mix_ffn_kernel, final=final),
        grid=(rows // tm,),
        in_specs=[pl.BlockSpec((tm, a1.shape[1]), lambda i: (i, 0)),
                  pl.BlockSpec((tm, a2.shape[1]), lambda i: (i, 0)),
                  pl.BlockSpec((tm, D), lambda i: (i, 0)),
                  pl.BlockSpec((1, D), lambda i: (0, 0)),
                  pl.BlockSpec((1, 6, D), _mod_row_map(tm, sample_t)),
                  pl.BlockSpec(w_out.shape, lambda i: (0, 0), **once),
                  pl.BlockSpec(w1.shape, lambda i: (0, 0), **once),
                  pl.BlockSpec(w2.shape, lambda i: (0, 0), **once),
                  pl.BlockSpec((1, D), lambda i: (0, 0))],
        out_specs=pl.BlockSpec((tm, D), lambda i: (i, 0)),
        out_shape=jax.ShapeDtypeStruct((rows, D), F32),
        compiler_params=_cparams(("arbitrary",)),
        name="mix_ffn",
    )(a1, a2, x, g, mod, w_out, w1, w2, final_g)


def _ssd_kernel(*refs, t, has_state):
    (za_ref, xs_ref, bm_ref, cm_ref, sm_ref, cwx_ref, cwb_ref, cwc_ref, cbx_ref, cbb_ref, cbc_ref,
     bias_ref, alog_ref, dsk_ref, ng_ref) = refs[:15]
    k = 15
    s0_ref = st_ref = None
    if has_state:
        s0_ref = refs[k]
        k += 1
    y_ref = refs[k]
    k += 1
    if not has_state:
        st_ref = refs[k]
        k += 1
    pad_scr, xs_scr, bb_scr, cc_scr, dt_scr, da_scr, dasm_scr, yacc_scr, state_scr = refs[k:]

    g = pl.program_id(1)
    nc = t // L
    npair = SSD_GW // 128

    xs_scr[...] = _conv_silu(xs_ref[...], cwx_ref, cbx_ref[...], pad_scr, t)
    bb_scr[...] = _conv_silu(bm_ref[...], cwb_ref, cbb_ref[...], pad_scr, t)
    cc_scr[...] = _conv_silu(cm_ref[...], cwc_ref, cbc_ref[...], pad_scr, t)

    dtv = _softplus(sm_ref[...] + bias_ref[...])
    dav = dtv * (-jnp.exp(alog_ref[...]))
    heads_per_group = SSD_HEADS // SSD_GROUPS
    for d in range(2):
        first = SM_DT + d * SSD_HEADS + g * heads_per_group
        src = _iota((128, SSD_GW), 0)
        e = (src == first + jnp.right_shift(_iota((128, SSD_GW), 1), 6)).astype(F32)
        dt_scr[d] = _dot01_r(dtv, e)
        da_scr[d] = _dot01_r(dav, e)
        sel = (_iota((128, 128), 0) == first + _iota((128, 128), 1)) & (_iota((128, 128), 1) < heads_per_group)
        dasm_scr[d] = _dot01_r(dav, sel.astype(F32))

    lane128 = _iota((L, 128), 1)
    for d in range(2):
        if has_state:
            state_scr[d] = s0_ref[0, d].reshape(SSD_GW, SSD_N).T
        else:
            state_scr[d] = jnp.zeros((SSD_N, SSD_GW), F32)

    def body(ci, carry):
        for d in range(2):
            rev = d == 1
            incl = _tri_masks(rev)[0]
            tri = incl.astype(F32)
            tri_t = _tri_masks(not rev)[0].astype(F32)
            last = 0 if rev else L - 1
            c = (nc - 1 - ci) if rev else ci
            r0 = pl.multiple_of(c * L, L)
            rows = pl.ds(r0, L)
            x = xs_scr[rows, :]
            bc = bb_scr[rows, :]
            cc = cc_scr[rows, :]
            acum = _dot01_l(tri, da_scr[d, rows, :])
            acum_t = _dot01_tn(dasm_scr[d, rows, :], tri_t)
            xdt = x * dt_scr[d, rows, :]
            cb = _bdot_nt(cc, bc)
            total = acum[last:last + 1, :]
            s_in = state_scr[d]
            y_inter = _bdot(cc, s_in) * jnp.exp(acum)
            pieces = []
            for j in range(npair):
                xp = xdt[:, j * 128:(j + 1) * 128]
                yp = None
                for hh in range(2):
                    h = 2 * j + hh
                    col = acum[:, h * SSD_P:h * SSD_P + 1]
                    row = acum_t[h:h + 1, :]
                    dec = jnp.where(incl, jnp.exp(jnp.minimum(col - row, 0.0)), 0.0)
                    mine = (lane128 >= SSD_P) if hh else (lane128 < SSD_P)
                    term = _bdot(cb * dec, jnp.where(mine, xp, 0.0))
                    yp = term if yp is None else yp + term
                pieces.append(yp)
            yacc_scr[d, rows, :] = jnp.concatenate(pieces, axis=1) + y_inter
            wx = jnp.exp(total - acum) * xdt
            state_scr[d] = s_in * jnp.exp(total) + _bdot_tn(bc, wx)
        return carry

    lax.fori_loop(0, nc, body, 0)
    if not has_state:
        for d in range(2):
            st_ref[0, d] = state_scr[d].T.reshape(heads_per_group, SSD_P, SSD_N)

    y = yacc_scr[0] + yacc_scr[1] + dsk_ref[...] * xs_scr[...]
    y = y * _silu(za_ref[...])
    var = jnp.mean(y * y, axis=-1, keepdims=True)
    y_ref[...] = (y * lax.rsqrt(var + NORM_EPS) * ng_ref[...]).astype(BF16)


def _ssd(proj, nseq, t, conv_w, conv_b, bias_row, alog_row, dskip_row, norm_g, state):
    has_state = state is not None
    seq = lambda blk: (lambda b, g: (b, blk(g)))
    const = lambda blk: (lambda b, g: (0, blk(g)))
    in_specs = [
        pl.BlockSpec((t, SSD_GW), seq(lambda g: EV_ZA // SSD_GW + g)),
        pl.BlockSpec((t, SSD_GW), seq(lambda g: EV_XS // SSD_GW + g)),
        pl.BlockSpec((t, 128), seq(lambda g: EV_B // 128 + g)),
        pl.BlockSpec((t, 128), seq(lambda g: EV_C // 128 + g)),
        pl.BlockSpec((t, 128), seq(lambda g: EV_SM // 128)),
        pl.BlockSpec((CONV_W, SSD_GW), const(lambda g: g)),
        pl.BlockSpec((CONV_W, 128), const(lambda g: 1024 // 128 + g)),
        pl.BlockSpec((CONV_W, 128), const(lambda g: 1280 // 128 + g)),
        pl.BlockSpec((1, SSD_GW), const(lambda g: g)),
        pl.BlockSpec((1, 128), const(lambda g: 1024 // 128 + g)),
        pl.BlockSpec((1, 128), const(lambda g: 1280 // 128 + g)),
        pl.BlockSpec((1, 128), const(lambda g: 0)),
        pl.BlockSpec((1, 128), const(lambda g: 0)),
        pl.BlockSpec((1, SSD_GW), const(lambda g: g)),
        pl.BlockSpec((1, SSD_GW), const(lambda g: g)),
    ]
    args = [proj, proj, proj, proj, proj, conv_w, conv_w, conv_w, conv_b, conv_b, conv_b,
            bias_row, alog_row, dskip_row, norm_g]
    st_block = (1, 2, SSD_HEADS // SSD_GROUPS, SSD_P, SSD_N)
    st_map = lambda b, g: (b, 0, g, 0, 0)
    out_specs = [pl.BlockSpec((t, SSD_GW), lambda b, g: (b, g))]
    out_shape = [jax.ShapeDtypeStruct((nseq * t, SSD_HEADS * SSD_P), BF16)]
    if has_state:
        in_specs.append(pl.BlockSpec(st_block, st_map))
        args.append(state)
    else:
        out_specs.append(pl.BlockSpec(st_block, st_map))
        out_shape.append(jax.ShapeDtypeStruct((nseq, 2, SSD_HEADS, SSD_P, SSD_N), F32))
    scratch = [pltpu.VMEM((t + 2 * CONV_PAD, SSD_GW), F32), pltpu.VMEM((t, SSD_GW), F32),
               pltpu.VMEM((t, 128), F32), pltpu.VMEM((t, 128), F32),
               pltpu.VMEM((2, t, SSD_GW), F32), pltpu.VMEM((2, t, SSD_GW), F32), pltpu.VMEM((2, t, 128), F32),
               pltpu.VMEM((2, t, SSD_GW), F32), pltpu.VMEM((2, SSD_N, SSD_GW), F32)]
    return pl.pallas_call(
        functools.partial(_ssd_kernel, t=t, has_state=has_state),
        grid=(nseq, SSD_GROUPS), in_specs=in_specs, out_specs=out_specs, out_shape=out_shape,
        scratch_shapes=scratch, compiler_params=_cparams(("arbitrary", "arbitrary")), name="ssd",
    )(*args)


def _gdn_kernel(*refs, t, nsub, has_state):
    (q_ref, k_ref, v_ref, zb_ref, sm_ref, cwq_ref, cwk_ref, cwv_ref, bias_ref, alog_ref, ng_ref) = refs[:11]
    k = 11
    s0_ref = st_ref = None
    if has_state:
        s0_ref = refs[k]
        k += 1
    o_ref = refs[k]
    k += 1
    if not has_state:
        st_ref = refs[k]
        k += 1
    (pad_scr, q_scr, k_scr, v_scr, g_scr, b_scr, ma_scr, c2_scr, bc_scr, dg_scr,
     oacc_scr, state_scr) = refs[k:]

    h = pl.program_id(1)
    nc_seq = t // L
    nc = nsub * nc_seq
    rows_all = nsub * t

    def l2n(x):
        return x * lax.rsqrt(jnp.sum(x * x, axis=-1, keepdims=True) + NORM_EPS)

    for s in range(nsub):
        rs = slice(s * t, (s + 1) * t)
        q_scr[rs, :] = l2n(_conv_silu(q_ref[rs, :], cwq_ref, None, pad_scr, t)) * (GDN_K ** -0.5)
        k_scr[rs, :] = l2n(_conv_silu(k_ref[rs, :], cwk_ref, None, pad_scr, t))
        v_scr[rs, :] = _conv_silu(v_ref[rs, :], cwv_ref, None, pad_scr, t)

    sm = sm_ref[...]
    gl = -jnp.exp(alog_ref[...]) * _softplus(sm + bias_ref[...])
    bt = jax.nn.sigmoid(sm)
    lane = _iota((rows_all, 128), 1)
    for d in range(2):
        gcol = jnp.sum(jnp.where(lane == SM_ALPHA + d * GDN_HEADS + h, gl, 0.0), axis=1, keepdims=True)
        bcol = jnp.sum(jnp.where(lane == SM_BETA + d * GDN_HEADS + h, bt, 0.0), axis=1, keepdims=True)
        g_scr[d] = jnp.broadcast_to(gcol, (rows_all, 128))
        b_scr[d] = jnp.broadcast_to(bcol, (rows_all, 128))

    masks = [_tri_masks(False), _tri_masks(True)]
    grp = min(GDN_GROUP, nc)

    def prep(gi, carry):
        cs = [gi * grp + j for j in range(grp)]
        rows = [pl.ds(pl.multiple_of(c * L, L), L) for c in cs]
        qkv = [(q_scr[r, :], k_scr[r, :], v_scr[r, :]) for r in rows]
        qkk = [_bdot_nt(jnp.concatenate([qc, kc], axis=0), kc) for qc, kc, _ in qkv]
        gbs = [[(g_scr[d, r, :], b_scr[d, r, :]) for d in range(2)] for r in rows]
        gam = [[_dot01_l(masks[d][0].astype(F32), gbs[j][d][0]) for d in range(2)] for j in range(grp)]
        gam_row = [[_dot01_tn(gbs[j][d][0], masks[1 - d][0].astype(F32))[0:1, :] for d in range(2)]
                   for j in range(grp)]
        per = []
        for j in range(grp):
            qc, kc, vc = qkv[j]
            row = []
            for d in range(2):
                incl, strict = masks[d]
                bb = gbs[j][d][1]
                g = gam[j][d]
                dec = jnp.where(incl, jnp.exp(jnp.minimum(g[:, 0:1] - gam_row[j][d], 0.0)), 0.0)
                a_mat = jnp.where(strict, bb[:, 0:1] * qkk[j][L:, :] * dec, 0.0)
                eg = jnp.exp(g)
                last = 0 if d else L - 1
                g_last = g[last:last + 1, :]
                rhs = jnp.concatenate([bb * vc, bb * kc * eg], axis=1)
                row.append((a_mat, rhs, qkk[j][0:L, :] * dec, qc * eg, kc * jnp.exp(g_last - g), g_last))
            per.append(row)
        packs = [jnp.concatenate([per[j][0][0], per[j][1][0], per[j + 1][0][0], per[j + 1][1][0]], axis=1)
                 for j in range(0, grp, 2)]
        inv = _inv_unit_packed_multi(packs)
        t_inv = [_to_bd(inv[j // 2][:, (j % 2) * 2 * L:(j % 2 + 1) * 2 * L]) for j in range(grp)]
        uw = [_bdot(t_inv[j], jnp.concatenate([per[j][0][1], per[j][1][1]], axis=0))
              for j in range(grp)]
        qk_uw = [_bdot(_to_bd(jnp.concatenate([per[j][0][2], per[j][1][2]], axis=1)), uw[j])
                 for j in range(grp)]
        ke_uw = [[_bdot_tn(per[j][d][4], uw[j][d * L:(d + 1) * L, :]) for d in range(2)]
                 for j in range(grp)]
        for j, c in enumerate(cs):
            srows = pl.ds(pl.multiple_of(c * GDN_K, GDN_K), GDN_K)
            drows = pl.ds(pl.multiple_of(c * 8, 8), 8)
            for d in range(2):
                half = slice(d * L, (d + 1) * L)
                mrows = pl.ds(pl.multiple_of(c * (L + GDN_K), L + GDN_K), L + GDN_K)
                ma_scr[d, mrows, :] = jnp.concatenate(
                    [per[j][d][3] - qk_uw[j][half, GDN_K:], -ke_uw[j][d][:, GDN_K:]], axis=0).astype(BF16)
                c2_scr[d, rows[j], :] = qk_uw[j][half, 0:GDN_K]
                bc_scr[d, srows, :] = ke_uw[j][d][:, 0:GDN_K]
                dg_scr[d, drows, :] = jnp.broadcast_to(jnp.exp(per[j][d][5]), (8, 128))
        return carry

    lax.fori_loop(0, nc // grp, prep, 0)

    chains = [(s, d) for s in range(nsub) for d in range(2)]
    for s, d in chains:
        state_scr[2 * s + d] = s0_ref[s, d, 0] if has_state else jnp.zeros((GDN_K, GDN_K), F32)

    def scan(i, carry):
        for s, d in chains:
            c = s * nc_seq + ((nc_seq - 1 - i) if d else i)
            rows = pl.ds(pl.multiple_of(c * L, L), L)
            srows = pl.ds(pl.multiple_of(c * GDN_K, GDN_K), GDN_K)
            drows = pl.ds(pl.multiple_of(c * 8, 8), 8)
            st = state_scr[2 * s + d]
            mrows = pl.ds(pl.multiple_of(c * (L + GDN_K), L + GDN_K), L + GDN_K)
            res = _dot(ma_scr[d, mrows, :], st.astype(BF16))
            oacc_scr[d, rows, :] = res[0:L, :] + c2_scr[d, rows, :]
            state_scr[2 * s + d] = dg_scr[d, drows, :][0:1, :] * st + res[L:, :] + bc_scr[d, srows, :]
        return carry

    lax.fori_loop(0, nc_seq, scan, 0)
    if not has_state:
        for s, d in chains:
            st_ref[s, d, 0] = state_scr[2 * s + d]

    o = oacc_scr[0] + oacc_scr[1]
    var = jnp.mean(o * o, axis=-1, keepdims=True)
    o_ref[...] = (o * lax.rsqrt(var + NORM_EPS) * ng_ref[...] * _silu(zb_ref[...])).astype(BF16)


def _gdn(proj, nseq, t, conv_w, bias_row, alog_row, norm_g, state, nsub=1):
    has_state = state is not None
    seq = lambda base: (lambda b, h: (b, base // 128 + h))
    const = lambda base: (lambda b, h: (0, base // 128 + h))
    rb = nsub * t
    in_specs = [
        pl.BlockSpec((rb, 128), seq(EV_Q)), pl.BlockSpec((rb, 128), seq(EV_K)), pl.BlockSpec((rb, 128), seq(EV_V)),
        pl.BlockSpec((rb, 128), seq(EV_ZB)), pl.BlockSpec((rb, 128), lambda b, h: (b, EV_SM // 128)),
        pl.BlockSpec((CONV_W, 128), const(0)), pl.BlockSpec((CONV_W, 128), const(1024)),
        pl.BlockSpec((CONV_W, 128), const(2048)),
        pl.BlockSpec((1, 128), lambda b, h: (0, 0)), pl.BlockSpec((1, 128), lambda b, h: (0, 0)),
        pl.BlockSpec((1, 128), lambda b, h: (0, 0)),
    ]
    args = [proj, proj, proj, proj, proj, conv_w, conv_w, conv_w, bias_row, alog_row, norm_g]
    st_block = (nsub, 2, 1, GDN_K, GDN_K)
    st_map = lambda b, h: (b, 0, h, 0, 0)
    out_specs = [pl.BlockSpec((rb, 128), lambda b, h: (b, h))]
    out_shape = [jax.ShapeDtypeStruct((nseq * t, GDN_HEADS * GDN_K), BF16)]
    if has_state:
        in_specs.append(pl.BlockSpec(st_block, st_map))
        args.append(state)
    else:
        out_specs.append(pl.BlockSpec(st_block, st_map))
        out_shape.append(jax.ShapeDtypeStruct((nseq, 2, GDN_HEADS, GDN_K, GDN_K), F32))
    nc = rb // L
    scratch = [pltpu.VMEM((t + 2 * CONV_PAD, 128), F32), pltpu.VMEM((rb, 128), F32), pltpu.VMEM((rb, 128), F32),
               pltpu.VMEM((rb, 128), F32), pltpu.VMEM((2, rb, 128), F32), pltpu.VMEM((2, rb, 128), F32),
               pltpu.VMEM((2, nc * (L + GDN_K), 128), BF16), pltpu.VMEM((2, rb, 128), F32),
               pltpu.VMEM((2, nc * GDN_K, GDN_K), F32),
               pltpu.VMEM((2, nc * 8, 128), F32),
               pltpu.VMEM((2, rb, 128), F32), pltpu.VMEM((2 * nsub, GDN_K, GDN_K), F32)]
    return pl.pallas_call(
        functools.partial(_gdn_kernel, t=t, nsub=nsub, has_state=has_state),
        grid=(nseq // nsub, GDN_HEADS), in_specs=in_specs, out_specs=out_specs, out_shape=out_shape,
        scratch_shapes=scratch, compiler_params=_cparams(("arbitrary", "arbitrary")), name="gdn",
    )(*args)


def _softmax_sink_pv(scores, sink, values):
    m = sink
    for s in scores:
        m = jnp.maximum(m, jnp.max(s, axis=-1, keepdims=True))
    den = jnp.exp(sink - m)
    ps = []
    for s in scores:
        p = jnp.exp(s - m)
        den = den + jnp.sum(p, axis=-1, keepdims=True)
        ps.append(p)
    out = None
    for p, v in zip(ps, values):
        term = _bdot(p / den, v)
        out = term if out is None else out + term
    return out


def _ctx_attn_kernel(sink_ref, q_ref, k_ref, v_ref, o_ref):
    kvh = pl.program_id(1)
    scale = ATT_DH ** -0.5
    kk, vv = k_ref[...], v_ref[...]
    for gq in range(ATT_GROUP):
        q = q_ref[:, gq * ATT_DH:(gq + 1) * ATT_DH]
        s = _bdot_nt(q, kk) * scale
        sink = sink_ref[kvh * ATT_GROUP + gq]
        o_ref[:, gq * ATT_DH:(gq + 1) * ATT_DH] = _softmax_sink_pv([s], sink, [vv]).astype(BF16)


def _ctx_attn(proj, nseq, t, sink):
    gw = ATT_GROUP * ATT_DH
    return pl.pallas_call(
        _ctx_attn_kernel,
        grid=(nseq, ATT_KVH),
        in_specs=[pl.BlockSpec(memory_space=pltpu.SMEM),
                  pl.BlockSpec((t, gw), lambda b, kv: (b, OD_Q // gw + kv)),
                  pl.BlockSpec((t, ATT_DH), lambda b, kv: (b, OD_K // ATT_DH + kv)),
                  pl.BlockSpec((t, ATT_DH), lambda b, kv: (b, OD_V // ATT_DH + kv))],
        out_specs=pl.BlockSpec((t, gw), lambda b, kv: (b, kv)),
        out_shape=jax.ShapeDtypeStruct((nseq * t, ATT_HEADS * ATT_DH), BF16),
        compiler_params=_cparams(("arbitrary", "arbitrary")), name="ctx_attn",
    )(sink, proj, proj, proj)


def _lat_attn_kernel(sink_ref, q_ref, k_ref, v_ref, kc_ref, vc_ref, cos_ref, sin_ref, o_ref, kp_scr, vp_scr, *, t):
    kvh = pl.program_id(1)
    scale = ATT_DH ** -0.5
    band = QBLOCK + 2 * WINDOW
    cos, sin = cos_ref[...], sin_ref[...]

    def rope(x):
        return x * cos + pltpu.roll(x, ATT_DH // 2, 1) * sin

    zeros = jnp.zeros((WINDOW, ATT_DH), F32)
    kp_scr[0:WINDOW, :] = zeros
    kp_scr[WINDOW + t:, :] = zeros
    kp_scr[WINDOW:WINDOW + t, :] = rope(k_ref[...])
    vp_scr[0:WINDOW, :] = zeros
    vp_scr[WINDOW + t:, :] = zeros
    vp_scr[WINDOW:WINDOW + t, :] = v_ref[...]
    kc, vc = kc_ref[...], vc_ref[...]

    def body(i, carry):
        start = pl.multiple_of(i * QBLOCK, QBLOCK)
        ki = kp_scr[pl.ds(start, band), :]
        vi = vp_scr[pl.ds(start, band), :]
        qpos = start + _iota((QBLOCK, band), 0)
        kpos = start - WINDOW + _iota((QBLOCK, band), 1)
        valid = (jnp.abs(qpos - kpos) <= WINDOW) & (kpos >= 0) & (kpos < t)
        cs = cos_ref[pl.ds(start, QBLOCK), :]
        sn = sin_ref[pl.ds(start, QBLOCK), :]
        for gq in range(ATT_GROUP):
            q = q_ref[pl.ds(start, QBLOCK), gq * ATT_DH:(gq + 1) * ATT_DH]
            q = q * cs + pltpu.roll(q, ATT_DH // 2, 1) * sn
            s_band = jnp.where(valid, _bdot_nt(q, ki) * scale, -jnp.inf)
            s_ctx = _bdot_nt(q, kc) * scale
            sink = sink_ref[kvh * ATT_GROUP + gq]
            o_ref[pl.ds(start, QBLOCK), gq * ATT_DH:(gq + 1) * ATT_DH] = _softmax_sink_pv(
                [s_band, s_ctx], sink, [vi, vc]).astype(BF16)
        return carry

    lax.fori_loop(0, t // QBLOCK, body, 0)


def _lat_attn(proj, nseq, t, sink, cache_k, cache_v, cos2, sin2):
    gw = ATT_GROUP * ATT_DH
    past = cache_k.shape[0] // nseq
    return pl.pallas_call(
        functools.partial(_lat_attn_kernel, t=t),
        grid=(nseq, ATT_KVH),
        in_specs=[pl.BlockSpec(memory_space=pltpu.SMEM),
                  pl.BlockSpec((t, gw), lambda b, kv: (b, OD_Q // gw + kv)),
                  pl.BlockSpec((t, ATT_DH), lambda b, kv: (b, OD_K // ATT_DH + kv)),
                  pl.BlockSpec((t, ATT_DH), lambda b, kv: (b, OD_V // ATT_DH + kv)),
                  pl.BlockSpec((past, ATT_DH), lambda b, kv: (b, kv)),
                  pl.BlockSpec((past, ATT_DH), lambda b, kv: (b, kv)),
                  pl.BlockSpec((t, ATT_DH), lambda b, kv: (0, 0)),
                  pl.BlockSpec((t, ATT_DH), lambda b, kv: (0, 0))],
        out_specs=pl.BlockSpec((t, gw), lambda b, kv: (b, kv)),
        out_shape=jax.ShapeDtypeStruct((nseq * t, ATT_HEADS * ATT_DH), BF16),
        scratch_shapes=[pltpu.VMEM((t + 2 * WINDOW, ATT_DH), F32), pltpu.VMEM((t + 2 * WINDOW, ATT_DH), F32)],
        compiler_params=_cparams(("arbitrary", "arbitrary")), name="lat_attn",
    )(sink, proj, proj, proj, cache_k, cache_v, cos2, sin2)


def _rope_tables(t):
    rows = t // GRID_W
    row = jnp.repeat(jnp.arange(rows), GRID_W).astype(F32)
    col = jnp.tile(jnp.arange(GRID_W), rows).astype(F32)
    n_freq = ATT_DH // 4
    inv = ROPE_BASE ** (-jnp.arange(n_freq, dtype=F32) / n_freq)
    ang = jnp.concatenate([row[:, None] * inv, col[:, None] * inv], axis=-1)
    cos, sin = jnp.cos(ang), jnp.sin(ang)
    return jnp.concatenate([cos, cos], axis=-1), jnp.concatenate([-sin, sin], axis=-1)


def _rwkv_kernel(*refs, t, nsub, has_state):
    (r_ref, k_ref, v_ref, wd_ref, ad_ref, gd_ref, mur_ref, muk_ref, muv_ref, muw_ref, mua_ref, mug_ref,
     w0_ref, wup_ref, a0_ref, aup_ref, gup_ref, kk_ref, ka_ref, rk_ref, lng_ref, lnb_ref) = refs[:22]
    k = 22
    s0_ref = st_ref = None
    if has_state:
        s0_ref = refs[k]
        k += 1
    o_ref = refs[k]
    k += 1
    if not has_state:
        st_ref = refs[k]
        k += 1
    (pad_scr, r_scr, v_scr, aa_scr, wl_scr, kd_scr, bb_scr, mah_scr, mal_scr, c2_scr, bc_scr, dg_scr,
     yacc_scr, state_scr) = refs[k:]
    nc_seq = t // L
    nc = nsub * nc_seq

    def shift_mix(x_ref, mu_ref):
        parts = []
        for s in range(nsub):
            x = x_ref[s * t:(s + 1) * t, :]
            zeros = jnp.zeros((CONV_PAD, 128), F32)
            pad_scr[0:CONV_PAD, :] = zeros
            pad_scr[CONV_PAD + t:, :] = zeros
            pad_scr[CONV_PAD:CONV_PAD + t, :] = x
            nb = 0.5 * (pad_scr[CONV_PAD - 1:CONV_PAD - 1 + t, :] + pad_scr[CONV_PAD + 1:CONV_PAD + 1 + t, :])
            parts.append(x + mu_ref[...] * (nb - x))
        return parts[0] if nsub == 1 else jnp.concatenate(parts, axis=0)

    r = shift_mix(r_ref, mur_ref)
    kx = shift_mix(k_ref, muk_ref)
    v = shift_mix(v_ref, muv_ref)
    wd = shift_mix(wd_ref, muw_ref)
    ad = shift_mix(ad_ref, mua_ref)
    gd = shift_mix(gd_ref, mug_ref)

    lane_sq = _iota((128, 128), 1)
    row_sq = _iota((128, 128), 0)
    same_head = ((lane_sq >= RW_K) == (row_sq >= RW_K))
    bd_ones = same_head.astype(F32)

    def head_sum(x):
        return _dot(jnp.concatenate(_split2(x), axis=1), jnp.concatenate([bd_ones, bd_ones], axis=0).astype(BF16))

    gate = _bdot(jax.nn.sigmoid(gd), gup_ref[...])
    kkv = kx * kk_ref[...]
    kkn = kkv * lax.rsqrt(head_sum(kkv * kkv) + NORM_EPS)
    tw = jnp.tanh(wd)
    rows_lo = _iota((128, 128), 0) < RW_K
    kd_sum = jnp.zeros((nsub * t, 128), F32)
    for d in range(2):
        half = rows_lo if d == 0 else jnp.logical_not(rows_lo)
        w = w0_ref[d:d + 1, :] + _bdot(tw, jnp.where(half, wup_ref[...], 0.0))
        wl_scr[d] = -jnp.exp(-_softplus(-w) - 0.5)
        a = jax.nn.sigmoid(a0_ref[d:d + 1, :] + _bdot(ad, jnp.where(half, aup_ref[...], 0.0)))
        kd = kx * (1.0 + (a - 1.0) * ka_ref[...])
        kd_scr[d] = kd
        kd_sum = kd_sum + kd
        bb_scr[d] = kkn * a
    r_scr[...] = r
    v_scr[...] = v
    aa_scr[...] = -kkn

    dup = ((_iota((RW_K, 128), 1) & (RW_K - 1)) == _iota((RW_K, 128), 0)).astype(F32)
    row_head1 = _iota((2 * L, 128), 0) >= L
    lane_head1 = _iota((2 * L, 128), 1) >= RW_K
    own_lanes = row_head1 == lane_head1
    sq_r, sq_c = _iota((2 * L, 2 * L), 0), _iota((2 * L, 2 * L), 1)
    same_blk = (sq_r >= L) == (sq_c >= L)
    pos_r, pos_c = sq_r & (L - 1), sq_c & (L - 1)
    ones_l = jnp.ones((L, 128), F32)
    bd_masks = [(same_blk & (pos_c <= pos_r), same_blk & (pos_c < pos_r)),
                (same_blk & (pos_c >= pos_r), same_blk & (pos_c > pos_r))]
    mm, mm_nt, mm_tn = _bdot, _bdot_nt, _bdot_tn
    grp = min(RW_GROUP, nc)

    def two(x):
        return jnp.where(own_lanes, jnp.concatenate([x, x], axis=0), 0.0)

    def prep(gi, carry):
        cs = [gi * grp + j for j in range(grp)]
        cds = [(j, d) for j in range(grp) for d in range(2)]
        rows = [pl.ds(pl.multiple_of(c * L, L), L) for c in cs]
        srows = [pl.ds(pl.multiple_of(c * 2 * L, 2 * L), 2 * L) for c in cs]
        vv2 = [two(v_scr[r, :]) for r in rows]
        wl = {(j, d): wl_scr[d, rows[j], :] for j, d in cds}
        cin = {jd: _dot01_l(_tri_masks(jd[1] == 1)[0].astype(F32), wl[jd]) for jd in cds}
        decay = {jd: jnp.exp(_dot01_tn(wl[jd], ones_l)) for jd in cds}
        ops = {}
        for j, d in cds:
            e_in, e_out = jnp.exp(cin[j, d]), jnp.exp(-cin[j, d])
            ops[j, d] = (two(aa_scr[rows[j], :] * jnp.exp(cin[j, d] - wl[j, d])), two(r_scr[rows[j], :] * e_in),
                         two(bb_scr[d, rows[j], :] * e_out), two(kd_scr[d, rows[j], :] * e_out))
        prod = {jd: mm_nt(jnp.concatenate(ops[jd][0:2], axis=0), jnp.concatenate(ops[jd][2:4], axis=0)) for jd in cds}
        blk = {}
        for j, d in cds:
            incl_bd, strict_bd = bd_masks[d]
            p = prod[j, d]
            blk[j, d] = (jnp.where(strict_bd, p[0:2 * L, 0:2 * L], 0.0), jnp.where(strict_bd, p[0:2 * L, 2 * L:], 0.0),
                         jnp.where(incl_bd, p[2 * L:, 0:2 * L], 0.0), jnp.where(incl_bd, p[2 * L:, 2 * L:], 0.0))
        packs = [-jnp.concatenate([blk[j, d][0][0:L, :] + blk[j, d][0][L:, :] for d in range(2)], axis=1)
                 for j in range(grp)]
        inv = _inv_unit_packed_multi(packs)
        t_inv = {(j, d): _to_bd(inv[j][:, d * 2 * L:(d + 1) * 2 * L]) for j, d in cds}
        kv = {(j, d): mm(jnp.concatenate(blk[j, d][1::2], axis=0), vv2[j]) for j, d in cds}
        m1c1 = {jd: mm(t_inv[jd], jnp.concatenate([ops[jd][0], kv[jd][0:2 * L, :]], axis=1)) for jd in cds}
        rm = {jd: mm(blk[jd][2], m1c1[jd]) for jd in cds}
        bm = {jd: mm_tn(ops[jd][2], m1c1[jd]) for jd in cds}
        kv2 = {(j, d): mm_tn(ops[j, d][3], vv2[j]) for j, d in cds}
        for j, d in cds:
            m2ah = jnp.concatenate([ops[j, d][1] + rm[j, d][:, 0:128], decay[j, d] * bm[j, d][:, 0:128]], axis=0)
            hi, lo = _split2(m2ah)
            mrows = pl.ds(pl.multiple_of(cs[j] * 4 * L, 4 * L), 4 * L)
            mah_scr[d, mrows, :] = hi
            mal_scr[d, mrows, :] = lo
            c2_scr[d, srows[j], :] = rm[j, d][:, 128:] + kv[j, d][2 * L:, :]
            bc_scr[d, srows[j], :] = decay[j, d] * (bm[j, d][:, 128:] + kv2[j, d])
            dg_scr[d, srows[j], :] = decay[j, d]
        return carry

    lax.fori_loop(0, nc // grp, prep, 0)

    chains = [(s, d) for s in range(nsub) for d in range(2)]
    for s, d in chains:
        if has_state:
            stacked = s0_ref[s, d].reshape(2 * RW_K, RW_K)
            state_scr[2 * s + d] = jnp.where(same_head, _dot01_r(stacked, dup), 0.0).T
        else:
            state_scr[2 * s + d] = jnp.zeros((128, 128), F32)

    def scan(i, carry):
        for s, d in chains:
            c = s * nc_seq + ((nc_seq - 1 - i) if d else i)
            rows = pl.ds(pl.multiple_of(c * L, L), L)
            srows = pl.ds(pl.multiple_of(c * 2 * L, 2 * L), 2 * L)
            h_st = state_scr[2 * s + d]
            mrows = pl.ds(pl.multiple_of(c * 4 * L, 4 * L), 4 * L)
            lh, ll = mah_scr[d, mrows, :], mal_scr[d, mrows, :]
            hh, hl = _split2(h_st)
            res = _dot(lh, hh)
            corr = res[2 * L:, :] + _dot(lh[2 * L:, :], hl) + _dot(ll[2 * L:, :], hh)
            y2 = res[0:2 * L, :] + c2_scr[d, srows, :]
            yacc_scr[d, rows, :] = y2[0:L, :] + y2[L:, :]
            state_scr[2 * s + d] = dg_scr[d, srows, :] * h_st + corr + bc_scr[d, srows, :]
        return carry

    lax.fori_loop(0, nc_seq, scan, 0)
    if not has_state:
        dup_t = ((_iota((128, RW_K), 0) & (RW_K - 1)) == _iota((128, RW_K), 1)).astype(F32)
        for s, d in chains:
            st_ref[s, d] = _dot01_r(state_scr[2 * s + d].T, dup_t).reshape(2, RW_K, RW_K)

    y = yacc_scr[0] + yacc_scr[1]
    mean = head_sum(y) * (1.0 / RW_K)
    yc = y - mean
    var = head_sum(yc * yc) * (1.0 / RW_K)
    yn = yc * lax.rsqrt(var + RWKV_LN_EPS) * lng_ref[...] + lnb_ref[...]
    bonus = head_sum(r * (0.5 * kd_sum) * rk_ref[...]) * v
    o_ref[...] = ((yn + bonus) * gate).astype(BF16)


def _rwkv(proj, nseq, t, prm, state, nsub=1):
    has_state = state is not None
    mu, w0, wup, a0, aup, gup, k_k, k_a, r_k, ln_g, ln_b = prm
    seq = lambda base: (lambda b, hp: (b, base // 128 + hp))
    fix = lambda base: (lambda b, hp: (b, base // 128))
    mu_at = lambda base: (lambda b, hp: (0, (base - OD_R) // 128 + hp))
    mu_fix = lambda base: (lambda b, hp: (0, (base - OD_R) // 128))
    col = lambda b, hp: (0, hp)
    rb = nsub * t
    in_specs = [
        pl.BlockSpec((rb, 128), seq(OD_R)), pl.BlockSpec((rb, 128), seq(OD_RK)), pl.BlockSpec((rb, 128), seq(OD_RV)),
        pl.BlockSpec((rb, 128), fix(OD_WD)), pl.BlockSpec((rb, 128), fix(OD_AD)), pl.BlockSpec((rb, 128), fix(OD_GD)),
        pl.BlockSpec((1, 128), mu_at(OD_R)), pl.BlockSpec((1, 128), mu_at(OD_RK)), pl.BlockSpec((1, 128), mu_at(OD_RV)),
        pl.BlockSpec((1, 128), mu_fix(OD_WD)), pl.BlockSpec((1, 128), mu_fix(OD_AD)), pl.BlockSpec((1, 128), mu_fix(OD_GD)),
        pl.BlockSpec((2, 128), col), pl.BlockSpec((128, 128), col),
        pl.BlockSpec((2, 128), col), pl.BlockSpec((128, 128), col),
        pl.BlockSpec((128, 128), col),
        pl.BlockSpec((1, 128), col), pl.BlockSpec((1, 128), col), pl.BlockSpec((1, 128), col),
        pl.BlockSpec((1, 128), col), pl.BlockSpec((1, 128), col),
    ]
    args = [proj] * 6 + [mu] * 6 + [w0, wup, a0, aup, gup, k_k, k_a, r_k, ln_g, ln_b]
    st_block = (nsub, 2, 2, RW_K, RW_K)
    st_map = lambda b, hp: (b, 0, hp, 0, 0)
    out_specs = [pl.BlockSpec((rb, 128), lambda b, hp: (b, hp))]
    out_shape = [jax.ShapeDtypeStruct((nseq * t, RW_HEADS * RW_K), BF16)]
    if has_state:
        in_specs.append(pl.BlockSpec(st_block, st_map))
        args.append(state)
    else:
        out_specs.append(pl.BlockSpec(st_block, st_map))
        out_shape.append(jax.ShapeDtypeStruct((nseq, 2, RW_HEADS, RW_K, RW_K), F32))
    scratch = [pltpu.VMEM((t + 2 * CONV_PAD, 128), F32), pltpu.VMEM((rb, 128), F32), pltpu.VMEM((rb, 128), F32),
               pltpu.VMEM((rb, 128), F32), pltpu.VMEM((2, rb, 128), F32), pltpu.VMEM((2, rb, 128), F32),
               pltpu.VMEM((2, rb, 128), F32)] + [pltpu.VMEM((2, 4 * rb, 128), BF16)] * 2 + [
               pltpu.VMEM((2, 2 * rb, 128), F32)] * 3 + [
               pltpu.VMEM((2, rb, 128), F32), pltpu.VMEM((2 * nsub, 128, 128), F32)]
    return pl.pallas_call(
        functools.partial(_rwkv_kernel, t=t, nsub=nsub, has_state=has_state),
        grid=(nseq // nsub, RW_HEADS // 2), in_specs=in_specs, out_specs=out_specs, out_shape=out_shape,
        scratch_shapes=scratch, compiler_params=_cparams(("arbitrary", "arbitrary")), name="rwkv",
    )(*args)


def _even_w_in(w):
    za, xs, bm, cm = w[:, 0:1024], w[:, 1024:2048], w[:, 2048:2304], w[:, 2304:2560]
    dt, q, kk, v = w[:, 2560:2592], w[:, 2592:3616], w[:, 3616:4640], w[:, 4640:5664]
    zb, al, be = w[:, 5664:6688], w[:, 6688:6704], w[:, 6704:6720]
    pad = jnp.zeros((w.shape[0], EV_COLS - EV_SM - 64), w.dtype)
    return jnp.concatenate([za, xs, zb, q, kk, v, bm, cm, dt, al, be, pad], axis=1)


def _pad_row(parts):
    flat = jnp.concatenate([p.reshape(-1) for p in parts])
    return jnp.pad(flat, (0, 128 - flat.shape[0])).reshape(1, 128)


def kernel(x_prompt, x_sample, state_ssd, state_gdn, cache_k, cache_v, state_rwkv, c, c_ctx, ada_w, ada_b, norm1_g, norm2_g, ffn_w1, ffn_w2, final_g, ev_w_in, ev_w_out, ssd_conv_w, ssd_conv_b, ssd_a_log, ssd_dt_bias, ssd_d, ssd_norm_g, gdn_conv_w, gdn_a_log, gdn_dt_bias, gdn_norm_g, od_w_in, od_w_out, attn_sink, rwkv_mu, rwkv_w0, rwkv_w_up, rwkv_a0, rwkv_a_up, rwkv_g_up, rwkv_k_k, rwkv_k_a, rwkv_r_k, rwkv_ln_g, rwkv_ln_b):
    bp, tp, _ = x_prompt.shape
    bs, ts, _ = x_sample.shape
    streams = [(x_prompt.reshape(bp * tp, D), bp, tp, None), (x_sample.reshape(bs * ts, D), bs, ts, ts)]

    cvec = jnp.concatenate([c_ctx[None, :], c, jnp.zeros((16 - 1 - bs, D), F32)], axis=0)
    mod = _adaln(cvec, ada_w, ada_b)

    w_in0 = _even_w_in(ev_w_in[0].astype(BF16))
    w_out0 = ev_w_out[0].astype(BF16)
    bias_row = _pad_row([ssd_dt_bias[0], gdn_dt_bias[0]])
    alog_row = _pad_row([ssd_a_log[0], gdn_a_log[0]])
    dskip_row = jnp.repeat(ssd_d[0], SSD_P).reshape(1, SSD_HEADS * SSD_P)
    w1_0, w2_0 = ffn_w1[0].astype(BF16), ffn_w2[0].astype(BF16)
    w_in1 = od_w_in[0].astype(BF16)
    w_out1 = od_w_out[0].astype(BF16)
    w1_1, w2_1 = ffn_w1[1].astype(BF16), ffn_w2[1].astype(BF16)
    rw_prm = (rwkv_mu[0].reshape(1, -1), rwkv_w0[0], rwkv_w_up[0].reshape(2 * 64, -1), rwkv_a0[0],
              rwkv_a_up[0].reshape(2 * 64, -1), rwkv_g_up[0], rwkv_k_k[0].reshape(1, -1), rwkv_k_a[0].reshape(1, -1),
              rwkv_r_k[0].reshape(1, -1), rwkv_ln_g[0].reshape(1, -1), rwkv_ln_b[0].reshape(1, -1))
    cos2, sin2 = _rope_tables(ts)
    past = cache_k.shape[2]
    ck = cache_k[:, 0].reshape(bs * past, ATT_KVH * ATT_DH)
    cv = cache_v[:, 0].reshape(bs * past, ATT_KVH * ATT_DH)

    outs = []
    for x, nseq, t, sample_t in streams:
        latent = sample_t is not None
        m0 = mod[0]
        proj = _inproj(x, norm1_g[0:1], m0, w_in0, sample_t)
        s_ssd = state_ssd[:, 0] if latent else None
        s_gdn = state_gdn[:, 0] if latent else None
        r_ssd = _ssd(proj, nseq, t, ssd_conv_w[0], ssd_conv_b[0].reshape(1, -1), bias_row, alog_row, dskip_row,
                     ssd_norm_g[0].reshape(1, -1), s_ssd)
        r_gdn = _gdn(proj, nseq, t, gdn_conv_w[0], bias_row, alog_row, gdn_norm_g[0].reshape(1, -1), s_gdn,
                     nsub=1 if latent else GDN_PROMPT_NSUB)
        x = _mix_ffn(r_ssd[0], r_gdn[0], x, norm2_g[0:1], m0, w_out0, w1_0, w2_0, final_g.reshape(1, D), sample_t,
                     final=False)
        m1 = mod[1]
        proj = _inproj(x, norm1_g[1:2], m1, w_in1, sample_t)
        if latent:
            att = _lat_attn(proj, nseq, t, attn_sink[0], ck, cv, cos2, sin2)
            r_rw = _rwkv(proj, nseq, t, rw_prm, state_rwkv[:, 0])
        else:
            att = _ctx_attn(proj, nseq, t, attn_sink[0])
            r_rw = _rwkv(proj, nseq, t, rw_prm, None, nsub=RW_PROMPT_NSUB)
        x = _mix_ffn(att, r_rw[0], x, norm2_g[1:2], m1, w_out1, w1_1, w2_1, final_g.reshape(1, D), sample_t,
                     final=True)
        outs.append((x.reshape(nseq, t, D), r_ssd, r_gdn, proj, r_rw))

    (y_prompt, p_ssd, p_gdn, p_proj, p_rw), (y_sample, _, _, _, _) = outs
    new_k = p_proj[:, OD_K:OD_V].reshape(bp, 1, tp, ATT_KVH, ATT_DH)
    new_v = p_proj[:, OD_V:OD_R].reshape(bp, 1, tp, ATT_KVH, ATT_DH)
    return (y_prompt, y_sample, p_ssd[1][:, None], p_gdn[1][:, None], new_k, new_v, p_rw[1][:, None])
```
